```python
import math
import jax, jax.numpy as jnp
from jax import lax
import numpy as np

D_MODEL = 1024
BATCH = 16
SEQ = 2048
DEPTH = 1

D_MIX = 2 * D_MODEL
M_WIDTH = D_MIX // 2
M_HEADS = 4
M_HD = M_WIDTH // M_HEADS
M_CHUNK = 64
CONV_K = 4
A_WIDTH = D_MIX - M_WIDTH
A_HEADS = 8
A_VHD = A_WIDTH // A_HEADS
A_QKD = A_VHD // 2
Q_BLOCK = 128
N_BUCKETS = 32
MAX_DIST = 128
EPS = 1e-6

SPLIT_SIZES = (M_WIDTH, M_WIDTH, M_WIDTH, M_WIDTH, M_WIDTH, M_HEADS, M_HEADS,
               A_WIDTH, A_WIDTH, A_WIDTH, A_WIDTH)
D_IN = 5 * M_WIDTH + 2 * M_HEADS + 4 * A_WIDTH

kernel_name = "hymba_mlstm_diffattn_hybrid"


def rmsnorm(x, w):
    xf = x.astype(jnp.float32)
    y = xf * lax.rsqrt(jnp.mean(xf * xf, axis=-1, keepdims=True) + EPS)
    return (y * w.astype(jnp.float32)).astype(x.dtype)


def head_layernorm(h, w):
    mu = jnp.mean(h, axis=-1, keepdims=True)
    var = jnp.mean(jnp.square(h - mu), axis=-1, keepdims=True)
    return (h - mu) * lax.rsqrt(var + EPS) * w.astype(jnp.float32)


def causal_dwconv(x, w):
    C = x.shape[-1]
    return lax.conv_general_dilated(
        x, w[:, None, :].astype(x.dtype), window_strides=(1,),
        padding=[(CONV_K - 1, 0)], dimension_numbers=('NWC', 'WIO', 'NWC'),
        feature_group_count=C)


def t5_bucket(q_pos, k_pos):
    n = jnp.maximum(q_pos[:, None] - k_pos[None, :], 0)
    max_exact = N_BUCKETS // 2
    nf = jnp.maximum(n, 1).astype(jnp.float32)
    large = max_exact + (jnp.log(nf / max_exact) / math.log(MAX_DIST / max_exact)
                         * (N_BUCKETS - max_exact)).astype(jnp.int32)
    large = jnp.minimum(large, N_BUCKETS - 1)
    return jnp.where(n < max_exact, n, large)


def mlstm_chunkwise(q, k, v, log_i, log_f):
    Bn, S, H, DH = q.shape
    NC = S // M_CHUNK

    def to_chunks(a):
        a = a.reshape((Bn, NC, M_CHUNK) + a.shape[2:])
        a = jnp.moveaxis(a, 3, 2)
        return jnp.moveaxis(a, 1, 0)

    qc = to_chunks(q) * (DH ** -0.5)
    kc = to_chunks(k)
    vc = to_chunks(v)
    lic = to_chunks(log_i)
    bc = jnp.cumsum(to_chunks(log_f), axis=-1)
    mask = jnp.tril(jnp.ones((M_CHUNK, M_CHUNK), dtype=bool))

    def step(carry, inp):
        C, n, m = carry
        qb, kb, vb, lib, bb = inp
        D = bb[..., :, None] - bb[..., None, :] + lib[..., None, :]
        D = jnp.where(mask, D, -jnp.inf)
        inter = bb + m[..., None]
        m_t = jnp.maximum(jnp.max(D, axis=-1), inter)
        Dw = jnp.exp(D - m_t[..., None])
        inter_w = jnp.exp(inter - m_t)
        s_qk = jnp.einsum('bhtd,bhsd->bhts', qb, kb) * Dw
        num = (inter_w[..., None] * jnp.einsum('bhed,bhtd->bhte', C, qb)
               + jnp.einsum('bhts,bhse->bhte', s_qk, vb))
        den = inter_w * jnp.einsum('bhd,bhtd->bht', n, qb) + jnp.sum(s_qk, axis=-1)
        h = num / jnp.maximum(jnp.abs(den), jnp.exp(-m_t))[..., None]
        bL = bb[..., -1]
        g = bL[..., None] - bb + lib
        m_new = jnp.maximum(bL + m, jnp.max(g, axis=-1))
        decay = jnp.exp(bL + m - m_new)
        w = jnp.exp(g - m_new[..., None])
        C_new = decay[..., None, None] * C + jnp.einsum('bhs,bhse,bhsd->bhed', w, vb, kb)
        n_new = decay[..., None] * n + jnp.einsum('bhs,bhsd->bhd', w, kb)
        return (C_new, n_new, m_new), h

    init = (jnp.zeros((Bn, H, DH, DH), jnp.float32),
            jnp.zeros((Bn, H, DH), jnp.float32),
            jnp.zeros((Bn, H), jnp.float32))
    _, hs = lax.scan(step, init, (qc, kc, vc, lic, bc))
    hs = jnp.transpose(hs, (1, 0, 3, 2, 4))
    return hs.reshape(Bn, S, H, DH)


def diff_attention(q, k, v, lam, rel_bias):
    S = q.shape[1]
    q = q * (A_QKD ** -0.5)
    outs = []
    for blk in range(S // Q_BLOCK):
        q0 = blk * Q_BLOCK
        ke = q0 + Q_BLOCK
        qb, kb, vb = q[:, q0:ke], k[:, :ke], v[:, :ke]
        logits = jnp.einsum('bqhmd,bkhmd->bmhqk', qb, kb).astype(jnp.float32)
        q_pos = jnp.arange(q0, ke)
        k_pos = jnp.arange(ke)
        bias = jnp.transpose(rel_bias[t5_bucket(q_pos, k_pos)], (2, 0, 1)).astype(jnp.float32)
        causal = k_pos[None, :] <= q_pos[:, None]
        logits = jnp.where(causal, logits + bias, -jnp.inf)
        p = jax.nn.softmax(logits, axis=-1)
        w = p[:, 0] - lam * p[:, 1]
        outs.append(jnp.einsum('bhqk,bkhe->bqhe', w, vb.astype(jnp.float32)))
    return jnp.concatenate(outs, axis=1)


def hybrid_layer(x, c, layer_idx, norm_w, w_ada, b_ada, w_in, b_i, b_f, conv_q_w,
                 conv_k_w, m_norm_w, lq1, lk1, lq2, lk2, a_norm_w, rel_bias, w_out):
    Bn, S, _ = x.shape
    mod = jax.nn.silu(c) @ w_ada + b_ada
    shift, scale, gate = jnp.split(mod, 3, axis=-1)
    h = rmsnorm(x, norm_w) * (1 + scale[:, None, :]) + shift[:, None, :]

    proj = h @ w_in
    split_pts = np.cumsum(SPLIT_SIZES)[:-1].tolist()
    mq, mk, mv, mo, mz, mi, mf, aq, ak, av, az = jnp.split(proj, split_pts, axis=-1)

    mq = jax.nn.silu(causal_dwconv(mq, conv_q_w))
    mk = jax.nn.silu(causal_dwconv(mk, conv_k_w))
    hs4 = (Bn, S, M_HEADS, M_HD)
    log_i = (mi + b_i).astype(jnp.float32)
    log_f = jax.nn.log_sigmoid((mf + b_f).astype(jnp.float32))
    hm = mlstm_chunkwise(mq.astype(jnp.float32).reshape(hs4),
                         mk.astype(jnp.float32).reshape(hs4),
                         mv.astype(jnp.float32).reshape(hs4), log_i, log_f)
    hm = head_layernorm(hm, m_norm_w.reshape(M_HEADS, M_HD))
    hm = jax.nn.sigmoid(mo.astype(jnp.float32)).reshape(hs4) * hm
    hm = hm.reshape(Bn, S, M_WIDTH).astype(x.dtype) * jax.nn.silu(mz)

    lam_init = 0.8 - 0.6 * math.exp(-0.3 * layer_idx)
    lam = (jnp.exp(jnp.sum(lq1.astype(jnp.float32) * lk1.astype(jnp.float32)))
           - jnp.exp(jnp.sum(lq2.astype(jnp.float32) * lk2.astype(jnp.float32))) + lam_init)
    ha = diff_attention(aq.reshape(Bn, S, A_HEADS, 2, A_QKD),
                        ak.reshape(Bn, S, A_HEADS, 2, A_QKD),
                        av.reshape(Bn, S, A_HEADS, A_VHD), lam, rel_bias)
    ha = rmsnorm(ha, a_norm_w) * (1 - lam_init)
    ha = ha.reshape(Bn, S, A_WIDTH).astype(x.dtype) * jax.nn.silu(az)

    y = jnp.concatenate([hm, ha], axis=-1) @ w_out
    return x + gate[:, None, :] * y


def setup_inputs(seed: int = 0) -> dict:
    key = jax.random.key(seed)
    ks = jax.random.split(key, 20)
    f32 = jnp.float32
    nrm = lambda k, s: jax.random.normal(k, s, f32)
    return {
        "x": nrm(ks[0], (BATCH, SEQ, D_MODEL)),
        "c": nrm(ks[1], (BATCH, D_MODEL)),
        "norm_w": 1.0 + 0.02 * nrm(ks[2], (DEPTH, D_MODEL)),
        "w_ada": 0.3 * D_MODEL ** -0.5 * nrm(ks[3], (DEPTH, D_MODEL, 3 * D_MODEL)),
        "b_ada": 0.01 * nrm(ks[4], (DEPTH, 3 * D_MODEL)),
        "w_in": D_MODEL ** -0.5 * nrm(ks[5], (DEPTH, D_MODEL, D_IN)),
        "b_i": 0.1 * nrm(ks[6], (DEPTH, M_HEADS)),
        "b_f": jnp.linspace(3.0, 6.0, M_HEADS, dtype=f32)[None, :] + 0.1 * nrm(ks[7], (DEPTH, M_HEADS)),
        "conv_q_w": CONV_K ** -0.5 * nrm(ks[8], (DEPTH, CONV_K, M_WIDTH)),
        "conv_k_w": CONV_K ** -0.5 * nrm(ks[9], (DEPTH, CONV_K, M_WIDTH)),
        "m_norm_w": 1.0 + 0.02 * nrm(ks[10], (DEPTH, M_WIDTH)),
        "lambda_q1": 0.1 * nrm(ks[11], (DEPTH, A_QKD)),
        "lambda_k1": 0.1 * nrm(ks[12], (DEPTH, A_QKD)),
        "lambda_q2": 0.1 * nrm(ks[13], (DEPTH, A_QKD)),
        "lambda_k2": 0.1 * nrm(ks[14], (DEPTH, A_QKD)),
        "a_norm_w": 1.0 + 0.02 * nrm(ks[15], (DEPTH, A_VHD)),
        "rel_bias": 0.5 * nrm(ks[16], (N_BUCKETS, A_HEADS)),
        "w_out": D_MIX ** -0.5 * nrm(ks[17], (DEPTH, D_MIX, D_MODEL)),
        "final_norm_w": 1.0 + 0.02 * nrm(ks[18], (D_MODEL,)),
    }


def reference(x, c, norm_w, w_ada, b_ada, w_in, b_i, b_f, conv_q_w, conv_k_w,
              m_norm_w, lambda_q1, lambda_k1, lambda_q2, lambda_k2, a_norm_w,
              rel_bias, w_out, final_norm_w):
    for l in range(DEPTH):
        x = hybrid_layer(x, c, l, norm_w[l], w_ada[l], b_ada[l], w_in[l], b_i[l], b_f[l],
                         conv_q_w[l], conv_k_w[l], m_norm_w[l], lambda_q1[l], lambda_k1[l],
                         lambda_q2[l], lambda_k2[l], a_norm_w[l], rel_bias, w_out[l])
    return rmsnorm(x, final_norm_w)
```

```python
import functools
import math

import numpy as np
import jax
import jax.numpy as jnp
from jax import lax
from jax.experimental import pallas as pl
from jax.experimental.pallas import tpu as pltpu

F32 = jnp.float32
BF16 = jnp.bfloat16

D_MODEL = 1024
M_WIDTH = 1024
M_HEADS = 4
M_HD = 256
CONV_K = 4
A_WIDTH = 1024
A_HEADS = 8
A_VHD = 128
A_QKD = 64
N_BUCKETS = 32
MAX_DIST = 128
EPS = 1e-6
LAM_INIT = 0.8 - 0.6 * math.exp(-0.3 * 0)

N_MAIN = 5 * M_WIDTH + 4 * A_WIDTH
GATE_COL0 = 5 * M_WIDTH
GATE_PAD = 128

M_CHUNK = 256
ATT_BLK = 256
NEG = -1e30

VMEM_LIMIT = 48 * 1024 * 1024


def _nt_dot(a, b):
    return lax.dot_general(a, b, (((1,), (1,)), ((), ())), preferred_element_type=F32)


def _tn_dot(a, b):
    return lax.dot_general(a, b, (((0,), (0,)), ((), ())), preferred_element_type=F32)


def _sigmoid(x):
    return 1.0 / (1.0 + jnp.exp(-x))


def _adaln_kernel(c_ref, w_ref, b_ref, o_ref):
    c = c_ref[...]
    sc = (c * _sigmoid(c)).astype(BF16)
    o_ref[0] = jnp.dot(sc, w_ref[...].astype(BF16), preferred_element_type=F32) + b_ref[0]


def _adaln(c, w_ada, b_ada):
    B, D = c.shape
    return pl.pallas_call(
        _adaln_kernel,
        grid=(3,),
        in_specs=[
            pl.BlockSpec((B, D), lambda j: (0, 0)),
            pl.BlockSpec((D, D), lambda j: (0, j)),
            pl.BlockSpec((1, 1, D), lambda j: (j, 0, 0)),
        ],
        out_specs=pl.BlockSpec((1, B, D), lambda j: (j, 0, 0)),
        out_shape=jax.ShapeDtypeStruct((3, B, D), F32),
        compiler_params=pltpu.CompilerParams(
            dimension_semantics=("arbitrary",), vmem_limit_bytes=VMEM_LIMIT),
        name="adaln",
    )(c, w_ada, b_ada.reshape(3, 1, D))


def _inproj_kernel(x_ref, shift_ref, scale_ref, nw_ref, w_ref, wg_ref, proj_ref, gates_ref, h_scr):
    j = pl.program_id(1)

    @pl.when(j == 0)
    def _():
        x = x_ref[...]
        y = x * lax.rsqrt(jnp.mean(x * x, axis=-1, keepdims=True) + EPS) * nw_ref[...]
        h = (y * (1.0 + scale_ref[0]) + shift_ref[0]).astype(BF16)
        h_scr[...] = h
        gates_ref[...] = jnp.dot(h, wg_ref[...], preferred_element_type=F32)

    proj_ref[...] = jnp.dot(h_scr[...], w_ref[...], preferred_element_type=F32).astype(BF16)


def _inproj(x2d, mod3, norm_w, w_main, w_gate, B, S, tm=1024, tn=1024):
    M, D = x2d.shape
    N = w_main.shape[1]
    tiles_per_seq = S // tm
    return pl.pallas_call(
        _inproj_kernel,
        grid=(M // tm, N // tn),
        in_specs=[
            pl.BlockSpec((tm, D), lambda i, j: (i, 0)),
            pl.BlockSpec((1, 1, D), lambda i, j: (i // tiles_per_seq, 0, 0)),
            pl.BlockSpec((1, 1, D), lambda i, j: (B + i // tiles_per_seq, 0, 0)),
            pl.BlockSpec((1, D), lambda i, j: (0, 0)),
            pl.BlockSpec((D, tn), lambda i, j: (0, j)),
            pl.BlockSpec((D, GATE_PAD), lambda i, j: (0, 0)),
        ],
        out_specs=[
            pl.BlockSpec((tm, tn), lambda i, j: (i, j)),
            pl.BlockSpec((tm, GATE_PAD), lambda i, j: (i, 0)),
        ],
        out_shape=[
            jax.ShapeDtypeStruct((M, N), BF16),
            jax.ShapeDtypeStruct((M, GATE_PAD), F32),
        ],
        scratch_shapes=[pltpu.VMEM((tm, D), BF16)],
        compiler_params=pltpu.CompilerParams(
            dimension_semantics=("arbitrary", "arbitrary"), vmem_limit_bytes=VMEM_LIMIT),
        name="inproj",
    )(x2d, mod3, mod3, norm_w, w_main, w_gate)


def _mlstm_kernel(mq_ref, mk_ref, mv_ref, mo_ref, mz_ref, g_ref, cqw_ref, ckw_ref, nw_ref, gb_ref,
                  o_ref, ct_scr, n_scr, m_scr, qbuf, kbuf):
    L = M_CHUNK
    c = pl.program_id(1)

    @pl.when(c == 0)
    def _():
        ct_scr[...] = jnp.zeros_like(ct_scr)
        n_scr[...] = jnp.zeros_like(n_scr)
        m_scr[...] = jnp.zeros_like(m_scr)
        qbuf[0:8, :] = jnp.zeros((8, M_WIDTH), F32)
        kbuf[0:8, :] = jnp.zeros((8, M_WIDTH), F32)

    def conv_silu(src_ref, buf, w_ref):
        buf[8:L + 8, :] = src_ref[0].astype(F32)
        acc = buf[8:L + 8, :] * w_ref[CONV_K - 1:CONV_K, :]
        for jj in range(CONV_K - 1):
            off = 8 - (CONV_K - 1) + jj
            acc = acc + buf[off:off + L, :] * w_ref[jj:jj + 1, :]
        buf[0:8, :] = buf[L:L + 8, :]
        return acc * _sigmoid(acc)

    q = conv_silu(mq_ref, qbuf, cqw_ref) * (M_HD ** -0.5)
    k = conv_silu(mk_ref, kbuf, ckw_ref)
    q16 = q.astype(BF16)
    k16 = k.astype(BF16)

    g = g_ref[0] + gb_ref[...]
    lf = jnp.minimum(g, 0.0) - jnp.log1p(jnp.exp(-jnp.abs(g)))
    row = lax.broadcasted_iota(jnp.int32, (L, GATE_PAD), 0)
    bc = lf
    sh = 1
    while sh < L:
        bc = bc + jnp.where(row >= sh, pltpu.roll(bc, sh, 0), 0.0)
        sh *= 2
    bsh = pltpu.roll(bc, GATE_PAD - M_HEADS, 1)
    a = g - bsh
    cm = a
    sh = 1
    while sh < L:
        cm = jnp.maximum(cm, jnp.where(row >= sh, pltpu.roll(cm, sh, 0), -jnp.inf))
        sh *= 2
    m_prev = m_scr[...]
    mm = jnp.maximum(cm, m_prev)
    iw = jnp.exp(m_prev - mm)
    eneg = jnp.exp(-(bsh + mm))
    a_t = a.T

    tri = (lax.broadcasted_iota(jnp.int32, (L, L), 1) <= lax.broadcasted_iota(jnp.int32, (L, L), 0))

    for h in range(M_HEADS):
        cs = slice(h * M_HD, (h + 1) * M_HD)
        qh, kh = q16[:, cs], k16[:, cs]
        vh = mv_ref[0, :, cs]
        m_col = mm[:, h:h + 1]
        iw_col = iw[:, h:h + 1]
        dw = jnp.where(tri, jnp.exp(a_t[h:h + 1, :] - m_col), 0.0)
        s = _nt_dot(qh, kh) * dw
        ct = ct_scr[h]
        num = iw_col * jnp.dot(qh, ct.astype(BF16), preferred_element_type=F32) \
            + jnp.dot(s.astype(BF16), vh, preferred_element_type=F32)
        den = iw_col * jnp.sum(q[:, cs] * n_scr[h:h + 1, :], axis=-1, keepdims=True) \
            + jnp.sum(s, axis=-1, keepdims=True)
        hv = num / jnp.maximum(jnp.abs(den), eneg[:, h:h + 1])
        mu = jnp.mean(hv, axis=-1, keepdims=True)
        hc = hv - mu
        var = jnp.mean(hc * hc, axis=-1, keepdims=True)
        hn = hc * lax.rsqrt(var + EPS) * nw_ref[:, cs]
        og = _sigmoid(mo_ref[0, :, cs].astype(F32))
        z = mz_ref[0, :, cs].astype(F32)
        o_ref[0, :, cs] = (og * hn * (z * _sigmoid(z))).astype(BF16)

        m_last = mm[L - 1:L, h:h + 1]
        w_col = jnp.exp(a[:, h:h + 1] - m_last)
        decay = iw[L - 1:L, h:h + 1]
        wv = (w_col * vh.astype(F32)).astype(BF16)
        ct_scr[h] = decay * ct + _tn_dot(kh, wv)
        n_scr[h:h + 1, :] = decay * n_scr[h:h + 1, :] + jnp.sum(w_col * k[:, cs], axis=0, keepdims=True)

    m_scr[...] = bsh[L - 1:L, :] + mm[L - 1:L, :]


def _mlstm(proj3, gates3, conv_q_w, conv_k_w, m_norm_w, gbias):
    B, S, _ = proj3.shape
    L = M_CHUNK
    W = M_WIDTH

    def col(cb):
        return pl.BlockSpec((1, L, W), lambda b, c, cb=cb: (b, c, cb))

    return pl.pallas_call(
        _mlstm_kernel,
        grid=(B, S // L),
        in_specs=[
            col(0), col(1), col(2), col(3), col(4),
            pl.BlockSpec((1, L, GATE_PAD), lambda b, c: (b, c, 0)),
            pl.BlockSpec((CONV_K, W), lambda b, c: (0, 0)),
            pl.BlockSpec((CONV_K, W), lambda b, c: (0, 0)),
            pl.BlockSpec((1, W), lambda b, c: (0, 0)),
            pl.BlockSpec((1, GATE_PAD), lambda b, c: (0, 0)),
        ],
        out_specs=pl.BlockSpec((1, L, W), lambda b, c: (b, c, 0)),
        out_shape=jax.ShapeDtypeStruct((B, S, W), BF16),
        scratch_shapes=[
            pltpu.VMEM((M_HEADS, M_HD, M_HD), F32),
            pltpu.VMEM((8, M_HD), F32),
            pltpu.VMEM((1, GATE_PAD), F32),
            pltpu.VMEM((L + 8, W), F32),
            pltpu.VMEM((L + 8, W), F32),
        ],
        compiler_params=pltpu.CompilerParams(
            dimension_semantics=("arbitrary", "arbitrary"), vmem_limit_bytes=VMEM_LIMIT),
        name="mlstm",
    )(proj3, proj3, proj3, proj3, proj3, gates3, conv_q_w, conv_k_w, m_norm_w, gbias)


def _bucket_tiles():
    T = ATT_BLK
    i = np.arange(T)[:, None]
    j = np.arange(T)[None, :]
    out = []
    for base in (0, T):
        n = base + i - j
        max_exact = N_BUCKETS // 2
        nf = np.maximum(n, 1).astype(np.float64)
        large = max_exact + (np.log(nf / max_exact) / math.log(MAX_DIST / max_exact)
                             * (N_BUCKETS - max_exact)).astype(np.int64)
        large = np.minimum(large, N_BUCKETS - 1)
        bk = np.where(n < max_exact, n, large)
        bk = np.where(n < 0, -1, bk)
        out.append(bk)
    return np.stack(out).astype(np.int32)


def _attn_kernel(relb_ref, bucket_ref, lam_ref, anw_ref, q_ref, k_ref, v_ref, z_ref, o_ref,
                 bias_scr, m_scr, l_scr, acc_scr):
    T = ATT_BLK
    h = pl.program_id(0)
    b = pl.program_id(1)
    qi = pl.program_id(2)

    @pl.when((b == 0) & (qi == 0))
    def _():
        far = relb_ref[N_BUCKETS - 1, h]
        for t in range(2):
            bk = bucket_ref[t]
            tile = jnp.zeros((T, T), F32)
            for bb in range(N_BUCKETS - 1):
                tile = jnp.where(bk == bb, relb_ref[bb, h] - far, tile)
            bias_scr[t] = jnp.where(bk < 0, NEG, tile)

    qs = q_ref[0] * (A_QKD ** -0.5)
    lane = lax.broadcasted_iota(jnp.int32, (T, A_VHD), 1)
    zero = jnp.zeros_like(qs)
    qq = jnp.concatenate([jnp.where(lane < A_QKD, qs, zero),
                          jnp.where(lane >= A_QKD, qs, zero)], axis=0)

    m_scr[...] = jnp.full_like(m_scr, NEG)
    l_scr[...] = jnp.zeros_like(l_scr)
    acc_scr[...] = jnp.zeros_like(acc_scr)

    def step(kj, bias):
        start = pl.multiple_of(kj * T, T)
        kb = k_ref[0, pl.ds(start, T), :]
        vb = v_ref[0, pl.ds(start, T), :]
        s = _nt_dot(qq, kb)
        if bias is not None:
            s = s + jnp.concatenate([bias, bias], axis=0)
        m_old = m_scr[...]
        m_new = jnp.maximum(m_old, jnp.max(s, axis=-1, keepdims=True))
        alpha = jnp.exp(m_old - m_new)
        p = jnp.exp(s - m_new)
        l_scr[...] = alpha * l_scr[...] + jnp.sum(p, axis=-1, keepdims=True)
        acc_scr[...] = alpha * acc_scr[...] + jnp.dot(p.astype(BF16), vb, preferred_element_type=F32)
        m_scr[...] = m_new

    def far_body(kj, carry):
        step(kj, None)
        return carry

    lax.fori_loop(0, jnp.maximum(qi - 1, 0), far_body, 0)

    @pl.when(qi >= 1)
    def _():
        step(qi - 1, bias_scr[1])

    step(qi, bias_scr[0])

    lam_p = lam_ref[...]
    lam = (jnp.exp(jnp.sum(lam_p[0:1] * lam_p[1:2], axis=-1, keepdims=True))
           - jnp.exp(jnp.sum(lam_p[2:3] * lam_p[3:4], axis=-1, keepdims=True)) + LAM_INIT)
    o = acc_scr[...] / l_scr[...]
    d = o[0:T] - lam * o[T:2 * T]
    y = d * lax.rsqrt(jnp.mean(d * d, axis=-1, keepdims=True) + EPS) * anw_ref[...] * (1.0 - LAM_INIT)
    z = z_ref[0].astype(F32)
    o_ref[0] = (y * (z * _sigmoid(z))).astype(BF16)


def _attn(proj3, rel_bias, lam_p, a_norm_w):
    B, S, _ = proj3.shape
    T = ATT_BLK
    q0 = GATE_COL0 // A_VHD
    k0 = q0 + A_HEADS
    v0 = k0 + A_HEADS
    z0 = v0 + A_HEADS
    buckets = jnp.asarray(_bucket_tiles())
    return pl.pallas_call(
        _attn_kernel,
        grid=(A_HEADS, B, S // T),
        in_specs=[
            pl.BlockSpec(memory_space=pltpu.SMEM),
            pl.BlockSpec((2, T, T), lambda h, b, i: (0, 0, 0)),
            pl.BlockSpec((4, A_QKD), lambda h, b, i: (0, 0)),
            pl.BlockSpec((1, A_VHD), lambda h, b, i: (0, 0)),
            pl.BlockSpec((1, T, A_VHD), lambda h, b, i: (b, i, q0 + h)),
            pl.BlockSpec((1, S, A_VHD), lambda h, b, i: (b, 0, k0 + h)),
            pl.BlockSpec((1, S, A_VHD), lambda h, b, i: (b, 0, v0 + h)),
            pl.BlockSpec((1, T, A_VHD), lambda h, b, i: (b, i, z0 + h)),
        ],
        out_specs=pl.BlockSpec((1, T, A_VHD), lambda h, b, i: (b, i, h)),
        out_shape=jax.ShapeDtypeStruct((B, S, A_WIDTH), BF16),
        scratch_shapes=[
            pltpu.VMEM((2, T, T), F32),
            pltpu.VMEM((2 * T, 1), F32),
            pltpu.VMEM((2 * T, 1), F32),
            pltpu.VMEM((2 * T, A_VHD), F32),
        ],
        compiler_params=pltpu.CompilerParams(
            dimension_semantics=("arbitrary", "arbitrary", "arbitrary"), vmem_limit_bytes=VMEM_LIMIT),
        name="attn",
    )(rel_bias, buckets, lam_p, a_norm_w, proj3, proj3, proj3, proj3)


def _outproj_kernel(hm_ref, ha_ref, x_ref, gate_ref, w_ref, fw_ref, o_ref):
    y = jnp.dot(hm_ref[...], w_ref[0:M_WIDTH, :], preferred_element_type=F32)
    y = y + jnp.dot(ha_ref[...], w_ref[M_WIDTH:, :], preferred_element_type=F32)
    r = x_ref[...] + gate_ref[0] * y
    o_ref[...] = r * lax.rsqrt(jnp.mean(r * r, axis=-1, keepdims=True) + EPS) * fw_ref[...]


def _outproj(hm2d, ha2d, x2d, mod3, w_out, final_w, B, S, tm=512):
    M, D = x2d.shape
    tiles_per_seq = S // tm
    return pl.pallas_call(
        _outproj_kernel,
        grid=(M // tm,),
        in_specs=[
            pl.BlockSpec((tm, M_WIDTH), lambda i: (i, 0)),
            pl.BlockSpec((tm, A_WIDTH), lambda i: (i, 0)),
            pl.BlockSpec((tm, D), lambda i: (i, 0)),
            pl.BlockSpec((1, 1, D), lambda i: (2 * B + i // tiles_per_seq, 0, 0)),
            pl.BlockSpec((M_WIDTH + A_WIDTH, D), lambda i: (0, 0)),
            pl.BlockSpec((1, D), lambda i: (0, 0)),
        ],
        out_specs=pl.BlockSpec((tm, D), lambda i: (i, 0)),
        out_shape=jax.ShapeDtypeStruct((M, D), F32),
        compiler_params=pltpu.CompilerParams(
            dimension_semantics=("arbitrary",), vmem_limit_bytes=VMEM_LIMIT),
        name="outproj",
    )(hm2d, ha2d, x2d, mod3, w_out, final_w)


def kernel(x, c, norm_w, w_ada, b_ada, w_in, b_i, b_f, conv_q_w, conv_k_w, m_norm_w,
           lambda_q1, lambda_k1, lambda_q2, lambda_k2, a_norm_w, rel_bias, w_out, final_norm_w):
    B, S, D = x.shape
    assert norm_w.shape[0] == 1, "single layer only"
    x2d = x.reshape(B * S, D)

    mod = _adaln(c, w_ada[0], b_ada[0])
    mod3 = mod.reshape(3 * B, 1, D)

    w_in0 = w_in[0]
    w_main = jnp.concatenate([w_in0[:, :GATE_COL0], w_in0[:, GATE_COL0 + 2 * M_HEADS:]], axis=1).astype(BF16)
    w_gate = jnp.pad(w_in0[:, GATE_COL0:GATE_COL0 + 2 * M_HEADS],
                     ((0, 0), (0, GATE_PAD - 2 * M_HEADS))).astype(BF16)
    proj, gates = _inproj(x2d, mod3, norm_w, w_main, w_gate, B, S)
    proj3 = proj.reshape(B, S, N_MAIN)
    gates3 = gates.reshape(B, S, GATE_PAD)

    gbias = jnp.pad(jnp.concatenate([b_i[0], b_f[0]]), (0, GATE_PAD - 2 * M_HEADS)).reshape(1, GATE_PAD)
    hm = _mlstm(proj3, gates3, conv_q_w[0], conv_k_w[0], m_norm_w, gbias)

    lam_p = jnp.stack([lambda_q1[0], lambda_k1[0], lambda_q2[0], lambda_k2[0]])
    ha = _attn(proj3, rel_bias, lam_p, a_norm_w)

    out = _outproj(hm.reshape(B * S, M_WIDTH), ha.reshape(B * S, A_WIDTH), x2d, mod3,
                   w_out[0].astype(BF16), final_norm_w.reshape(1, D), B, S)
    return out.reshape(B, S, D)
```

```python
import functools
import math

import numpy as np
import jax
import jax.numpy as jnp
from jax import lax
from jax.experimental import pallas as pl
from jax.experimental.pallas import tpu as pltpu

F32 = jnp.float32
BF16 = jnp.bfloat16

D_MODEL = 1024
M_WIDTH = 1024
M_HEADS = 4
M_HD = 256
CONV_K = 4
A_WIDTH = 1024
A_HEADS = 8
A_VHD = 128
A_QKD = 64
N_BUCKETS = 32
MAX_DIST = 128
EPS = 1e-6
LAM_INIT = 0.8 - 0.6 * math.exp(-0.3 * 0)

N_MAIN = 5 * M_WIDTH + 4 * A_WIDTH
GATE_COL0 = 5 * M_WIDTH
GATE_PAD = 128

M_CHUNK = 256
ATT_BLK = 256
ATT_HB = 2
NEG = -1e30

VMEM_LIMIT = 48 * 1024 * 1024


def _nt_dot(a, b):
    return lax.dot_general(a, b, (((1,), (1,)), ((), ())), preferred_element_type=F32)


def _tn_dot(a, b):
    return lax.dot_general(a, b, (((0,), (0,)), ((), ())), preferred_element_type=F32)


def _sigmoid(x):
    return 1.0 / (1.0 + jnp.exp(-x))


def _adaln_kernel(c_ref, w_ref, b_ref, o_ref):
    c = c_ref[...]
    sc = (c * _sigmoid(c)).astype(BF16)
    o_ref[0] = jnp.dot(sc, w_ref[...].astype(BF16), preferred_element_type=F32) + b_ref[0]


def _adaln(c, w_ada, b_ada):
    B, D = c.shape
    return pl.pallas_call(
        _adaln_kernel,
        grid=(3,),
        in_specs=[
            pl.BlockSpec((B, D), lambda j: (0, 0)),
            pl.BlockSpec((D, D), lambda j: (0, j)),
            pl.BlockSpec((1, 1, D), lambda j: (j, 0, 0)),
        ],
        out_specs=pl.BlockSpec((1, B, D), lambda j: (j, 0, 0)),
        out_shape=jax.ShapeDtypeStruct((3, B, D), F32),
        compiler_params=pltpu.CompilerParams(
            dimension_semantics=("arbitrary",), vmem_limit_bytes=VMEM_LIMIT),
        name="adaln",
    )(c, w_ada, b_ada.reshape(3, 1, D))


def _inproj_kernel(x_ref, shift_ref, scale_ref, nw_ref, w_ref, wg_ref, proj_ref, gates_ref, h_scr):
    j = pl.program_id(1)

    @pl.when(j == 0)
    def _():
        x = x_ref[...]
        y = x * lax.rsqrt(jnp.mean(x * x, axis=-1, keepdims=True) + EPS) * nw_ref[...]
        h = (y * (1.0 + scale_ref[0]) + shift_ref[0]).astype(BF16)
        h_scr[...] = h
        gates_ref[...] = jnp.dot(h, wg_ref[...], preferred_element_type=F32)

    proj_ref[...] = jnp.dot(h_scr[...], w_ref[...], preferred_element_type=F32).astype(BF16)


def _inproj(x2d, mod3, norm_w, w_main, w_gate, B, S, tm=1024, tn=1024):
    M, D = x2d.shape
    N = w_main.shape[1]
    tiles_per_seq = S // tm
    return pl.pallas_call(
        _inproj_kernel,
        grid=(M // tm, N // tn),
        in_specs=[
            pl.BlockSpec((tm, D), lambda i, j: (i, 0)),
            pl.BlockSpec((1, 1, D), lambda i, j: (i // tiles_per_seq, 0, 0)),
            pl.BlockSpec((1, 1, D), lambda i, j: (B + i // tiles_per_seq, 0, 0)),
            pl.BlockSpec((1, D), lambda i, j: (0, 0)),
            pl.BlockSpec((D, tn), lambda i, j: (0, j)),
            pl.BlockSpec((D, GATE_PAD), lambda i, j: (0, 0)),
        ],
        out_specs=[
            pl.BlockSpec((tm, tn), lambda i, j: (i, j)),
            pl.BlockSpec((tm, GATE_PAD), lambda i, j: (i, 0)),
        ],
        out_shape=[
            jax.ShapeDtypeStruct((M, N), BF16),
            jax.ShapeDtypeStruct((M, GATE_PAD), F32),
        ],
        scratch_shapes=[pltpu.VMEM((tm, D), BF16)],
        compiler_params=pltpu.CompilerParams(
            dimension_semantics=("arbitrary", "arbitrary"), vmem_limit_bytes=VMEM_LIMIT),
        name="inproj",
    )(x2d, mod3, mod3, norm_w, w_main, w_gate)


def _mlstm_kernel(mq_ref, mk_ref, mv_ref, mo_ref, mz_ref, g_ref, cqw_ref, ckw_ref, nw_ref, gb_ref,
                  o_ref, ct_scr, n_scr, m_scr, qbuf, kbuf):
    L = M_CHUNK
    c = pl.program_id(1)

    @pl.when(c == 0)
    def _():
        ct_scr[...] = jnp.zeros_like(ct_scr)
        n_scr[...] = jnp.zeros_like(n_scr)
        m_scr[...] = jnp.zeros_like(m_scr)
        qbuf[0:8, :] = jnp.zeros((8, M_WIDTH), F32)
        kbuf[0:8, :] = jnp.zeros((8, M_WIDTH), F32)

    def conv_silu(src_ref, buf, w_ref):
        buf[8:L + 8, :] = src_ref[0].astype(F32)
        acc = buf[8:L + 8, :] * w_ref[CONV_K - 1:CONV_K, :]
        for jj in range(CONV_K - 1):
            off = 8 - (CONV_K - 1) + jj
            acc = acc + buf[off:off + L, :] * w_ref[jj:jj + 1, :]
        buf[0:8, :] = buf[L:L + 8, :]
        return acc * _sigmoid(acc)

    q = conv_silu(mq_ref, qbuf, cqw_ref) * (M_HD ** -0.5)
    k = conv_silu(mk_ref, kbuf, ckw_ref)
    q16 = q.astype(BF16)
    k16 = k.astype(BF16)

    g = g_ref[0] + gb_ref[...]
    lf = jnp.minimum(g, 0.0) - jnp.log1p(jnp.exp(-jnp.abs(g)))
    row = lax.broadcasted_iota(jnp.int32, (L, GATE_PAD), 0)
    bc = lf
    sh = 1
    while sh < L:
        bc = bc + jnp.where(row >= sh, pltpu.roll(bc, sh, 0), 0.0)
        sh *= 2
    bsh = pltpu.roll(bc, GATE_PAD - M_HEADS, 1)
    a = g - bsh
    cm = a
    sh = 1
    while sh < L:
        cm = jnp.maximum(cm, jnp.where(row >= sh, pltpu.roll(cm, sh, 0), -jnp.inf))
        sh *= 2
    m_prev = m_scr[...]
    mm = jnp.maximum(cm, m_prev)
    iw = jnp.exp(m_prev - mm)
    eneg = jnp.exp(-(bsh + mm))
    a_t = a.T

    tri = (lax.broadcasted_iota(jnp.int32, (L, L), 1) <= lax.broadcasted_iota(jnp.int32, (L, L), 0))

    for h in range(M_HEADS):
        cs = slice(h * M_HD, (h + 1) * M_HD)
        qh, kh = q16[:, cs], k16[:, cs]
        vh = mv_ref[0, :, cs]
        m_col = mm[:, h:h + 1]
        iw_col = iw[:, h:h + 1]
        dw = jnp.where(tri, jnp.exp(a_t[h:h + 1, :] - m_col), 0.0)
        s = _nt_dot(qh, kh) * dw
        ct = ct_scr[h]
        num = iw_col * jnp.dot(qh, ct.astype(BF16), preferred_element_type=F32) \
            + jnp.dot(s.astype(BF16), vh, preferred_element_type=F32)
        den = iw_col * jnp.sum(q[:, cs] * n_scr[h:h + 1, :], axis=-1, keepdims=True) \
            + jnp.sum(s, axis=-1, keepdims=True)
        hv = num / jnp.maximum(jnp.abs(den), eneg[:, h:h + 1])
        mu = jnp.mean(hv, axis=-1, keepdims=True)
        hc = hv - mu
        var = jnp.mean(hc * hc, axis=-1, keepdims=True)
        hn = hc * lax.rsqrt(var + EPS) * nw_ref[:, cs]
        og = _sigmoid(mo_ref[0, :, cs].astype(F32))
        z = mz_ref[0, :, cs].astype(F32)
        o_ref[0, :, cs] = (og * hn * (z * _sigmoid(z))).astype(BF16)

        m_last = mm[L - 1:L, h:h + 1]
        w_col = jnp.exp(a[:, h:h + 1] - m_last)
        decay = iw[L - 1:L, h:h + 1]
        wv = (w_col * vh.astype(F32)).astype(BF16)
        ct_scr[h] = decay * ct + _tn_dot(kh, wv)
        n_scr[h:h + 1, :] = decay * n_scr[h:h + 1, :] + jnp.sum(w_col * k[:, cs], axis=0, keepdims=True)

    m_scr[...] = bsh[L - 1:L, :] + mm[L - 1:L, :]


def _mlstm(proj3, gates3, conv_q_w, conv_k_w, m_norm_w, gbias):
    B, S, _ = proj3.shape
    L = M_CHUNK
    W = M_WIDTH

    def col(cb):
        return pl.BlockSpec((1, L, W), lambda b, c, cb=cb: (b, c, cb))

    return pl.pallas_call(
        _mlstm_kernel,
        grid=(B, S // L),
        in_specs=[
            col(0), col(1), col(2), col(3), col(4),
            pl.BlockSpec((1, L, GATE_PAD), lambda b, c: (b, c, 0)),
            pl.BlockSpec((CONV_K, W), lambda b, c: (0, 0)),
            pl.BlockSpec((CONV_K, W), lambda b, c: (0, 0)),
            pl.BlockSpec((1, W), lambda b, c: (0, 0)),
            pl.BlockSpec((1, GATE_PAD), lambda b, c: (0, 0)),
        ],
        out_specs=pl.BlockSpec((1, L, W), lambda b, c: (b, c, 0)),
        out_shape=jax.ShapeDtypeStruct((B, S, W), BF16),
        scratch_shapes=[
            pltpu.VMEM((M_HEADS, M_HD, M_HD), F32),
            pltpu.VMEM((8, M_HD), F32),
            pltpu.VMEM((1, GATE_PAD), F32),
            pltpu.VMEM((L + 8, W), F32),
            pltpu.VMEM((L + 8, W), F32),
        ],
        compiler_params=pltpu.CompilerParams(
            dimension_semantics=("arbitrary", "arbitrary"), vmem_limit_bytes=VMEM_LIMIT),
        name="mlstm",
    )(proj3, proj3, proj3, proj3, proj3, gates3, conv_q_w, conv_k_w, m_norm_w, gbias)


def _bucket_tiles():
    T = ATT_BLK
    kk = np.arange(T)[:, None]
    qq = np.arange(T)[None, :]
    out = []
    for base in (0, T):
        n = base + qq - kk
        max_exact = N_BUCKETS // 2
        nf = np.maximum(n, 1).astype(np.float64)
        large = max_exact + (np.log(nf / max_exact) / math.log(MAX_DIST / max_exact)
                             * (N_BUCKETS - max_exact)).astype(np.int64)
        large = np.minimum(large, N_BUCKETS - 1)
        bk = np.where(n < max_exact, n, large)
        bk = np.where(n < 0, -1, bk)
        out.append(bk)
    return np.stack(out).astype(np.int32)


def _attn_kernel(relb_ref, bucket_ref, lam_ref, anw_ref, q_ref, k_ref, v_ref, z_ref, o_ref,
                 bias_scr, vt_scr, acc_scr):
    T = ATT_BLK
    HB = ATT_HB
    S = k_ref.shape[1]
    hg = pl.program_id(0)
    b = pl.program_id(1)
    qi = pl.program_id(2)
    heads = [slice(hb * A_VHD, (hb + 1) * A_VHD) for hb in range(HB)]

    @pl.when((b == 0) & (qi == 0))
    def _():
        for hb in range(HB):
            h = hg * HB + hb
            far = relb_ref[N_BUCKETS - 1, h]
            for t in range(2):
                bk = bucket_ref[t]
                tile = jnp.zeros((T, T), F32)
                for bb in range(N_BUCKETS - 1):
                    tile = jnp.where(bk == bb, relb_ref[bb, h] - far, tile)
                tile = jnp.where(bk < 0, NEG, tile)
                bias_scr[hb, t] = jnp.concatenate([tile, tile], axis=1)

    @pl.when(qi == 0)
    def _():
        for hb in range(HB):
            for ci in range(S // T):
                rows = slice(ci * T, (ci + 1) * T)
                vt_scr[hb, :, rows] = v_ref[0, rows, heads[hb]].astype(F32).T.astype(BF16)

    lane = lax.broadcasted_iota(jnp.int32, (T, A_VHD), 1)
    qqs = []
    for hb in range(HB):
        qs = q_ref[0, :, heads[hb]] * (A_QKD ** -0.5)
        zero = jnp.zeros_like(qs)
        qqs.append(jnp.concatenate([jnp.where(lane < A_QKD, qs, zero),
                                    jnp.where(lane >= A_QKD, qs, zero)], axis=0))

    acc_scr[...] = jnp.zeros_like(acc_scr)

    def step(kj, bias_idx, carry):
        start = pl.multiple_of(kj * T, T)
        out = []
        ss = [_nt_dot(k_ref[0, pl.ds(start, T), heads[hb]], qqs[hb]) for hb in range(HB)]
        for hb in range(HB):
            m_old, l_old = carry[hb]
            s = ss[hb]
            if bias_idx is not None:
                s = s + bias_scr[hb, bias_idx]
            m_new = jnp.maximum(m_old, jnp.max(s, axis=0, keepdims=True))
            alpha = jnp.exp(m_old - m_new)
            p = jnp.exp(s - m_new)
            l_new = alpha * l_old + jnp.sum(p, axis=0, keepdims=True)
            pv = jnp.dot(vt_scr[hb, :, pl.ds(start, T)], p.astype(BF16),
                         preferred_element_type=F32)
            acc_scr[hb] = alpha * acc_scr[hb] + pv
            out.append((m_new, l_new))
        return tuple(out)

    init = tuple((jnp.full((1, 2 * T), NEG, F32), jnp.zeros((1, 2 * T), F32)) for _ in range(HB))
    carry = lax.fori_loop(0, jnp.maximum(qi - 1, 0), lambda kj, c: step(kj, None, c), init)
    carry = lax.cond(qi >= 1, lambda c: step(qi - 1, 1, c), lambda c: c, carry)
    carry = step(qi, 0, carry)

    lam_p = lam_ref[...]
    lam = (jnp.exp(jnp.sum(lam_p[0:1] * lam_p[1:2], axis=-1, keepdims=True))
           - jnp.exp(jnp.sum(lam_p[2:3] * lam_p[3:4], axis=-1, keepdims=True)) + LAM_INIT)
    for hb in range(HB):
        o = acc_scr[hb] / carry[hb][1]
        d = (o[:, 0:T] - lam * o[:, T:2 * T]).T
        y = d * lax.rsqrt(jnp.mean(d * d, axis=-1, keepdims=True) + EPS) * anw_ref[...] * (1.0 - LAM_INIT)
        z = z_ref[0, :, heads[hb]].astype(F32)
        o_ref[0, :, heads[hb]] = (y * (z * _sigmoid(z))).astype(BF16)


def _attn(proj3, rel_bias, lam_p, a_norm_w):
    B, S, _ = proj3.shape
    T = ATT_BLK
    HB = ATT_HB
    W = HB * A_VHD
    q0 = GATE_COL0 // W
    k0 = q0 + A_HEADS // HB
    v0 = k0 + A_HEADS // HB
    z0 = v0 + A_HEADS // HB
    buckets = jnp.asarray(_bucket_tiles())
    return pl.pallas_call(
        _attn_kernel,
        grid=(A_HEADS // HB, B, S // T),
        in_specs=[
            pl.BlockSpec(memory_space=pltpu.SMEM),
            pl.BlockSpec((2, T, T), lambda h, b, i: (0, 0, 0)),
            pl.BlockSpec((4, A_QKD), lambda h, b, i: (0, 0)),
            pl.BlockSpec((1, A_VHD), lambda h, b, i: (0, 0)),
            pl.BlockSpec((1, T, W), lambda h, b, i: (b, i, q0 + h)),
            pl.BlockSpec((1, S, W), lambda h, b, i: (b, 0, k0 + h)),
            pl.BlockSpec((1, S, W), lambda h, b, i: (b, 0, v0 + h)),
            pl.BlockSpec((1, T, W), lambda h, b, i: (b, i, z0 + h)),
        ],
        out_specs=pl.BlockSpec((1, T, W), lambda h, b, i: (b, i, h)),
        out_shape=jax.ShapeDtypeStruct((B, S, A_WIDTH), BF16),
        scratch_shapes=[
            pltpu.VMEM((HB, 2, T, 2 * T), F32),
            pltpu.VMEM((HB, A_VHD, S), BF16),
            pltpu.VMEM((HB, A_VHD, 2 * T), F32),
        ],
        compiler_params=pltpu.CompilerParams(
            dimension_semantics=("arbitrary", "arbitrary", "arbitrary"), vmem_limit_bytes=VMEM_LIMIT),
        name="attn",
    )(rel_bias, buckets, lam_p, a_norm_w, proj3, proj3, proj3, proj3)


def _outproj_kernel(hm_ref, ha_ref, x_ref, gate_ref, w_ref, fw_ref, o_ref):
    y = jnp.dot(hm_ref[...], w_ref[0:M_WIDTH, :], preferred_element_type=F32)
    y = y + jnp.dot(ha_ref[...], w_ref[M_WIDTH:, :], preferred_element_type=F32)
    r = x_ref[...] + gate_ref[0] * y
    o_ref[...] = r * lax.rsqrt(jnp.mean(r * r, axis=-1, keepdims=True) + EPS) * fw_ref[...]


def _outproj(hm2d, ha2d, x2d, mod3, w_out, final_w, B, S, tm=512):
    M, D = x2d.shape
    tiles_per_seq = S // tm
    return pl.pallas_call(
        _outproj_kernel,
        grid=(M // tm,),
        in_specs=[
            pl.BlockSpec((tm, M_WIDTH), lambda i: (i, 0)),
            pl.BlockSpec((tm, A_WIDTH), lambda i: (i, 0)),
            pl.BlockSpec((tm, D), lambda i: (i, 0)),
            pl.BlockSpec((1, 1, D), lambda i: (2 * B + i // tiles_per_seq, 0, 0)),
            pl.BlockSpec((M_WIDTH + A_WIDTH, D), lambda i: (0, 0)),
            pl.BlockSpec((1, D), lambda i: (0, 0)),
        ],
        out_specs=pl.BlockSpec((tm, D), lambda i: (i, 0)),
        out_shape=jax.ShapeDtypeStruct((M, D), F32),
        compiler_params=pltpu.CompilerParams(
            dimension_semantics=("arbitrary",), vmem_limit_bytes=VMEM_LIMIT),
        name="outproj",
    )(hm2d, ha2d, x2d, mod3, w_out, final_w)


def kernel(x, c, norm_w, w_ada, b_ada, w_in, b_i, b_f, conv_q_w, conv_k_w, m_norm_w,
           lambda_q1, lambda_k1, lambda_q2, lambda_k2, a_norm_w, rel_bias, w_out, final_norm_w):
    B, S, D = x.shape
    assert norm_w.shape[0] == 1, "single layer only"
    x2d = x.reshape(B * S, D)

    mod = _adaln(c, w_ada[0], b_ada[0])
    mod3 = mod.reshape(3 * B, 1, D)

    w_in0 = w_in[0]
    w_main = jnp.concatenate([w_in0[:, :GATE_COL0], w_in0[:, GATE_COL0 + 2 * M_HEADS:]], axis=1).astype(BF16)
    w_gate = jnp.pad(w_in0[:, GATE_COL0:GATE_COL0 + 2 * M_HEADS],
                     ((0, 0), (0, GATE_PAD - 2 * M_HEADS))).astype(BF16)
    proj, gates = _inproj(x2d, mod3, norm_w, w_main, w_gate, B, S)
    proj3 = proj.reshape(B, S, N_MAIN)
    gates3 = gates.reshape(B, S, GATE_PAD)

    gbias = jnp.pad(jnp.concatenate([b_i[0], b_f[0]]), (0, GATE_PAD - 2 * M_HEADS)).reshape(1, GATE_PAD)
    hm = _mlstm(proj3, gates3, conv_q_w[0], conv_k_w[0], m_norm_w, gbias)

    lam_p = jnp.stack([lambda_q1[0], lambda_k1[0], lambda_q2[0], lambda_k2[0]])
    ha = _attn(proj3, rel_bias, lam_p, a_norm_w)

    out = _outproj(hm.reshape(B * S, M_WIDTH), ha.reshape(B * S, A_WIDTH), x2d, mod3,
                   w_out[0].astype(BF16), final_norm_w.reshape(1, D), B, S)
    return out.reshape(B, S, D)
```

```python
import functools
import math

import numpy as np
import jax
import jax.numpy as jnp
from jax import lax
from jax.experimental import pallas as pl
from jax.experimental.pallas import tpu as pltpu

F32 = jnp.float32
BF16 = jnp.bfloat16

D_MODEL = 1024
M_WIDTH = 1024
M_HEADS = 4
M_HD = 256
CONV_K = 4
A_WIDTH = 1024
A_HEADS = 8
A_VHD = 128
A_QKD = 64
N_BUCKETS = 32
MAX_DIST = 128
EPS = 1e-6
LAM_INIT = 0.8 - 0.6 * math.exp(-0.3 * 0)

N_MAIN = 5 * M_WIDTH + 4 * A_WIDTH
GATE_COL0 = 5 * M_WIDTH
GATE_PAD = 128

M_CHUNK = 256
ATT_BLK = 256
ATT_HB = 2
VT_ROWS = A_VHD + 16
LOG2E = math.log2(math.e)
NEG = -1e30

VMEM_LIMIT = 48 * 1024 * 1024


def _nt_dot(a, b):
    return lax.dot_general(a, b, (((1,), (1,)), ((), ())), preferred_element_type=F32)


def _tn_dot(a, b):
    return lax.dot_general(a, b, (((0,), (0,)), ((), ())), preferred_element_type=F32)


def _sigmoid(x):
    return 1.0 / (1.0 + jnp.exp(-x))


def _adaln_kernel(c_ref, w_ref, b_ref, o_ref):
    c = c_ref[...]
    sc = (c * _sigmoid(c)).astype(BF16)
    o_ref[0] = jnp.dot(sc, w_ref[...].astype(BF16), preferred_element_type=F32) + b_ref[0]


def _adaln(c, w_ada, b_ada):
    B, D = c.shape
    return pl.pallas_call(
        _adaln_kernel,
        grid=(3,),
        in_specs=[
            pl.BlockSpec((B, D), lambda j: (0, 0)),
            pl.BlockSpec((D, D), lambda j: (0, j)),
            pl.BlockSpec((1, 1, D), lambda j: (j, 0, 0)),
        ],
        out_specs=pl.BlockSpec((1, B, D), lambda j: (j, 0, 0)),
        out_shape=jax.ShapeDtypeStruct((3, B, D), F32),
        compiler_params=pltpu.CompilerParams(
            dimension_semantics=("arbitrary",), vmem_limit_bytes=VMEM_LIMIT),
        name="adaln",
    )(c, w_ada, b_ada.reshape(3, 1, D))


def _inproj_kernel(x_ref, shift_ref, scale_ref, nw_ref, w_ref, wg_ref, proj_ref, gates_ref, h_scr):
    j = pl.program_id(1)

    @pl.when(j == 0)
    def _():
        x = x_ref[...]
        y = x * lax.rsqrt(jnp.mean(x * x, axis=-1, keepdims=True) + EPS) * nw_ref[...]
        h = (y * (1.0 + scale_ref[0]) + shift_ref[0]).astype(BF16)
        h_scr[...] = h
        gates_ref[...] = jnp.dot(h, wg_ref[...], preferred_element_type=F32)

    proj_ref[...] = jnp.dot(h_scr[...], w_ref[...], preferred_element_type=F32).astype(BF16)


def _inproj(x2d, mod3, norm_w, w_main, w_gate, B, S, tm=1024, tn=1024):
    M, D = x2d.shape
    N = w_main.shape[1]
    tiles_per_seq = S // tm
    return pl.pallas_call(
        _inproj_kernel,
        grid=(M // tm, N // tn),
        in_specs=[
            pl.BlockSpec((tm, D), lambda i, j: (i, 0)),
            pl.BlockSpec((1, 1, D), lambda i, j: (i // tiles_per_seq, 0, 0)),
            pl.BlockSpec((1, 1, D), lambda i, j: (B + i // tiles_per_seq, 0, 0)),
            pl.BlockSpec((1, D), lambda i, j: (0, 0)),
            pl.BlockSpec((D, tn), lambda i, j: (0, j)),
            pl.BlockSpec((D, GATE_PAD), lambda i, j: (0, 0)),
        ],
        out_specs=[
            pl.BlockSpec((tm, tn), lambda i, j: (i, j)),
            pl.BlockSpec((tm, GATE_PAD), lambda i, j: (i, 0)),
        ],
        out_shape=[
            jax.ShapeDtypeStruct((M, N), BF16),
            jax.ShapeDtypeStruct((M, GATE_PAD), F32),
        ],
        scratch_shapes=[pltpu.VMEM((tm, D), BF16)],
        compiler_params=pltpu.CompilerParams(
            dimension_semantics=("arbitrary", "arbitrary"), vmem_limit_bytes=VMEM_LIMIT),
        name="inproj",
    )(x2d, mod3, mod3, norm_w, w_main, w_gate)


def _mlstm_kernel(mq_ref, mk_ref, mv_ref, mo_ref, mz_ref, g_ref, cqw_ref, ckw_ref, nw_ref, gb_ref,
                  o_ref, ct_scr, n_scr, m_scr, qbuf, kbuf):
    L = M_CHUNK
    c = pl.program_id(1)

    @pl.when(c == 0)
    def _():
        ct_scr[...] = jnp.zeros_like(ct_scr)
        n_scr[...] = jnp.zeros_like(n_scr)
        m_scr[...] = jnp.zeros_like(m_scr)
        qbuf[0:8, :] = jnp.zeros((8, M_WIDTH), F32)
        kbuf[0:8, :] = jnp.zeros((8, M_WIDTH), F32)

    def conv_silu(src_ref, buf, w_ref):
        buf[8:L + 8, :] = src_ref[0].astype(F32)
        acc = buf[8:L + 8, :] * w_ref[CONV_K - 1:CONV_K, :]
        for jj in range(CONV_K - 1):
            off = 8 - (CONV_K - 1) + jj
            acc = acc + buf[off:off + L, :] * w_ref[jj:jj + 1, :]
        buf[0:8, :] = buf[L:L + 8, :]
        return acc * _sigmoid(acc)

    q = conv_silu(mq_ref, qbuf, cqw_ref) * (M_HD ** -0.5)
    k = conv_silu(mk_ref, kbuf, ckw_ref)
    q16 = q.astype(BF16)
    k16 = k.astype(BF16)

    g = g_ref[0] + gb_ref[...]
    lf = jnp.minimum(g, 0.0) - jnp.log1p(jnp.exp(-jnp.abs(g)))
    row = lax.broadcasted_iota(jnp.int32, (L, GATE_PAD), 0)
    bc = lf
    sh = 1
    while sh < L:
        bc = bc + jnp.where(row >= sh, pltpu.roll(bc, sh, 0), 0.0)
        sh *= 2
    bsh = pltpu.roll(bc, GATE_PAD - M_HEADS, 1)
    a = g - bsh
    cm = a
    sh = 1
    while sh < L:
        cm = jnp.maximum(cm, jnp.where(row >= sh, pltpu.roll(cm, sh, 0), -jnp.inf))
        sh *= 2
    m_prev = m_scr[...]
    mm = jnp.maximum(cm, m_prev)
    iw = jnp.exp(m_prev - mm)
    eneg = jnp.exp(-(bsh + mm))
    a_t = a.T

    tri = (lax.broadcasted_iota(jnp.int32, (L, L), 1) <= lax.broadcasted_iota(jnp.int32, (L, L), 0))

    for h in range(M_HEADS):
        cs = slice(h * M_HD, (h + 1) * M_HD)
        qh, kh = q16[:, cs], k16[:, cs]
        vh = mv_ref[0, :, cs]
        m_col = mm[:, h:h + 1]
        iw_col = iw[:, h:h + 1]
        dw = jnp.where(tri, jnp.exp(a_t[h:h + 1, :] - m_col), 0.0)
        s = _nt_dot(qh, kh) * dw
        ct = ct_scr[h]
        num = iw_col * jnp.dot(qh, ct.astype(BF16), preferred_element_type=F32) \
            + jnp.dot(s.astype(BF16), vh, preferred_element_type=F32)
        den = iw_col * jnp.sum(q[:, cs] * n_scr[h:h + 1, :], axis=-1, keepdims=True) \
            + jnp.sum(s, axis=-1, keepdims=True)
        hv = num / jnp.maximum(jnp.abs(den), eneg[:, h:h + 1])
        mu = jnp.mean(hv, axis=-1, keepdims=True)
        hc = hv - mu
        var = jnp.mean(hc * hc, axis=-1, keepdims=True)
        hn = hc * lax.rsqrt(var + EPS) * nw_ref[:, cs]
        og = _sigmoid(mo_ref[0, :, cs].astype(F32))
        z = mz_ref[0, :, cs].astype(F32)
        o_ref[0, :, cs] = (og * hn * (z * _sigmoid(z))).astype(BF16)

        m_last = mm[L - 1:L, h:h + 1]
        w_col = jnp.exp(a[:, h:h + 1] - m_last)
        decay = iw[L - 1:L, h:h + 1]
        wv = (w_col * vh.astype(F32)).astype(BF16)
        ct_scr[h] = decay * ct + _tn_dot(kh, wv)
        n_scr[h:h + 1, :] = decay * n_scr[h:h + 1, :] + jnp.sum(w_col * k[:, cs], axis=0, keepdims=True)

    m_scr[...] = bsh[L - 1:L, :] + mm[L - 1:L, :]


def _mlstm(proj3, gates3, conv_q_w, conv_k_w, m_norm_w, gbias):
    B, S, _ = proj3.shape
    L = M_CHUNK
    W = M_WIDTH

    def col(cb):
        return pl.BlockSpec((1, L, W), lambda b, c, cb=cb: (b, c, cb))

    return pl.pallas_call(
        _mlstm_kernel,
        grid=(B, S // L),
        in_specs=[
            col(0), col(1), col(2), col(3), col(4),
            pl.BlockSpec((1, L, GATE_PAD), lambda b, c: (b, c, 0)),
            pl.BlockSpec((CONV_K, W), lambda b, c: (0, 0)),
            pl.BlockSpec((CONV_K, W), lambda b, c: (0, 0)),
            pl.BlockSpec((1, W), lambda b, c: (0, 0)),
            pl.BlockSpec((1, GATE_PAD), lambda b, c: (0, 0)),
        ],
        out_specs=pl.BlockSpec((1, L, W), lambda b, c: (b, c, 0)),
        out_shape=jax.ShapeDtypeStruct((B, S, W), BF16),
        scratch_shapes=[
            pltpu.VMEM((M_HEADS, M_HD, M_HD), F32),
            pltpu.VMEM((8, M_HD), F32),
            pltpu.VMEM((1, GATE_PAD), F32),
            pltpu.VMEM((L + 8, W), F32),
            pltpu.VMEM((L + 8, W), F32),
        ],
        compiler_params=pltpu.CompilerParams(
            dimension_semantics=("arbitrary", "arbitrary"), vmem_limit_bytes=VMEM_LIMIT),
        name="mlstm",
    )(proj3, proj3, proj3, proj3, proj3, gates3, conv_q_w, conv_k_w, m_norm_w, gbias)


def _bucket_tiles():
    T = ATT_BLK
    kk = np.arange(T)[:, None]
    qq = np.arange(T)[None, :]
    out = []
    for base in (0, T):
        n = base + qq - kk
        max_exact = N_BUCKETS // 2
        nf = np.maximum(n, 1).astype(np.float64)
        large = max_exact + (np.log(nf / max_exact) / math.log(MAX_DIST / max_exact)
                             * (N_BUCKETS - max_exact)).astype(np.int64)
        large = np.minimum(large, N_BUCKETS - 1)
        bk = np.where(n < max_exact, n, large)
        bk = np.where(n < 0, -1, bk)
        out.append(bk)
    return np.stack(out).astype(np.int32)


def _attn_kernel(relb_ref, bucket_ref, lam_ref, anw_ref, q_ref, k_ref, v_ref, z_ref, o_ref,
                 bias_scr, vt_scr, qq_scr, s0_scr, s1_scr, acc_scr):
    T = ATT_BLK
    HB = ATT_HB
    S = k_ref.shape[1]
    NQ = S // T
    hg = pl.program_id(0)
    b = pl.program_id(1)
    heads = [slice(hb * A_VHD, (hb + 1) * A_VHD) for hb in range(HB)]

    @pl.when(b == 0)
    def _():
        for hb in range(HB):
            h = hg * HB + hb
            far = relb_ref[N_BUCKETS - 1, h]
            for t in range(2):
                bk = bucket_ref[t]
                tile = jnp.zeros((T, T), F32)
                for bb in range(N_BUCKETS - 1):
                    tile = jnp.where(bk == bb, (relb_ref[bb, h] - far) * LOG2E, tile)
                tile = jnp.where(bk < 0, NEG, tile)
                bias_scr[hb, t] = jnp.concatenate([tile, tile], axis=1)

    lane = lax.broadcasted_iota(jnp.int32, (T, A_VHD), 1)
    ones_rows = (lax.broadcasted_iota(jnp.int32, (VT_ROWS - A_VHD, T), 0) == 0).astype(BF16)
    for hb in range(HB):
        for ci in range(NQ):
            rows = slice(ci * T, (ci + 1) * T)
            vt_scr[hb, 0:A_VHD, rows] = v_ref[0, rows, heads[hb]].astype(F32).T.astype(BF16)
            vt_scr[hb, A_VHD:VT_ROWS, rows] = ones_rows
            qs = (q_ref[0, rows, heads[hb]].astype(F32) * (A_QKD ** -0.5 * LOG2E)).astype(BF16)
            zero = jnp.zeros_like(qs)
            qq_scr[hb, ci, 0:T, :] = jnp.where(lane < A_QKD, qs, zero)
            qq_scr[hb, ci, T:2 * T, :] = jnp.where(lane >= A_QKD, qs, zero)
    acc_scr[...] = jnp.zeros_like(acc_scr)

    s_bufs = (s0_scr, s1_scr)

    def issue(buf, qb, kj):
        start = pl.multiple_of(kj * T, T)
        for hb in range(HB):
            s_bufs[buf][hb] = _nt_dot(k_ref[0, pl.ds(start, T), heads[hb]], qq_scr[hb, qb])

    def consume(buf, kj, bias_idx, ms):
        start = pl.multiple_of(kj * T, T)
        out = []
        for hb in range(HB):
            s = s_bufs[buf][hb]
            if bias_idx is not None:
                s = s + bias_scr[hb, bias_idx]
            m_old = ms[hb]
            m_new = jnp.maximum(m_old, jnp.max(s, axis=0, keepdims=True))
            alpha = jnp.exp2(m_old - m_new)
            p = jnp.exp2(s - m_new).astype(BF16)
            pv = jnp.dot(vt_scr[hb, :, pl.ds(start, T)], p, preferred_element_type=F32)
            acc_scr[hb] = alpha * acc_scr[hb] + pv
            out.append(m_new)
        return tuple(out)

    lam_p = lam_ref[...]
    lam = (jnp.exp(jnp.sum(lam_p[0:1] * lam_p[1:2], axis=-1, keepdims=True))
           - jnp.exp(jnp.sum(lam_p[2:3] * lam_p[3:4], axis=-1, keepdims=True)) + LAM_INIT)

    def finalize(qi):
        rows = pl.ds(pl.multiple_of(qi * T, T), T)
        for hb in range(HB):
            acc = acc_scr[hb]
            o = acc[0:A_VHD] / acc[A_VHD:A_VHD + 1]
            d = (o[:, 0:T] - lam * o[:, T:2 * T]).T
            y = d * lax.rsqrt(jnp.mean(d * d, axis=-1, keepdims=True) + EPS) * anw_ref[...] * (1.0 - LAM_INIT)
            z = z_ref[0, rows, heads[hb]].astype(F32)
            o_ref[0, rows, heads[hb]] = (y * (z * _sigmoid(z))).astype(BF16)

    m_init = tuple(jnp.full((1, 2 * T), NEG, F32) for _ in range(HB))

    def stage(par, qb, kj_next, kj, bias_idx, ms):
        def run(cur, ms):
            issue(1 - cur, qb, kj_next)
            return consume(cur, kj, bias_idx, ms)

        return lax.cond(par == 0, functools.partial(run, 0), functools.partial(run, 1), ms)

    issue(0, 0, 0)
    stage(0, 1, 0, 0, 0, m_init)
    finalize(0)

    def qblock(qi, carry):
        t0 = lax.shift_right_logical(qi * (qi + 1), 1)
        ms = lax.fori_loop(0, qi - 1, lambda j, ms: stage((t0 + j) & 1, qi, j + 1, j, None, ms), m_init)
        ms = stage((t0 + qi - 1) & 1, qi, qi, qi - 1, 1, ms)
        stage((t0 + qi) & 1, jnp.minimum(qi + 1, NQ - 1), 0, qi, 0, ms)
        finalize(qi)
        return carry

    lax.fori_loop(1, NQ, qblock, 0)


def _attn(proj3, rel_bias, lam_p, a_norm_w):
    B, S, _ = proj3.shape
    T = ATT_BLK
    HB = ATT_HB
    W = HB * A_VHD
    q0 = GATE_COL0 // W
    k0 = q0 + A_HEADS // HB
    v0 = k0 + A_HEADS // HB
    z0 = v0 + A_HEADS // HB
    buckets = jnp.asarray(_bucket_tiles())
    return pl.pallas_call(
        _attn_kernel,
        grid=(A_HEADS // HB, B),
        in_specs=[
            pl.BlockSpec(memory_space=pltpu.SMEM),
            pl.BlockSpec((2, T, T), lambda h, b: (0, 0, 0)),
            pl.BlockSpec((4, A_QKD), lambda h, b: (0, 0)),
            pl.BlockSpec((1, A_VHD), lambda h, b: (0, 0)),
            pl.BlockSpec((1, S, W), lambda h, b: (b, 0, q0 + h)),
            pl.BlockSpec((1, S, W), lambda h, b: (b, 0, k0 + h)),
            pl.BlockSpec((1, S, W), lambda h, b: (b, 0, v0 + h)),
            pl.BlockSpec((1, S, W), lambda h, b: (b, 0, z0 + h)),
        ],
        out_specs=pl.BlockSpec((1, S, W), lambda h, b: (b, 0, h)),
        out_shape=jax.ShapeDtypeStruct((B, S, A_WIDTH), BF16),
        scratch_shapes=[
            pltpu.VMEM((HB, 2, T, 2 * T), F32),
            pltpu.VMEM((HB, VT_ROWS, S), BF16),
            pltpu.VMEM((HB, S // T, 2 * T, A_VHD), BF16),
            pltpu.VMEM((HB, T, 2 * T), F32),
            pltpu.VMEM((HB, T, 2 * T), F32),
            pltpu.VMEM((HB, VT_ROWS, 2 * T), F32),
        ],
        compiler_params=pltpu.CompilerParams(
            dimension_semantics=("arbitrary", "arbitrary"), vmem_limit_bytes=VMEM_LIMIT),
        name="attn",
    )(rel_bias, buckets, lam_p, a_norm_w, proj3, proj3, proj3, proj3)


def _outproj_kernel(hm_ref, ha_ref, x_ref, gate_ref, w_ref, fw_ref, o_ref):
    y = jnp.dot(hm_ref[...], w_ref[0:M_WIDTH, :], preferred_element_type=F32)
    y = y + jnp.dot(ha_ref[...], w_ref[M_WIDTH:, :], preferred_element_type=F32)
    r = x_ref[...] + gate_ref[0] * y
    o_ref[...] = r * lax.rsqrt(jnp.mean(r * r, axis=-1, keepdims=True) + EPS) * fw_ref[...]


def _outproj(hm2d, ha2d, x2d, mod3, w_out, final_w, B, S, tm=512):
    M, D = x2d.shape
    tiles_per_seq = S // tm
    return pl.pallas_call(
        _outproj_kernel,
        grid=(M // tm,),
        in_specs=[
            pl.BlockSpec((tm, M_WIDTH), lambda i: (i, 0)),
            pl.BlockSpec((tm, A_WIDTH), lambda i: (i, 0)),
            pl.BlockSpec((tm, D), lambda i: (i, 0)),
            pl.BlockSpec((1, 1, D), lambda i: (2 * B + i // tiles_per_seq, 0, 0)),
            pl.BlockSpec((M_WIDTH + A_WIDTH, D), lambda i: (0, 0)),
            pl.BlockSpec((1, D), lambda i: (0, 0)),
        ],
        out_specs=pl.BlockSpec((tm, D), lambda i: (i, 0)),
        out_shape=jax.ShapeDtypeStruct((M, D), F32),
        compiler_params=pltpu.CompilerParams(
            dimension_semantics=("arbitrary",), vmem_limit_bytes=VMEM_LIMIT),
        name="outproj",
    )(hm2d, ha2d, x2d, mod3, w_out, final_w)


def kernel(x, c, norm_w, w_ada, b_ada, w_in, b_i, b_f, conv_q_w, conv_k_w, m_norm_w,
           lambda_q1, lambda_k1, lambda_q2, lambda_k2, a_norm_w, rel_bias, w_out, final_norm_w):
    B, S, D = x.shape
    assert norm_w.shape[0] == 1, "single layer only"
    x2d = x.reshape(B * S, D)

    mod = _adaln(c, w_ada[0], b_ada[0])
    mod3 = mod.reshape(3 * B, 1, D)

    w_in0 = w_in[0]
    w_main = jnp.concatenate([w_in0[:, :GATE_COL0], w_in0[:, GATE_COL0 + 2 * M_HEADS:]], axis=1).astype(BF16)
    w_gate = jnp.pad(w_in0[:, GATE_COL0:GATE_COL0 + 2 * M_HEADS],
                     ((0, 0), (0, GATE_PAD - 2 * M_HEADS))).astype(BF16)
    proj, gates = _inproj(x2d, mod3, norm_w, w_main, w_gate, B, S)
    proj3 = proj.reshape(B, S, N_MAIN)
    gates3 = gates.reshape(B, S, GATE_PAD)

    gbias = jnp.pad(jnp.concatenate([b_i[0], b_f[0]]), (0, GATE_PAD - 2 * M_HEADS)).reshape(1, GATE_PAD)
    hm = _mlstm(proj3, gates3, conv_q_w[0], conv_k_w[0], m_norm_w, gbias)

    lam_p = jnp.stack([lambda_q1[0], lambda_k1[0], lambda_q2[0], lambda_k2[0]])
    ha = _attn(proj3, rel_bias, lam_p, a_norm_w)

    out = _outproj(hm.reshape(B * S, M_WIDTH), ha.reshape(B * S, A_WIDTH), x2d, mod3,
                   w_out[0].astype(BF16), final_norm_w.reshape(1, D), B, S)
    return out.reshape(B, S, D)
```

```python
import functools
import math

import numpy as np
import jax
import jax.numpy as jnp
from jax import lax
from jax.experimental import pallas as pl
from jax.experimental.pallas import tpu as pltpu

F32 = jnp.float32
BF16 = jnp.bfloat16

D_MODEL = 1024
M_WIDTH = 1024
M_HEADS = 4
M_HD = 256
CONV_K = 4
A_WIDTH = 1024
A_HEADS = 8
A_VHD = 128
A_QKD = 64
N_BUCKETS = 32
MAX_DIST = 128
EPS = 1e-6
LAM_INIT = 0.8 - 0.6 * math.exp(-0.3 * 0)

N_MAIN = 5 * M_WIDTH + 4 * A_WIDTH
GATE_COL0 = 5 * M_WIDTH
GATE_PAD = 128

M_CHUNK = 256
ATT_BLK = 256
ATT_HB = 4
VT_ROWS = A_VHD + 16
LOG2E = math.log2(math.e)
NEG = -1e30

VMEM_LIMIT = 48 * 1024 * 1024


def _nt_dot(a, b):
    return lax.dot_general(a, b, (((1,), (1,)), ((), ())), preferred_element_type=F32)


def _tn_dot(a, b):
    return lax.dot_general(a, b, (((0,), (0,)), ((), ())), preferred_element_type=F32)


def _sigmoid(x):
    return 1.0 / (1.0 + jnp.exp(-x))


def _adaln_kernel(c_ref, w_ref, b_ref, o_ref):
    c = c_ref[...]
    sc = (c * _sigmoid(c)).astype(BF16)
    o_ref[0] = jnp.dot(sc, w_ref[...].astype(BF16), preferred_element_type=F32) + b_ref[0]


def _adaln(c, w_ada, b_ada):
    B, D = c.shape
    return pl.pallas_call(
        _adaln_kernel,
        grid=(3,),
        in_specs=[
            pl.BlockSpec((B, D), lambda j: (0, 0)),
            pl.BlockSpec((D, D), lambda j: (0, j)),
            pl.BlockSpec((1, 1, D), lambda j: (j, 0, 0)),
        ],
        out_specs=pl.BlockSpec((1, B, D), lambda j: (j, 0, 0)),
        out_shape=jax.ShapeDtypeStruct((3, B, D), F32),
        compiler_params=pltpu.CompilerParams(
            dimension_semantics=("arbitrary",), vmem_limit_bytes=VMEM_LIMIT),
        name="adaln",
    )(c, w_ada, b_ada.reshape(3, 1, D))


def _inproj_kernel(x_ref, shift_ref, scale_ref, nw_ref, w_ref, wg_ref, proj_ref, gates_ref, h_scr):
    j = pl.program_id(1)

    @pl.when(j == 0)
    def _():
        x = x_ref[...]
        y = x * lax.rsqrt(jnp.mean(x * x, axis=-1, keepdims=True) + EPS) * nw_ref[...]
        h = (y * (1.0 + scale_ref[0]) + shift_ref[0]).astype(BF16)
        h_scr[...] = h
        gates_ref[...] = jnp.dot(h, wg_ref[...], preferred_element_type=F32)

    proj_ref[...] = jnp.dot(h_scr[...], w_ref[...], preferred_element_type=F32).astype(BF16)


def _inproj(x2d, mod3, norm_w, w_main, w_gate, B, S, tm=1024, tn=1024):
    M, D = x2d.shape
    N = w_main.shape[1]
    tiles_per_seq = S // tm
    return pl.pallas_call(
        _inproj_kernel,
        grid=(M // tm, N // tn),
        in_specs=[
            pl.BlockSpec((tm, D), lambda i, j: (i, 0)),
            pl.BlockSpec((1, 1, D), lambda i, j: (i // tiles_per_seq, 0, 0)),
            pl.BlockSpec((1, 1, D), lambda i, j: (B + i // tiles_per_seq, 0, 0)),
            pl.BlockSpec((1, D), lambda i, j: (0, 0)),
            pl.BlockSpec((D, tn), lambda i, j: (0, j)),
            pl.BlockSpec((D, GATE_PAD), lambda i, j: (0, 0)),
        ],
        out_specs=[
            pl.BlockSpec((tm, tn), lambda i, j: (i, j)),
            pl.BlockSpec((tm, GATE_PAD), lambda i, j: (i, 0)),
        ],
        out_shape=[
            jax.ShapeDtypeStruct((M, N), BF16),
            jax.ShapeDtypeStruct((M, GATE_PAD), F32),
        ],
        scratch_shapes=[pltpu.VMEM((tm, D), BF16)],
        compiler_params=pltpu.CompilerParams(
            dimension_semantics=("arbitrary", "arbitrary"), vmem_limit_bytes=VMEM_LIMIT),
        name="inproj",
    )(x2d, mod3, mod3, norm_w, w_main, w_gate)


def _mlstm_kernel(mq_ref, mk_ref, mv_ref, mo_ref, mz_ref, g_ref, cqw_ref, ckw_ref, nw_ref, gb_ref,
                  o_ref, ct_scr, n_scr, m_scr, qbuf, kbuf):
    L = M_CHUNK
    c = pl.program_id(1)

    @pl.when(c == 0)
    def _():
        ct_scr[...] = jnp.zeros_like(ct_scr)
        n_scr[...] = jnp.zeros_like(n_scr)
        m_scr[...] = jnp.zeros_like(m_scr)
        qbuf[0:8, :] = jnp.zeros((8, M_WIDTH), F32)
        kbuf[0:8, :] = jnp.zeros((8, M_WIDTH), F32)

    def conv_silu(src_ref, buf, w_ref):
        buf[8:L + 8, :] = src_ref[0].astype(F32)
        acc = buf[8:L + 8, :] * w_ref[CONV_K - 1:CONV_K, :]
        for jj in range(CONV_K - 1):
            off = 8 - (CONV_K - 1) + jj
            acc = acc + buf[off:off + L, :] * w_ref[jj:jj + 1, :]
        buf[0:8, :] = buf[L:L + 8, :]
        return acc * _sigmoid(acc)

    q = conv_silu(mq_ref, qbuf, cqw_ref) * (M_HD ** -0.5)
    k = conv_silu(mk_ref, kbuf, ckw_ref)
    q16 = q.astype(BF16)
    k16 = k.astype(BF16)

    g = g_ref[0] + gb_ref[...]
    lf = jnp.minimum(g, 0.0) - jnp.log1p(jnp.exp(-jnp.abs(g)))
    row = lax.broadcasted_iota(jnp.int32, (L, GATE_PAD), 0)
    bc = lf
    sh = 1
    while sh < L:
        bc = bc + jnp.where(row >= sh, pltpu.roll(bc, sh, 0), 0.0)
        sh *= 2
    bsh = pltpu.roll(bc, GATE_PAD - M_HEADS, 1)
    a = g - bsh
    cm = a
    sh = 1
    while sh < L:
        cm = jnp.maximum(cm, jnp.where(row >= sh, pltpu.roll(cm, sh, 0), -jnp.inf))
        sh *= 2
    m_prev = m_scr[...]
    mm = jnp.maximum(cm, m_prev)
    iw = jnp.exp(m_prev - mm)
    eneg = jnp.exp(-(bsh + mm))
    a_t = a.T

    tri = (lax.broadcasted_iota(jnp.int32, (L, L), 1) <= lax.broadcasted_iota(jnp.int32, (L, L), 0))

    for h in range(M_HEADS):
        cs = slice(h * M_HD, (h + 1) * M_HD)
        qh, kh = q16[:, cs], k16[:, cs]
        vh = mv_ref[0, :, cs]
        m_col = mm[:, h:h + 1]
        iw_col = iw[:, h:h + 1]
        dw = jnp.where(tri, jnp.exp(a_t[h:h + 1, :] - m_col), 0.0)
        s = _nt_dot(qh, kh) * dw
        ct = ct_scr[h]
        num = iw_col * jnp.dot(qh, ct.astype(BF16), preferred_element_type=F32) \
            + jnp.dot(s.astype(BF16), vh, preferred_element_type=F32)
        den = iw_col * jnp.sum(q[:, cs] * n_scr[h:h + 1, :], axis=-1, keepdims=True) \
            + jnp.sum(s, axis=-1, keepdims=True)
        hv = num / jnp.maximum(jnp.abs(den), eneg[:, h:h + 1])
        mu = jnp.mean(hv, axis=-1, keepdims=True)
        hc = hv - mu
        var = jnp.mean(hc * hc, axis=-1, keepdims=True)
        hn = hc * lax.rsqrt(var + EPS) * nw_ref[:, cs]
        og = _sigmoid(mo_ref[0, :, cs].astype(F32))
        z = mz_ref[0, :, cs].astype(F32)
        o_ref[0, :, cs] = (og * hn * (z * _sigmoid(z))).astype(BF16)

        m_last = mm[L - 1:L, h:h + 1]
        w_col = jnp.exp(a[:, h:h + 1] - m_last)
        decay = iw[L - 1:L, h:h + 1]
        wv = (w_col * vh.astype(F32)).astype(BF16)
        ct_scr[h] = decay * ct + _tn_dot(kh, wv)
        n_scr[h:h + 1, :] = decay * n_scr[h:h + 1, :] + jnp.sum(w_col * k[:, cs], axis=0, keepdims=True)

    m_scr[...] = bsh[L - 1:L, :] + mm[L - 1:L, :]


def _mlstm(proj3, gates3, conv_q_w, conv_k_w, m_norm_w, gbias):
    B, S, _ = proj3.shape
    L = M_CHUNK
    W = M_WIDTH

    def col(cb):
        return pl.BlockSpec((1, L, W), lambda b, c, cb=cb: (b, c, cb))

    return pl.pallas_call(
        _mlstm_kernel,
        grid=(B, S // L),
        in_specs=[
            col(0), col(1), col(2), col(3), col(4),
            pl.BlockSpec((1, L, GATE_PAD), lambda b, c: (b, c, 0)),
            pl.BlockSpec((CONV_K, W), lambda b, c: (0, 0)),
            pl.BlockSpec((CONV_K, W), lambda b, c: (0, 0)),
            pl.BlockSpec((1, W), lambda b, c: (0, 0)),
            pl.BlockSpec((1, GATE_PAD), lambda b, c: (0, 0)),
        ],
        out_specs=pl.BlockSpec((1, L, W), lambda b, c: (b, c, 0)),
        out_shape=jax.ShapeDtypeStruct((B, S, W), BF16),
        scratch_shapes=[
            pltpu.VMEM((M_HEADS, M_HD, M_HD), F32),
            pltpu.VMEM((8, M_HD), F32),
            pltpu.VMEM((1, GATE_PAD), F32),
            pltpu.VMEM((L + 8, W), F32),
            pltpu.VMEM((L + 8, W), F32),
        ],
        compiler_params=pltpu.CompilerParams(
            dimension_semantics=("arbitrary", "arbitrary"), vmem_limit_bytes=VMEM_LIMIT),
        name="mlstm",
    )(proj3, proj3, proj3, proj3, proj3, gates3, conv_q_w, conv_k_w, m_norm_w, gbias)


def _bucket_tiles():
    T = ATT_BLK
    kk = np.arange(T)[:, None]
    qq = np.arange(T)[None, :]
    out = []
    for base in (0, T):
        n = base + qq - kk
        max_exact = N_BUCKETS // 2
        nf = np.maximum(n, 1).astype(np.float64)
        large = max_exact + (np.log(nf / max_exact) / math.log(MAX_DIST / max_exact)
                             * (N_BUCKETS - max_exact)).astype(np.int64)
        large = np.minimum(large, N_BUCKETS - 1)
        bk = np.where(n < max_exact, n, large)
        bk = np.where(n < 0, -1, bk)
        out.append(bk)
    return np.stack(out).astype(np.int32)


def _attn_kernel(relb_ref, bucket_ref, lam_ref, anw_ref, q_ref, k_ref, v_ref, z_ref, o_ref,
                 bias_scr, vt_scr, qq_scr, s0_scr, s1_scr, acc_scr):
    T = ATT_BLK
    HB = ATT_HB
    S = k_ref.shape[1]
    NQ = S // T
    hg = pl.program_id(0)
    b = pl.program_id(1)
    heads = [slice(hb * A_VHD, (hb + 1) * A_VHD) for hb in range(HB)]

    @pl.when(b == 0)
    def _():
        for hb in range(HB):
            h = hg * HB + hb
            far = relb_ref[N_BUCKETS - 1, h]
            for t in range(2):
                bk = bucket_ref[t]
                tile = jnp.zeros((T, T), F32)
                for bb in range(N_BUCKETS - 1):
                    tile = jnp.where(bk == bb, (relb_ref[bb, h] - far) * LOG2E, tile)
                tile = jnp.where(bk < 0, NEG, tile)
                bias_scr[hb, t] = jnp.concatenate([tile, tile], axis=1)

    lane = lax.broadcasted_iota(jnp.int32, (T, A_VHD), 1)
    ones_rows = (lax.broadcasted_iota(jnp.int32, (VT_ROWS - A_VHD, T), 0) == 0).astype(BF16)
    for hb in range(HB):
        for ci in range(NQ):
            rows = slice(ci * T, (ci + 1) * T)
            vt_scr[hb, 0:A_VHD, rows] = v_ref[0, rows, heads[hb]].astype(F32).T.astype(BF16)
            vt_scr[hb, A_VHD:VT_ROWS, rows] = ones_rows
            qs = (q_ref[0, rows, heads[hb]].astype(F32) * (A_QKD ** -0.5 * LOG2E)).astype(BF16)
            zero = jnp.zeros_like(qs)
            qq_scr[hb, ci, 0:T, :] = jnp.where(lane < A_QKD, qs, zero)
            qq_scr[hb, ci, T:2 * T, :] = jnp.where(lane >= A_QKD, qs, zero)
    acc_scr[...] = jnp.zeros_like(acc_scr)

    s_bufs = (s0_scr, s1_scr)

    def issue(buf, qb, kj, hb):
        start = pl.multiple_of(kj * T, T)
        s_bufs[buf][hb] = _nt_dot(k_ref[0, pl.ds(start, T), heads[hb]], qq_scr[hb, qb])

    def run_stage(cur, qb, kj_next, kj, bias_idx, ms):
        start = pl.multiple_of(kj * T, T)
        out = []
        for hb in range(HB):
            s = s_bufs[cur][hb]
            if bias_idx is not None:
                s = s + bias_scr[hb, bias_idx]
            m_old = ms[hb]
            m_new = jnp.maximum(m_old, jnp.max(s, axis=0, keepdims=True))
            alpha = jnp.exp2(m_old - m_new)
            p = jnp.exp2(s - m_new).astype(BF16)
            issue(1 - cur, qb, kj_next, hb)
            pv = jnp.dot(vt_scr[hb, :, pl.ds(start, T)], p, preferred_element_type=F32)
            acc_scr[hb] = alpha * acc_scr[hb] + pv
            out.append(m_new)
        return tuple(out)

    lam_p = lam_ref[...]
    lam = (jnp.exp(jnp.sum(lam_p[0:1] * lam_p[1:2], axis=-1, keepdims=True))
           - jnp.exp(jnp.sum(lam_p[2:3] * lam_p[3:4], axis=-1, keepdims=True)) + LAM_INIT)

    def finalize(qi):
        rows = pl.ds(pl.multiple_of(qi * T, T), T)
        for hb in range(HB):
            acc = acc_scr[hb]
            o = acc[0:A_VHD] / acc[A_VHD:A_VHD + 1]
            d = (o[:, 0:T] - lam * o[:, T:2 * T]).T
            y = d * lax.rsqrt(jnp.mean(d * d, axis=-1, keepdims=True) + EPS) * anw_ref[...] * (1.0 - LAM_INIT)
            z = z_ref[0, rows, heads[hb]].astype(F32)
            o_ref[0, rows, heads[hb]] = (y * (z * _sigmoid(z))).astype(BF16)

    m_init = tuple(jnp.full((1, 2 * T), NEG, F32) for _ in range(HB))

    def stage(par, qb, kj_next, kj, bias_idx, ms):
        def run(cur, ms):
            return run_stage(cur, qb, kj_next, kj, bias_idx, ms)

        return lax.cond(par == 0, functools.partial(run, 0), functools.partial(run, 1), ms)

    for hb in range(HB):
        issue(0, 0, 0, hb)
    stage(0, 1, 0, 0, 0, m_init)
    finalize(0)

    def qblock(qi, carry):
        t0 = lax.shift_right_logical(qi * (qi + 1), 1)
        ms = lax.fori_loop(0, qi - 1, lambda j, ms: stage((t0 + j) & 1, qi, j + 1, j, None, ms), m_init)
        ms = stage((t0 + qi - 1) & 1, qi, qi, qi - 1, 1, ms)
        stage((t0 + qi) & 1, jnp.minimum(qi + 1, NQ - 1), 0, qi, 0, ms)
        finalize(qi)
        return carry

    lax.fori_loop(1, NQ, qblock, 0)


def _attn(proj3, rel_bias, lam_p, a_norm_w):
    B, S, _ = proj3.shape
    T = ATT_BLK
    HB = ATT_HB
    W = HB * A_VHD
    q0 = GATE_COL0 // W
    k0 = q0 + A_HEADS // HB
    v0 = k0 + A_HEADS // HB
    z0 = v0 + A_HEADS // HB
    buckets = jnp.asarray(_bucket_tiles())
    return pl.pallas_call(
        _attn_kernel,
        grid=(A_HEADS // HB, B),
        in_specs=[
            pl.BlockSpec(memory_space=pltpu.SMEM),
            pl.BlockSpec((2, T, T), lambda h, b: (0, 0, 0)),
            pl.BlockSpec((4, A_QKD), lambda h, b: (0, 0)),
            pl.BlockSpec((1, A_VHD), lambda h, b: (0, 0)),
            pl.BlockSpec((1, S, W), lambda h, b: (b, 0, q0 + h)),
            pl.BlockSpec((1, S, W), lambda h, b: (b, 0, k0 + h)),
            pl.BlockSpec((1, S, W), lambda h, b: (b, 0, v0 + h)),
            pl.BlockSpec((1, S, W), lambda h, b: (b, 0, z0 + h)),
        ],
        out_specs=pl.BlockSpec((1, S, W), lambda h, b: (b, 0, h)),
        out_shape=jax.ShapeDtypeStruct((B, S, A_WIDTH), BF16),
        scratch_shapes=[
            pltpu.VMEM((HB, 2, T, 2 * T), F32),
            pltpu.VMEM((HB, VT_ROWS, S), BF16),
            pltpu.VMEM((HB, S // T, 2 * T, A_VHD), BF16),
            pltpu.VMEM((HB, T, 2 * T), F32),
            pltpu.VMEM((HB, T, 2 * T), F32),
            pltpu.VMEM((HB, VT_ROWS, 2 * T), F32),
        ],
        compiler_params=pltpu.CompilerParams(
            dimension_semantics=("arbitrary", "arbitrary"), vmem_limit_bytes=VMEM_LIMIT),
        name="attn",
    )(rel_bias, buckets, lam_p, a_norm_w, proj3, proj3, proj3, proj3)


def _outproj_kernel(hm_ref, ha_ref, x_ref, gate_ref, w_ref, fw_ref, o_ref):
    y = jnp.dot(hm_ref[...], w_ref[0:M_WIDTH, :], preferred_element_type=F32)
    y = y + jnp.dot(ha_ref[...], w_ref[M_WIDTH:, :], preferred_element_type=F32)
    r = x_ref[...] + gate_ref[0] * y
    o_ref[...] = r * lax.rsqrt(jnp.mean(r * r, axis=-1, keepdims=True) + EPS) * fw_ref[...]


def _outproj(hm2d, ha2d, x2d, mod3, w_out, final_w, B, S, tm=512):
    M, D = x2d.shape
    tiles_per_seq = S // tm
    return pl.pallas_call(
        _outproj_kernel,
        grid=(M // tm,),
        in_specs=[
            pl.BlockSpec((tm, M_WIDTH), lambda i: (i, 0)),
            pl.BlockSpec((tm, A_WIDTH), lambda i: (i, 0)),
            pl.BlockSpec((tm, D), lambda i: (i, 0)),
            pl.BlockSpec((1, 1, D), lambda i: (2 * B + i // tiles_per_seq, 0, 0)),
            pl.BlockSpec((M_WIDTH + A_WIDTH, D), lambda i: (0, 0)),
            pl.BlockSpec((1, D), lambda i: (0, 0)),
        ],
        out_specs=pl.BlockSpec((tm, D), lambda i: (i, 0)),
        out_shape=jax.ShapeDtypeStruct((M, D), F32),
        compiler_params=pltpu.CompilerParams(
            dimension_semantics=("arbitrary",), vmem_limit_bytes=VMEM_LIMIT),
        name="outproj",
    )(hm2d, ha2d, x2d, mod3, w_out, final_w)


def kernel(x, c, norm_w, w_ada, b_ada, w_in, b_i, b_f, conv_q_w, conv_k_w, m_norm_w,
           lambda_q1, lambda_k1, lambda_q2, lambda_k2, a_norm_w, rel_bias, w_out, final_norm_w):
    B, S, D = x.shape
    assert norm_w.shape[0] == 1, "single layer only"
    x2d = x.reshape(B * S, D)

    mod = _adaln(c, w_ada[0], b_ada[0])
    mod3 = mod.reshape(3 * B, 1, D)

    w_in0 = w_in[0]
    w_main = jnp.concatenate([w_in0[:, :GATE_COL0], w_in0[:, GATE_COL0 + 2 * M_HEADS:]], axis=1).astype(BF16)
    w_gate = jnp.pad(w_in0[:, GATE_COL0:GATE_COL0 + 2 * M_HEADS],
                     ((0, 0), (0, GATE_PAD - 2 * M_HEADS))).astype(BF16)
    proj, gates = _inproj(x2d, mod3, norm_w, w_main, w_gate, B, S)
    proj3 = proj.reshape(B, S, N_MAIN)
    gates3 = gates.reshape(B, S, GATE_PAD)

    gbias = jnp.pad(jnp.concatenate([b_i[0], b_f[0]]), (0, GATE_PAD - 2 * M_HEADS)).reshape(1, GATE_PAD)
    hm = _mlstm(proj3, gates3, conv_q_w[0], conv_k_w[0], m_norm_w, gbias)

    lam_p = jnp.stack([lambda_q1[0], lambda_k1[0], lambda_q2[0], lambda_k2[0]])
    ha = _attn(proj3, rel_bias, lam_p, a_norm_w)

    out = _outproj(hm.reshape(B * S, M_WIDTH), ha.reshape(B * S, A_WIDTH), x2d, mod3,
                   w_out[0].astype(BF16), final_norm_w.reshape(1, D), B, S)
    return out.reshape(B, S, D)
```

```python
import functools
import math

import numpy as np
import jax
import jax.numpy as jnp
from jax import lax
from jax.experimental import pallas as pl
from jax.experimental.pallas import tpu as pltpu

F32 = jnp.float32
BF16 = jnp.bfloat16

D_MODEL = 1024
M_WIDTH = 1024
M_HEADS = 4
M_HD = 256
CONV_K = 4
A_WIDTH = 1024
A_HEADS = 8
A_VHD = 128
A_QKD = 64
N_BUCKETS = 32
MAX_DIST = 128
EPS = 1e-6
LAM_INIT = 0.8 - 0.6 * math.exp(-0.3 * 0)

N_MAIN = 5 * M_WIDTH + 4 * A_WIDTH
GATE_COL0 = 5 * M_WIDTH
GATE_PAD = 128

M_CHUNK = 256
M_AUG = M_HD + 128
ATT_BLK = 256
ATT_HB = 4
VT_ROWS = A_VHD + 16
LOG2E = math.log2(math.e)
NEG = -1e30

VMEM_LIMIT = 48 * 1024 * 1024


def _nt_dot(a, b):
    return lax.dot_general(a, b, (((1,), (1,)), ((), ())), preferred_element_type=F32)


def _tn_dot(a, b):
    return lax.dot_general(a, b, (((0,), (0,)), ((), ())), preferred_element_type=F32)


def _sigmoid(x):
    return 1.0 / (1.0 + jnp.exp(-x))


def _adaln_kernel(c_ref, w_ref, b_ref, o_ref):
    c = c_ref[...]
    sc = (c * _sigmoid(c)).astype(BF16)
    o_ref[0] = jnp.dot(sc, w_ref[...].astype(BF16), preferred_element_type=F32) + b_ref[0]


def _adaln(c, w_ada, b_ada):
    B, D = c.shape
    return pl.pallas_call(
        _adaln_kernel,
        grid=(3,),
        in_specs=[
            pl.BlockSpec((B, D), lambda j: (0, 0)),
            pl.BlockSpec((D, D), lambda j: (0, j)),
            pl.BlockSpec((1, 1, D), lambda j: (j, 0, 0)),
        ],
        out_specs=pl.BlockSpec((1, B, D), lambda j: (j, 0, 0)),
        out_shape=jax.ShapeDtypeStruct((3, B, D), F32),
        compiler_params=pltpu.CompilerParams(
            dimension_semantics=("arbitrary",), vmem_limit_bytes=VMEM_LIMIT),
        name="adaln",
    )(c, w_ada, b_ada.reshape(3, 1, D))


IN_TN = 1024
IN_CHUNK = 256
IN_TILES = 9
P_TILES = 7
P_COLS = P_TILES * IN_TN


def _inproj_kernel(cs_ref, x_ref, shift_ref, scale_ref, nw_ref, w_ref, wg_ref, cw_ref,
                   p_ref, g_ref, gates_ref, h_scr, cbuf, tail_scr, *, tiles_per_seq):
    i = pl.program_id(0)
    j = pl.program_id(1)
    tm, tn = p_ref.shape

    @pl.when(j == 0)
    def _():
        x = x_ref[...]
        y = x * lax.rsqrt(jnp.mean(x * x, axis=-1, keepdims=True) + EPS) * nw_ref[...]
        h = (y * (1.0 + scale_ref[0]) + shift_ref[0]).astype(BF16)
        h_scr[...] = h
        gates_ref[...] = jnp.dot(h, wg_ref[...], preferred_element_type=F32)

    @pl.when((i == 0) & (j == 0))
    def _():
        tail_scr[...] = jnp.zeros_like(tail_scr)

    cw = IN_CHUNK
    chunks = [slice(c * cw, (c + 1) * cw) for c in range(tn // cw)]

    def acc(cols):
        return jnp.dot(h_scr[...], w_ref[:, cols], preferred_element_type=F32)

    @pl.when(j < 2)
    def _():
        first = (i % tiles_per_seq) == 0
        for cols in chunks:
            cbuf[8:tm + 8, cols] = acc(cols)
            cbuf[0:8, cols] = jnp.where(first, 0.0, tail_scr[j, :, cols])
            y = cbuf[8:tm + 8, cols] * cw_ref[0, CONV_K - 1:CONV_K, cols]
            for jj in range(CONV_K - 1):
                off = 8 - (CONV_K - 1) + jj
                y = y + cbuf[off:off + tm, cols] * cw_ref[0, jj:jj + 1, cols]
            tail_scr[j, :, cols] = cbuf[tm:tm + 8, cols]
            p_ref[:, cols] = (y * _sigmoid(y) * cs_ref[j]).astype(BF16)

    @pl.when((j == 2) | (j == 5) | (j == 6) | (j == 7))
    def _():
        for cols in chunks:
            p_ref[:, cols] = (acc(cols) * cs_ref[j]).astype(BF16)

    @pl.when((j == 3) | (j == 4))
    def _():
        for c in range(tn // 2 // cw):
            o = acc(slice(c * cw, (c + 1) * cw))
            z = acc(slice(tn // 2 + c * cw, tn // 2 + (c + 1) * cw))
            g_ref[:, c * cw:(c + 1) * cw] = (_sigmoid(o) * (z * _sigmoid(z))).astype(BF16)

    @pl.when(j == 8)
    def _():
        for cols in chunks:
            a = acc(cols)
            p_ref[:, cols] = (a * _sigmoid(a)).astype(BF16)


def _inproj(x2d, mod3, norm_w, w_main, w_gate, conv_w, col_scale, B, S, tm=1024):
    M, D = x2d.shape
    tn = IN_TN
    tiles_per_seq = S // tm
    return pl.pallas_call(
        functools.partial(_inproj_kernel, tiles_per_seq=tiles_per_seq),
        grid=(M // tm, IN_TILES),
        in_specs=[
            pl.BlockSpec(memory_space=pltpu.SMEM),
            pl.BlockSpec((tm, D), lambda i, j: (i, 0)),
            pl.BlockSpec((1, 1, D), lambda i, j: (i // tiles_per_seq, 0, 0)),
            pl.BlockSpec((1, 1, D), lambda i, j: (B + i // tiles_per_seq, 0, 0)),
            pl.BlockSpec((1, D), lambda i, j: (0, 0)),
            pl.BlockSpec((D, tn), lambda i, j: (0, j)),
            pl.BlockSpec((D, GATE_PAD), lambda i, j: (0, 0)),
            pl.BlockSpec((1, CONV_K, tn), lambda i, j: (jnp.minimum(j, 1), 0, 0)),
        ],
        out_specs=[
            pl.BlockSpec((tm, tn), lambda i, j: (i, jnp.where(j < 3, j, jnp.maximum(j - 2, 3)))),
            pl.BlockSpec((tm, tn // 2), lambda i, j: (i, jnp.clip(j - 3, 0, 1))),
            pl.BlockSpec((tm, GATE_PAD), lambda i, j: (i, 0)),
        ],
        out_shape=[
            jax.ShapeDtypeStruct((M, P_COLS), BF16),
            jax.ShapeDtypeStruct((M, M_WIDTH), BF16),
            jax.ShapeDtypeStruct((M, GATE_PAD), F32),
        ],
        scratch_shapes=[
            pltpu.VMEM((tm, D), BF16),
            pltpu.VMEM((tm + 8, tn), F32),
            pltpu.VMEM((2, 8, tn), F32),
        ],
        compiler_params=pltpu.CompilerParams(
            dimension_semantics=("arbitrary", "arbitrary"), vmem_limit_bytes=VMEM_LIMIT),
        name="inproj",
    )(col_scale, x2d, mod3, mod3, norm_w, w_main, w_gate, conv_w)


def _mlstm_kernel(q_ref, k_ref, v_ref, og_ref, g_ref, nw_ref, gb_ref, o_ref, ct_scr, m_scr):
    L = M_CHUNK
    c = pl.program_id(1)

    @pl.when(c == 0)
    def _():
        ct_scr[...] = jnp.zeros_like(ct_scr)
        m_scr[...] = jnp.zeros_like(m_scr)

    g = g_ref[0] + gb_ref[...]
    lf = jnp.minimum(g, 0.0) - jnp.log1p(jnp.exp(-jnp.abs(g)))
    row = lax.broadcasted_iota(jnp.int32, (L, GATE_PAD), 0)
    bc = lf
    sh = 1
    while sh < L:
        bc = bc + jnp.where(row >= sh, pltpu.roll(bc, sh, 0), 0.0)
        sh *= 2
    bsh = pltpu.roll(bc, GATE_PAD - M_HEADS, 1)
    a = g - bsh
    cm = a
    sh = 1
    while sh < L:
        cm = jnp.maximum(cm, jnp.where(row >= sh, pltpu.roll(cm, sh, 0), -jnp.inf))
        sh *= 2
    m_prev = m_scr[...]
    mm = jnp.maximum(cm, m_prev)
    iw = jnp.exp(m_prev - mm)
    eneg = jnp.exp(-(bsh + mm))
    a_t = a.T

    tri = (lax.broadcasted_iota(jnp.int32, (L, L), 1) <= lax.broadcasted_iota(jnp.int32, (L, L), 0))

    ones_col = (lax.broadcasted_iota(jnp.int32, (L, M_AUG - M_HD), 1) == 0).astype(BF16)

    hs = range(M_HEADS)
    cols = [slice(h * M_HD, (h + 1) * M_HD) for h in hs]
    qs = [q_ref[0, :, cols[h]] for h in hs]
    ks = [k_ref[0, :, cols[h]] for h in hs]
    v_aug = [jnp.concatenate([v_ref[0, :, cols[h]], ones_col], axis=1) for h in hs]
    cts = [ct_scr[h] for h in hs]

    s_raw = [_nt_dot(qs[h], ks[h]) for h in hs]
    r_state = [jnp.dot(qs[h], cts[h].astype(BF16), preferred_element_type=F32) for h in hs]
    for h in hs:
        m_last = mm[L - 1:L, h:h + 1]
        w_col = jnp.exp(a[:, h:h + 1] - m_last)
        decay = iw[L - 1:L, h:h + 1]
        wv = (w_col * v_aug[h].astype(F32)).astype(BF16)
        ct_scr[h] = decay * cts[h] + _tn_dot(ks[h], wv)

    s16 = []
    for h in hs:
        dw = jnp.where(tri, jnp.exp(a_t[h:h + 1, :] - mm[:, h:h + 1]), 0.0)
        s16.append((s_raw[h] * dw).astype(BF16))
    rs = [iw[:, h:h + 1] * r_state[h] + jnp.dot(s16[h], v_aug[h], preferred_element_type=F32) for h in hs]

    for h in hs:
        num = rs[h][:, 0:M_HD]
        d = jnp.maximum(jnp.abs(rs[h][:, M_HD:M_HD + 1]), eneg[:, h:h + 1])
        hc = num - jnp.mean(num, axis=-1, keepdims=True)
        var = jnp.mean(hc * hc, axis=-1, keepdims=True)
        hn = hc * lax.rsqrt(var + EPS * d * d) * nw_ref[:, cols[h]]
        o_ref[0, :, cols[h]] = (hn * og_ref[0, :, cols[h]].astype(F32)).astype(BF16)

    m_scr[...] = bsh[L - 1:L, :] + mm[L - 1:L, :]


def _mlstm(p3, og3, gates3, m_norm_w, gbias):
    B, S, _ = p3.shape
    L = M_CHUNK
    W = M_WIDTH

    def col(cb):
        return pl.BlockSpec((1, L, W), lambda b, c, cb=cb: (b, c, cb))

    return pl.pallas_call(
        _mlstm_kernel,
        grid=(B, S // L),
        in_specs=[
            col(0), col(1), col(2),
            pl.BlockSpec((1, L, W), lambda b, c: (b, c, 0)),
            pl.BlockSpec((1, L, GATE_PAD), lambda b, c: (b, c, 0)),
            pl.BlockSpec((1, W), lambda b, c: (0, 0)),
            pl.BlockSpec((1, GATE_PAD), lambda b, c: (0, 0)),
        ],
        out_specs=pl.BlockSpec((1, L, W), lambda b, c: (b, c, 0)),
        out_shape=jax.ShapeDtypeStruct((B, S, W), BF16),
        scratch_shapes=[
            pltpu.VMEM((M_HEADS, M_HD, M_AUG), F32),
            pltpu.VMEM((1, GATE_PAD), F32),
        ],
        compiler_params=pltpu.CompilerParams(
            dimension_semantics=("arbitrary", "arbitrary"), vmem_limit_bytes=VMEM_LIMIT),
        name="mlstm",
    )(p3, p3, p3, og3, gates3, m_norm_w, gbias)


def _bucket_tiles():
    T = ATT_BLK
    kk = np.arange(T)[:, None]
    qq = np.arange(T)[None, :]
    out = []
    for base in (0, T):
        n = base + qq - kk
        max_exact = N_BUCKETS // 2
        nf = np.maximum(n, 1).astype(np.float64)
        large = max_exact + (np.log(nf / max_exact) / math.log(MAX_DIST / max_exact)
                             * (N_BUCKETS - max_exact)).astype(np.int64)
        large = np.minimum(large, N_BUCKETS - 1)
        bk = np.where(n < max_exact, n, large)
        bk = np.where(n < 0, -1, bk)
        out.append(bk)
    return np.stack(out).astype(np.int32)


def _attn_kernel(relb_ref, bucket_ref, lam_ref, anw_ref, q_ref, k_ref, v_ref, z_ref, o_ref,
                 bias_scr, vt_scr, qq_scr, s0_scr, s1_scr, acc_scr):
    T = ATT_BLK
    HB = ATT_HB
    S = k_ref.shape[1]
    NQ = S // T
    hg = pl.program_id(0)
    b = pl.program_id(1)
    heads = [slice(hb * A_VHD, (hb + 1) * A_VHD) for hb in range(HB)]

    @pl.when(b == 0)
    def _():
        for hb in range(HB):
            h = hg * HB + hb
            far = relb_ref[N_BUCKETS - 1, h]
            for t in range(2):
                bk = bucket_ref[t]
                tile = jnp.zeros((T, T), F32)
                for bb in range(N_BUCKETS - 1):
                    tile = jnp.where(bk == bb, (relb_ref[bb, h] - far) * LOG2E, tile)
                tile = jnp.where(bk < 0, NEG, tile)
                bias_scr[hb, t] = jnp.concatenate([tile, tile], axis=1)

    lane = lax.broadcasted_iota(jnp.int32, (T, A_VHD), 1)
    ones_rows = (lax.broadcasted_iota(jnp.int32, (VT_ROWS - A_VHD, T), 0) == 0).astype(BF16)
    for hb in range(HB):
        for ci in range(NQ):
            rows = slice(ci * T, (ci + 1) * T)
            vt_scr[hb, 0:A_VHD, rows] = v_ref[0, rows, heads[hb]].astype(F32).T.astype(BF16)
            vt_scr[hb, A_VHD:VT_ROWS, rows] = ones_rows
            qs = q_ref[0, rows, heads[hb]]
            zero = jnp.zeros_like(qs)
            qq_scr[hb, ci, 0:T, :] = jnp.where(lane < A_QKD, qs, zero)
            qq_scr[hb, ci, T:2 * T, :] = jnp.where(lane >= A_QKD, qs, zero)
    acc_scr[...] = jnp.zeros_like(acc_scr)

    s_bufs = (s0_scr, s1_scr)

    def issue(buf, qb, kj, hb):
        start = pl.multiple_of(kj * T, T)
        s_bufs[buf][hb] = _nt_dot(k_ref[0, pl.ds(start, T), heads[hb]], qq_scr[hb, qb])

    def run_stage(cur, qb, kj_next, kj, bias_idx, ms):
        start = pl.multiple_of(kj * T, T)
        out = []
        for hb in range(HB):
            s = s_bufs[cur][hb]
            if bias_idx is not None:
                s = s + bias_scr[hb, bias_idx]
            m_old = ms[hb]
            m_new = jnp.maximum(m_old, jnp.max(s, axis=0, keepdims=True))
            alpha = jnp.exp2(m_old - m_new)
            p = jnp.exp2(s - m_new).astype(BF16)
            issue(1 - cur, qb, kj_next, hb)
            pv = jnp.dot(vt_scr[hb, :, pl.ds(start, T)], p, preferred_element_type=F32)
            acc_scr[hb] = alpha * acc_scr[hb] + pv
            out.append(m_new)
        return tuple(out)

    lam_p = lam_ref[...]
    lam = (jnp.exp(jnp.sum(lam_p[0:1] * lam_p[1:2], axis=-1, keepdims=True))
           - jnp.exp(jnp.sum(lam_p[2:3] * lam_p[3:4], axis=-1, keepdims=True)) + LAM_INIT)

    def finalize(qi):
        rows = pl.ds(pl.multiple_of(qi * T, T), T)
        for hb in range(HB):
            acc = acc_scr[hb]
            o = acc[0:A_VHD] / acc[A_VHD:A_VHD + 1]
            d = (o[:, 0:T] - lam * o[:, T:2 * T]).T
            y = d * lax.rsqrt(jnp.mean(d * d, axis=-1, keepdims=True) + EPS) * anw_ref[...] * (1.0 - LAM_INIT)
            o_ref[0, rows, heads[hb]] = (y * z_ref[0, rows, heads[hb]].astype(F32)).astype(BF16)

    m_init = tuple(jnp.full((1, 2 * T), NEG, F32) for _ in range(HB))

    def stage(par, qb, kj_next, kj, bias_idx, ms):
        def run(cur, ms):
            return run_stage(cur, qb, kj_next, kj, bias_idx, ms)

        return lax.cond(par == 0, functools.partial(run, 0), functools.partial(run, 1), ms)

    for hb in range(HB):
        issue(0, 0, 0, hb)
    stage(0, 1, 0, 0, 0, m_init)
    finalize(0)

    def qblock(qi, carry):
        t0 = lax.shift_right_logical(qi * (qi + 1), 1)
        ms = lax.fori_loop(0, qi - 1, lambda j, ms: stage((t0 + j) & 1, qi, j + 1, j, None, ms), m_init)
        ms = stage((t0 + qi - 1) & 1, qi, qi, qi - 1, 1, ms)
        stage((t0 + qi) & 1, jnp.minimum(qi + 1, NQ - 1), 0, qi, 0, ms)
        finalize(qi)
        return carry

    lax.fori_loop(1, NQ, qblock, 0)


def _attn(proj3, rel_bias, lam_p, a_norm_w):
    B, S, _ = proj3.shape
    T = ATT_BLK
    HB = ATT_HB
    W = HB * A_VHD
    q0 = 3 * IN_TN // W
    k0 = q0 + A_HEADS // HB
    v0 = k0 + A_HEADS // HB
    z0 = v0 + A_HEADS // HB
    buckets = jnp.asarray(_bucket_tiles())
    return pl.pallas_call(
        _attn_kernel,
        grid=(A_HEADS // HB, B),
        in_specs=[
            pl.BlockSpec(memory_space=pltpu.SMEM),
            pl.BlockSpec((2, T, T), lambda h, b: (0, 0, 0)),
            pl.BlockSpec((4, A_QKD), lambda h, b: (0, 0)),
            pl.BlockSpec((1, A_VHD), lambda h, b: (0, 0)),
            pl.BlockSpec((1, S, W), lambda h, b: (b, 0, q0 + h)),
            pl.BlockSpec((1, S, W), lambda h, b: (b, 0, k0 + h)),
            pl.BlockSpec((1, S, W), lambda h, b: (b, 0, v0 + h)),
            pl.BlockSpec((1, S, W), lambda h, b: (b, 0, z0 + h)),
        ],
        out_specs=pl.BlockSpec((1, S, W), lambda h, b: (b, 0, h)),
        out_shape=jax.ShapeDtypeStruct((B, S, A_WIDTH), BF16),
        scratch_shapes=[
            pltpu.VMEM((HB, 2, T, 2 * T), F32),
            pltpu.VMEM((HB, VT_ROWS, S), BF16),
            pltpu.VMEM((HB, S // T, 2 * T, A_VHD), BF16),
            pltpu.VMEM((HB, T, 2 * T), F32),
            pltpu.VMEM((HB, T, 2 * T), F32),
            pltpu.VMEM((HB, VT_ROWS, 2 * T), F32),
        ],
        compiler_params=pltpu.CompilerParams(
            dimension_semantics=("arbitrary", "arbitrary"), vmem_limit_bytes=VMEM_LIMIT),
        name="attn",
    )(rel_bias, buckets, lam_p, a_norm_w, proj3, proj3, proj3, proj3)


def _outproj_kernel(hm_ref, ha_ref, x_ref, gate_ref, w_ref, fw_ref, o_ref):
    y = jnp.dot(hm_ref[...], w_ref[0:M_WIDTH, :], preferred_element_type=F32)
    y = y + jnp.dot(ha_ref[...], w_ref[M_WIDTH:, :], preferred_element_type=F32)
    r = x_ref[...] + gate_ref[0] * y
    o_ref[...] = r * lax.rsqrt(jnp.mean(r * r, axis=-1, keepdims=True) + EPS) * fw_ref[...]


def _outproj(hm2d, ha2d, x2d, mod3, w_out, final_w, B, S, tm=512):
    M, D = x2d.shape
    tiles_per_seq = S // tm
    return pl.pallas_call(
        _outproj_kernel,
        grid=(M // tm,),
        in_specs=[
            pl.BlockSpec((tm, M_WIDTH), lambda i: (i, 0)),
            pl.BlockSpec((tm, A_WIDTH), lambda i: (i, 0)),
            pl.BlockSpec((tm, D), lambda i: (i, 0)),
            pl.BlockSpec((1, 1, D), lambda i: (2 * B + i // tiles_per_seq, 0, 0)),
            pl.BlockSpec((M_WIDTH + A_WIDTH, D), lambda i: (0, 0)),
            pl.BlockSpec((1, D), lambda i: (0, 0)),
        ],
        out_specs=pl.BlockSpec((tm, D), lambda i: (i, 0)),
        out_shape=jax.ShapeDtypeStruct((M, D), F32),
        compiler_params=pltpu.CompilerParams(
            dimension_semantics=("arbitrary",), vmem_limit_bytes=VMEM_LIMIT),
        name="outproj",
    )(hm2d, ha2d, x2d, mod3, w_out, final_w)


def kernel(x, c, norm_w, w_ada, b_ada, w_in, b_i, b_f, conv_q_w, conv_k_w, m_norm_w,
           lambda_q1, lambda_k1, lambda_q2, lambda_k2, a_norm_w, rel_bias, w_out, final_norm_w):
    B, S, D = x.shape
    assert norm_w.shape[0] == 1, "single layer only"
    x2d = x.reshape(B * S, D)

    mod = _adaln(c, w_ada[0], b_ada[0])
    mod3 = mod.reshape(3 * B, 1, D)

    w_in0 = w_in[0]
    half = M_WIDTH // 2
    mo0, mz0 = 3 * M_WIDTH, 4 * M_WIDTH
    a0 = GATE_COL0 + 2 * M_HEADS
    w_main = jnp.concatenate([
        w_in0[:, :3 * M_WIDTH],
        w_in0[:, mo0:mo0 + half], w_in0[:, mz0:mz0 + half],
        w_in0[:, mo0 + half:mo0 + M_WIDTH], w_in0[:, mz0 + half:mz0 + M_WIDTH],
        w_in0[:, a0:]], axis=1).astype(BF16)
    w_gate = jnp.pad(w_in0[:, GATE_COL0:a0], ((0, 0), (0, GATE_PAD - 2 * M_HEADS))).astype(BF16)
    conv_w = jnp.stack([conv_q_w[0], conv_k_w[0]])
    col_scale = jnp.ones((IN_TILES,), F32).at[0].set(M_HD ** -0.5).at[5].set(A_QKD ** -0.5 * LOG2E)
    p, og, gates = _inproj(x2d, mod3, norm_w, w_main, w_gate, conv_w, col_scale, B, S)
    p3 = p.reshape(B, S, P_COLS)
    gates3 = gates.reshape(B, S, GATE_PAD)

    gbias = jnp.pad(jnp.concatenate([b_i[0], b_f[0]]), (0, GATE_PAD - 2 * M_HEADS)).reshape(1, GATE_PAD)
    hm = _mlstm(p3, og.reshape(B, S, M_WIDTH), gates3, m_norm_w, gbias)

    lam_p = jnp.stack([lambda_q1[0], lambda_k1[0], lambda_q2[0], lambda_k2[0]])
    ha = _attn(p3, rel_bias, lam_p, a_norm_w)

    out = _outproj(hm.reshape(B * S, M_WIDTH), ha.reshape(B * S, A_WIDTH), x2d, mod3,
                   w_out[0].astype(BF16), final_norm_w.reshape(1, D), B, S)
    return out.reshape(B, S, D)
```

```python
import functools
import math

import numpy as np
import jax
import jax.numpy as jnp
from jax import lax
from jax.experimental import pallas as pl
from jax.experimental.pallas import tpu as pltpu

F32 = jnp.float32
BF16 = jnp.bfloat16

D_MODEL = 1024
M_WIDTH = 1024
M_HEADS = 4
M_HD = 256
CONV_K = 4
A_WIDTH = 1024
A_HEADS = 8
A_VHD = 128
A_QKD = 64
N_BUCKETS = 32
MAX_DIST = 128
EPS = 1e-6
LAM_INIT = 0.8 - 0.6 * math.exp(-0.3 * 0)

N_MAIN = 5 * M_WIDTH + 4 * A_WIDTH
GATE_COL0 = 5 * M_WIDTH
GATE_PAD = 128

M_CHUNK = 256
M_AUG = M_HD + 128
ATT_BLK = 256
ATT_HB = 4
VT_ROWS = A_VHD + 16
LOG2E = math.log2(math.e)
NEG = -1e30

VMEM_LIMIT = 56 * 1024 * 1024


def _nt_dot(a, b):
    return lax.dot_general(a, b, (((1,), (1,)), ((), ())), preferred_element_type=F32)


def _tn_dot(a, b):
    return lax.dot_general(a, b, (((0,), (0,)), ((), ())), preferred_element_type=F32)


def _sigmoid(x):
    return 1.0 / (1.0 + jnp.exp(-x))


def _adaln_kernel(c_ref, w_ref, b_ref, o_ref):
    c = c_ref[...]
    sc = (c * _sigmoid(c)).astype(BF16)
    o_ref[0] = jnp.dot(sc, w_ref[...].astype(BF16), preferred_element_type=F32) + b_ref[0]


def _adaln(c, w_ada, b_ada):
    B, D = c.shape
    return pl.pallas_call(
        _adaln_kernel,
        grid=(3,),
        in_specs=[
            pl.BlockSpec((B, D), lambda j: (0, 0)),
            pl.BlockSpec((D, D), lambda j: (0, j)),
            pl.BlockSpec((1, 1, D), lambda j: (j, 0, 0)),
        ],
        out_specs=pl.BlockSpec((1, B, D), lambda j: (j, 0, 0)),
        out_shape=jax.ShapeDtypeStruct((3, B, D), F32),
        compiler_params=pltpu.CompilerParams(
            dimension_semantics=("arbitrary",), vmem_limit_bytes=VMEM_LIMIT),
        name="adaln",
    )(c, w_ada, b_ada.reshape(3, 1, D))


IN_TN = 1024
IN_TILES = 9
P_TILES = 7
P_COLS = P_TILES * IN_TN
IN_ORDER = (0, 1, 3, 4, 8, 2, 5, 6, 7)
IN_P_SLOT = {0: 0, 1: 1, 2: 2, 5: 3, 6: 4, 7: 5, 8: 6}


def _inproj_kernel(x_ref, shift_ref, scale_ref, nw_ref, w_ref, wg_ref, cw_ref,
                   p_ref, g_ref, gates_ref, h_scr, raw0, raw1, tail_scr, *, tiles_per_seq):
    i = pl.program_id(0)
    tm = x_ref.shape[0]
    tn = IN_TN
    raws = (raw0, raw1)
    col_scale = {0: M_HD ** -0.5, 5: A_QKD ** -0.5 * LOG2E}

    x = x_ref[...]
    y = x * lax.rsqrt(jnp.mean(x * x, axis=-1, keepdims=True) + EPS) * nw_ref[...]
    h = (y * (1.0 + scale_ref[0]) + shift_ref[0]).astype(BF16)
    h_scr[...] = h
    gates_ref[...] = jnp.dot(h, wg_ref[...], preferred_element_type=F32)

    @pl.when(i == 0)
    def _():
        tail_scr[...] = jnp.zeros_like(tail_scr)

    first = (i % tiles_per_seq) == 0

    def matmul(j, buf):
        raws[buf][8:tm + 8, :] = jnp.dot(h_scr[...], w_ref[:, j * tn:(j + 1) * tn],
                                         preferred_element_type=F32)

    def tail(j, buf):
        raw = raws[buf]
        if j in (3, 4):
            o, z = raw[8:tm + 8, 0:tn // 2], raw[8:tm + 8, tn // 2:tn]
            g_ref[:, (j - 3) * (tn // 2):(j - 2) * (tn // 2)] = (_sigmoid(o) * (z * _sigmoid(z))).astype(BF16)
            return
        if j in (0, 1):
            raw[0:8, :] = jnp.where(first, 0.0, tail_scr[j])
            y = raw[8:tm + 8, :] * cw_ref[j, CONV_K - 1:CONV_K, :]
            for jj in range(CONV_K - 1):
                off = 8 - (CONV_K - 1) + jj
                y = y + raw[off:off + tm, :] * cw_ref[j, jj:jj + 1, :]
            tail_scr[j] = raw[tm:tm + 8, :]
            y = y * _sigmoid(y)
        elif j == 8:
            a = raw[8:tm + 8, :]
            y = a * _sigmoid(a)
        else:
            y = raw[8:tm + 8, :]
        if j in col_scale:
            y = y * col_scale[j]
        slot = IN_P_SLOT[j]
        p_ref[:, slot * tn:(slot + 1) * tn] = y.astype(BF16)

    matmul(IN_ORDER[0], 0)
    for n in range(1, IN_TILES):
        tail(IN_ORDER[n - 1], (n - 1) % 2)
        matmul(IN_ORDER[n], n % 2)
    tail(IN_ORDER[-1], (IN_TILES - 1) % 2)


def _inproj(x2d, mod3, norm_w, w_main, w_gate, conv_w, B, S, tm=512):
    M, D = x2d.shape
    tn = IN_TN
    tiles_per_seq = S // tm
    return pl.pallas_call(
        functools.partial(_inproj_kernel, tiles_per_seq=tiles_per_seq),
        grid=(M // tm,),
        in_specs=[
            pl.BlockSpec((tm, D), lambda i: (i, 0)),
            pl.BlockSpec((1, 1, D), lambda i: (i // tiles_per_seq, 0, 0)),
            pl.BlockSpec((1, 1, D), lambda i: (B + i // tiles_per_seq, 0, 0)),
            pl.BlockSpec((1, D), lambda i: (0, 0)),
            pl.BlockSpec((D, IN_TILES * tn), lambda i: (0, 0), pipeline_mode=pl.Buffered(1)),
            pl.BlockSpec((D, GATE_PAD), lambda i: (0, 0)),
            pl.BlockSpec((2, CONV_K, tn), lambda i: (0, 0, 0)),
        ],
        out_specs=[
            pl.BlockSpec((tm, P_COLS), lambda i: (i, 0)),
            pl.BlockSpec((tm, M_WIDTH), lambda i: (i, 0)),
            pl.BlockSpec((tm, GATE_PAD), lambda i: (i, 0)),
        ],
        out_shape=[
            jax.ShapeDtypeStruct((M, P_COLS), BF16),
            jax.ShapeDtypeStruct((M, M_WIDTH), BF16),
            jax.ShapeDtypeStruct((M, GATE_PAD), F32),
        ],
        scratch_shapes=[
            pltpu.VMEM((tm, D), BF16),
            pltpu.VMEM((tm + 8, tn), F32),
            pltpu.VMEM((tm + 8, tn), F32),
            pltpu.VMEM((2, 8, tn), F32),
        ],
        compiler_params=pltpu.CompilerParams(
            dimension_semantics=("arbitrary",), vmem_limit_bytes=VMEM_LIMIT),
        name="inproj",
    )(x2d, mod3, mod3, norm_w, w_main, w_gate, conv_w)


def _mlstm_kernel(q_ref, k_ref, v_ref, og_ref, g_ref, nw_ref, gb_ref, o_ref, ct_scr, m_scr):
    L = M_CHUNK
    c = pl.program_id(1)

    @pl.when(c == 0)
    def _():
        ct_scr[...] = jnp.zeros_like(ct_scr)
        m_scr[...] = jnp.zeros_like(m_scr)

    g = g_ref[0] + gb_ref[...]
    lf = jnp.minimum(g, 0.0) - jnp.log1p(jnp.exp(-jnp.abs(g)))
    row = lax.broadcasted_iota(jnp.int32, (L, GATE_PAD), 0)
    bc = lf
    sh = 1
    while sh < L:
        bc = bc + jnp.where(row >= sh, pltpu.roll(bc, sh, 0), 0.0)
        sh *= 2
    bsh = pltpu.roll(bc, GATE_PAD - M_HEADS, 1)
    a = g - bsh
    cm = a
    sh = 1
    while sh < L:
        cm = jnp.maximum(cm, jnp.where(row >= sh, pltpu.roll(cm, sh, 0), -jnp.inf))
        sh *= 2
    m_prev = m_scr[...]
    mm = jnp.maximum(cm, m_prev)
    iw = jnp.exp(m_prev - mm)
    eneg = jnp.exp(-(bsh + mm))
    a_t = a.T

    tri = (lax.broadcasted_iota(jnp.int32, (L, L), 1) <= lax.broadcasted_iota(jnp.int32, (L, L), 0))

    ones_col = (lax.broadcasted_iota(jnp.int32, (L, M_AUG - M_HD), 1) == 0).astype(BF16)

    hs = range(M_HEADS)
    cols = [slice(h * M_HD, (h + 1) * M_HD) for h in hs]
    qs = [q_ref[0, :, cols[h]] for h in hs]
    ks = [k_ref[0, :, cols[h]] for h in hs]
    v_aug = [jnp.concatenate([v_ref[0, :, cols[h]], ones_col], axis=1) for h in hs]
    cts = [ct_scr[h] for h in hs]

    s_raw = [_nt_dot(qs[h], ks[h]) for h in hs]
    r_state = [jnp.dot(qs[h], cts[h].astype(BF16), preferred_element_type=F32) for h in hs]
    for h in hs:
        m_last = mm[L - 1:L, h:h + 1]
        w_col = jnp.exp(a[:, h:h + 1] - m_last)
        decay = iw[L - 1:L, h:h + 1]
        wv = (w_col * v_aug[h].astype(F32)).astype(BF16)
        ct_scr[h] = decay * cts[h] + _tn_dot(ks[h], wv)

    s16 = []
    for h in hs:
        dw = jnp.where(tri, jnp.exp(a_t[h:h + 1, :] - mm[:, h:h + 1]), 0.0)
        s16.append((s_raw[h] * dw).astype(BF16))
    rs = [iw[:, h:h + 1] * r_state[h] + jnp.dot(s16[h], v_aug[h], preferred_element_type=F32) for h in hs]

    for h in hs:
        num = rs[h][:, 0:M_HD]
        d = jnp.maximum(jnp.abs(rs[h][:, M_HD:M_HD + 1]), eneg[:, h:h + 1])
        hc = num - jnp.mean(num, axis=-1, keepdims=True)
        var = jnp.mean(hc * hc, axis=-1, keepdims=True)
        hn = hc * lax.rsqrt(var + EPS * d * d) * nw_ref[:, cols[h]]
        o_ref[0, :, cols[h]] = (hn * og_ref[0, :, cols[h]].astype(F32)).astype(BF16)

    m_scr[...] = bsh[L - 1:L, :] + mm[L - 1:L, :]


def _mlstm(p3, og3, gates3, m_norm_w, gbias):
    B, S, _ = p3.shape
    L = M_CHUNK
    W = M_WIDTH

    def col(cb):
        return pl.BlockSpec((1, L, W), lambda b, c, cb=cb: (b, c, cb))

    return pl.pallas_call(
        _mlstm_kernel,
        grid=(B, S // L),
        in_specs=[
            col(0), col(1), col(2),
            pl.BlockSpec((1, L, W), lambda b, c: (b, c, 0)),
            pl.BlockSpec((1, L, GATE_PAD), lambda b, c: (b, c, 0)),
            pl.BlockSpec((1, W), lambda b, c: (0, 0)),
            pl.BlockSpec((1, GATE_PAD), lambda b, c: (0, 0)),
        ],
        out_specs=pl.BlockSpec((1, L, W), lambda b, c: (b, c, 0)),
        out_shape=jax.ShapeDtypeStruct((B, S, W), BF16),
        scratch_shapes=[
            pltpu.VMEM((M_HEADS, M_HD, M_AUG), F32),
            pltpu.VMEM((1, GATE_PAD), F32),
        ],
        compiler_params=pltpu.CompilerParams(
            dimension_semantics=("arbitrary", "arbitrary"), vmem_limit_bytes=VMEM_LIMIT),
        name="mlstm",
    )(p3, p3, p3, og3, gates3, m_norm_w, gbias)


def _bucket_tiles():
    T = ATT_BLK
    kk = np.arange(T)[:, None]
    qq = np.arange(T)[None, :]
    out = []
    for base in (0, T):
        n = base + qq - kk
        max_exact = N_BUCKETS // 2
        nf = np.maximum(n, 1).astype(np.float64)
        large = max_exact + (np.log(nf / max_exact) / math.log(MAX_DIST / max_exact)
                             * (N_BUCKETS - max_exact)).astype(np.int64)
        large = np.minimum(large, N_BUCKETS - 1)
        bk = np.where(n < max_exact, n, large)
        bk = np.where(n < 0, -1, bk)
        out.append(bk)
    return np.stack(out).astype(np.int32)


def _attn_kernel(relb_ref, bucket_ref, lam_ref, anw_ref, q_ref, k_ref, v_ref, z_ref, o_ref,
                 bias_scr, vt_scr, qq_scr, s0_scr, s1_scr, acc_scr):
    T = ATT_BLK
    HB = ATT_HB
    S = k_ref.shape[1]
    NQ = S // T
    hg = pl.program_id(0)
    b = pl.program_id(1)
    heads = [slice(hb * A_VHD, (hb + 1) * A_VHD) for hb in range(HB)]

    @pl.when(b == 0)
    def _():
        for hb in range(HB):
            h = hg * HB + hb
            far = relb_ref[N_BUCKETS - 1, h]
            for t in range(2):
                bk = bucket_ref[t]
                tile = jnp.zeros((T, T), F32)
                for bb in range(N_BUCKETS - 1):
                    tile = jnp.where(bk == bb, (relb_ref[bb, h] - far) * LOG2E, tile)
                tile = jnp.where(bk < 0, NEG, tile)
                bias_scr[hb, t] = jnp.concatenate([tile, tile], axis=1)

    lane = lax.broadcasted_iota(jnp.int32, (T, A_VHD), 1)
    ones_rows = (lax.broadcasted_iota(jnp.int32, (VT_ROWS - A_VHD, T), 0) == 0).astype(BF16)
    for hb in range(HB):
        for ci in range(NQ):
            rows = slice(ci * T, (ci + 1) * T)
            vt_scr[hb, 0:A_VHD, rows] = v_ref[0, rows, heads[hb]].astype(F32).T.astype(BF16)
            vt_scr[hb, A_VHD:VT_ROWS, rows] = ones_rows
            qs = q_ref[0, rows, heads[hb]]
            zero = jnp.zeros_like(qs)
            qq_scr[hb, ci, 0:T, :] = jnp.where(lane < A_QKD, qs, zero)
            qq_scr[hb, ci, T:2 * T, :] = jnp.where(lane >= A_QKD, qs, zero)
    acc_scr[...] = jnp.zeros_like(acc_scr)

    s_bufs = (s0_scr, s1_scr)

    def issue(buf, qb, kj, hb):
        start = pl.multiple_of(kj * T, T)
        s_bufs[buf][hb] = _nt_dot(k_ref[0, pl.ds(start, T), heads[hb]], qq_scr[hb, qb])

    def run_stage(cur, qb, kj_next, kj, bias_idx, ms):
        start = pl.multiple_of(kj * T, T)
        out = []
        for hb in range(HB):
            s = s_bufs[cur][hb]
            if bias_idx is not None:
                s = s + bias_scr[hb, bias_idx]
            m_old = ms[hb]
            m_new = jnp.maximum(m_old, jnp.max(s, axis=0, keepdims=True))
            alpha = jnp.exp2(m_old - m_new)
            p = jnp.exp2(s - m_new).astype(BF16)
            issue(1 - cur, qb, kj_next, hb)
            pv = jnp.dot(vt_scr[hb, :, pl.ds(start, T)], p, preferred_element_type=F32)
            acc_scr[hb] = alpha * acc_scr[hb] + pv
            out.append(m_new)
        return tuple(out)

    lam_p = lam_ref[...]
    lam = (jnp.exp(jnp.sum(lam_p[0:1] * lam_p[1:2], axis=-1, keepdims=True))
           - jnp.exp(jnp.sum(lam_p[2:3] * lam_p[3:4], axis=-1, keepdims=True)) + LAM_INIT)

    def finalize(qi):
        rows = pl.ds(pl.multiple_of(qi * T, T), T)
        for hb in range(HB):
            acc = acc_scr[hb]
            o = acc[0:A_VHD] / acc[A_VHD:A_VHD + 1]
            d = (o[:, 0:T] - lam * o[:, T:2 * T]).T
            y = d * lax.rsqrt(jnp.mean(d * d, axis=-1, keepdims=True) + EPS) * anw_ref[...] * (1.0 - LAM_INIT)
            o_ref[0, rows, heads[hb]] = (y * z_ref[0, rows, heads[hb]].astype(F32)).astype(BF16)

    m_init = tuple(jnp.full((1, 2 * T), NEG, F32) for _ in range(HB))

    def stage(par, qb, kj_next, kj, bias_idx, ms):
        def run(cur, ms):
            return run_stage(cur, qb, kj_next, kj, bias_idx, ms)

        return lax.cond(par == 0, functools.partial(run, 0), functools.partial(run, 1), ms)

    for hb in range(HB):
        issue(0, 0, 0, hb)
    stage(0, 1, 0, 0, 0, m_init)
    finalize(0)

    def qblock(qi, carry):
        t0 = lax.shift_right_logical(qi * (qi + 1), 1)
        ms = lax.fori_loop(0, qi - 1, lambda j, ms: stage((t0 + j) & 1, qi, j + 1, j, None, ms), m_init)
        ms = stage((t0 + qi - 1) & 1, qi, qi, qi - 1, 1, ms)
        stage((t0 + qi) & 1, jnp.minimum(qi + 1, NQ - 1), 0, qi, 0, ms)
        finalize(qi)
        return carry

    lax.fori_loop(1, NQ, qblock, 0)


def _attn(proj3, rel_bias, lam_p, a_norm_w):
    B, S, _ = proj3.shape
    T = ATT_BLK
    HB = ATT_HB
    W = HB * A_VHD
    q0 = 3 * IN_TN // W
    k0 = q0 + A_HEADS // HB
    v0 = k0 + A_HEADS // HB
    z0 = v0 + A_HEADS // HB
    buckets = jnp.asarray(_bucket_tiles())
    return pl.pallas_call(
        _attn_kernel,
        grid=(A_HEADS // HB, B),
        in_specs=[
            pl.BlockSpec(memory_space=pltpu.SMEM),
            pl.BlockSpec((2, T, T), lambda h, b: (0, 0, 0)),
            pl.BlockSpec((4, A_QKD), lambda h, b: (0, 0)),
            pl.BlockSpec((1, A_VHD), lambda h, b: (0, 0)),
            pl.BlockSpec((1, S, W), lambda h, b: (b, 0, q0 + h)),
            pl.BlockSpec((1, S, W), lambda h, b: (b, 0, k0 + h)),
            pl.BlockSpec((1, S, W), lambda h, b: (b, 0, v0 + h)),
            pl.BlockSpec((1, S, W), lambda h, b: (b, 0, z0 + h)),
        ],
        out_specs=pl.BlockSpec((1, S, W), lambda h, b: (b, 0, h)),
        out_shape=jax.ShapeDtypeStruct((B, S, A_WIDTH), BF16),
        scratch_shapes=[
            pltpu.VMEM((HB, 2, T, 2 * T), F32),
            pltpu.VMEM((HB, VT_ROWS, S), BF16),
            pltpu.VMEM((HB, S // T, 2 * T, A_VHD), BF16),
            pltpu.VMEM((HB, T, 2 * T), F32),
            pltpu.VMEM((HB, T, 2 * T), F32),
            pltpu.VMEM((HB, VT_ROWS, 2 * T), F32),
        ],
        compiler_params=pltpu.CompilerParams(
            dimension_semantics=("arbitrary", "arbitrary"), vmem_limit_bytes=VMEM_LIMIT),
        name="attn",
    )(rel_bias, buckets, lam_p, a_norm_w, proj3, proj3, proj3, proj3)


def _outproj_kernel(hm_ref, ha_ref, x_ref, gate_ref, w_ref, fw_ref, o_ref):
    y = jnp.dot(hm_ref[...], w_ref[0:M_WIDTH, :], preferred_element_type=F32)
    y = y + jnp.dot(ha_ref[...], w_ref[M_WIDTH:, :], preferred_element_type=F32)
    r = x_ref[...] + gate_ref[0] * y
    o_ref[...] = r * lax.rsqrt(jnp.mean(r * r, axis=-1, keepdims=True) + EPS) * fw_ref[...]


def _outproj(hm2d, ha2d, x2d, mod3, w_out, final_w, B, S, tm=512):
    M, D = x2d.shape
    tiles_per_seq = S // tm
    return pl.pallas_call(
        _outproj_kernel,
        grid=(M // tm,),
        in_specs=[
            pl.BlockSpec((tm, M_WIDTH), lambda i: (i, 0)),
            pl.BlockSpec((tm, A_WIDTH), lambda i: (i, 0)),
            pl.BlockSpec((tm, D), lambda i: (i, 0)),
            pl.BlockSpec((1, 1, D), lambda i: (2 * B + i // tiles_per_seq, 0, 0)),
            pl.BlockSpec((M_WIDTH + A_WIDTH, D), lambda i: (0, 0)),
            pl.BlockSpec((1, D), lambda i: (0, 0)),
        ],
        out_specs=pl.BlockSpec((tm, D), lambda i: (i, 0)),
        out_shape=jax.ShapeDtypeStruct((M, D), F32),
        compiler_params=pltpu.CompilerParams(
            dimension_semantics=("arbitrary",), vmem_limit_bytes=VMEM_LIMIT),
        name="outproj",
    )(hm2d, ha2d, x2d, mod3, w_out, final_w)


def kernel(x, c, norm_w, w_ada, b_ada, w_in, b_i, b_f, conv_q_w, conv_k_w, m_norm_w,
           lambda_q1, lambda_k1, lambda_q2, lambda_k2, a_norm_w, rel_bias, w_out, final_norm_w):
    B, S, D = x.shape
    assert norm_w.shape[0] == 1, "single layer only"
    x2d = x.reshape(B * S, D)

    mod = _adaln(c, w_ada[0], b_ada[0])
    mod3 = mod.reshape(3 * B, 1, D)

    w_in0 = w_in[0]
    half = M_WIDTH // 2
    mo0, mz0 = 3 * M_WIDTH, 4 * M_WIDTH
    a0 = GATE_COL0 + 2 * M_HEADS
    w_main = jnp.concatenate([
        w_in0[:, :3 * M_WIDTH],
        w_in0[:, mo0:mo0 + half], w_in0[:, mz0:mz0 + half],
        w_in0[:, mo0 + half:mo0 + M_WIDTH], w_in0[:, mz0 + half:mz0 + M_WIDTH],
        w_in0[:, a0:]], axis=1).astype(BF16)
    w_gate = jnp.pad(w_in0[:, GATE_COL0:a0], ((0, 0), (0, GATE_PAD - 2 * M_HEADS))).astype(BF16)
    conv_w = jnp.stack([conv_q_w[0], conv_k_w[0]])
    p, og, gates = _inproj(x2d, mod3, norm_w, w_main, w_gate, conv_w, B, S)
    p3 = p.reshape(B, S, P_COLS)
    gates3 = gates.reshape(B, S, GATE_PAD)

    gbias = jnp.pad(jnp.concatenate([b_i[0], b_f[0]]), (0, GATE_PAD - 2 * M_HEADS)).reshape(1, GATE_PAD)
    hm = _mlstm(p3, og.reshape(B, S, M_WIDTH), gates3, m_norm_w, gbias)

    lam_p = jnp.stack([lambda_q1[0], lambda_k1[0], lambda_q2[0], lambda_k2[0]])
    ha = _attn(p3, rel_bias, lam_p, a_norm_w)

    out = _outproj(hm.reshape(B * S, M_WIDTH), ha.reshape(B * S, A_WIDTH), x2d, mod3,
                   w_out[0].astype(BF16), final_norm_w.reshape(1, D), B, S)
    return out.reshape(B, S, D)
```

```python
import functools
import math

import numpy as np
import jax
import jax.numpy as jnp
from jax import lax
from jax.experimental import pallas as pl
from jax.experimental.pallas import tpu as pltpu

F32 = jnp.float32
BF16 = jnp.bfloat16

D_MODEL = 1024
M_WIDTH = 1024
M_HEADS = 4
M_HD = 256
CONV_K = 4
A_WIDTH = 1024
A_HEADS = 8
A_VHD = 128
A_QKD = 64
N_BUCKETS = 32
MAX_DIST = 128
EPS = 1e-6
LAM_INIT = 0.8 - 0.6 * math.exp(-0.3 * 0)

N_MAIN = 5 * M_WIDTH + 4 * A_WIDTH
GATE_COL0 = 5 * M_WIDTH
GATE_PAD = 128

M_CHUNK = 256
M_AUG = M_HD + 128
ATT_BLK = 256
ATT_HB = 2
VT_ROWS = A_VHD + 16
LOG2E = math.log2(math.e)
NEG = -1e30

VMEM_LIMIT = 56 * 1024 * 1024


def _nt_dot(a, b):
    return lax.dot_general(a, b, (((1,), (1,)), ((), ())), preferred_element_type=F32)


def _tn_dot(a, b):
    return lax.dot_general(a, b, (((0,), (0,)), ((), ())), preferred_element_type=F32)


def _sigmoid(x):
    return 1.0 / (1.0 + jnp.exp(-x))


def _adaln_kernel(c_ref, w_ref, b_ref, o_ref):
    c = c_ref[...]
    sc = (c * _sigmoid(c)).astype(BF16)
    o_ref[0] = jnp.dot(sc, w_ref[...].astype(BF16), preferred_element_type=F32) + b_ref[0]


def _adaln(c, w_ada, b_ada):
    B, D = c.shape
    return pl.pallas_call(
        _adaln_kernel,
        grid=(3,),
        in_specs=[
            pl.BlockSpec((B, D), lambda j: (0, 0)),
            pl.BlockSpec((D, D), lambda j: (0, j)),
            pl.BlockSpec((1, 1, D), lambda j: (j, 0, 0)),
        ],
        out_specs=pl.BlockSpec((1, B, D), lambda j: (j, 0, 0)),
        out_shape=jax.ShapeDtypeStruct((3, B, D), F32),
        compiler_params=pltpu.CompilerParams(
            dimension_semantics=("arbitrary",), vmem_limit_bytes=VMEM_LIMIT),
        name="adaln",
    )(c, w_ada, b_ada.reshape(3, 1, D))


IN_TN = 1024
IN_TILES = 9
P_TILES = 7
P_COLS = P_TILES * IN_TN
IN_ORDER = (0, 1, 3, 4, 8, 2, 5, 6, 7)
IN_P_SLOT = {0: 0, 1: 1, 2: 2, 5: 3, 6: 4, 7: 5, 8: 6}


def _inproj_kernel(x_ref, shift_ref, scale_ref, nw_ref, w_ref, wg_ref, cw_ref,
                   p_ref, g_ref, gates_ref, h_scr, raw0, raw1, tail_scr, *, tiles_per_seq):
    i = pl.program_id(0)
    tm = x_ref.shape[0]
    tn = IN_TN
    raws = (raw0, raw1)
    col_scale = {0: M_HD ** -0.5, 5: A_QKD ** -0.5 * LOG2E}

    x = x_ref[...]
    y = x * lax.rsqrt(jnp.mean(x * x, axis=-1, keepdims=True) + EPS) * nw_ref[...]
    h = (y * (1.0 + scale_ref[0]) + shift_ref[0]).astype(BF16)
    h_scr[...] = h
    gates_ref[...] = jnp.dot(h, wg_ref[...], preferred_element_type=F32)

    @pl.when(i == 0)
    def _():
        tail_scr[...] = jnp.zeros_like(tail_scr)

    first = (i % tiles_per_seq) == 0

    def matmul(j, buf):
        raws[buf][8:tm + 8, :] = jnp.dot(h_scr[...], w_ref[:, j * tn:(j + 1) * tn],
                                         preferred_element_type=F32)

    def tail(j, buf):
        raw = raws[buf]
        if j in (3, 4):
            o, z = raw[8:tm + 8, 0:tn // 2], raw[8:tm + 8, tn // 2:tn]
            g_ref[:, (j - 3) * (tn // 2):(j - 2) * (tn // 2)] = (_sigmoid(o) * (z * _sigmoid(z))).astype(BF16)
            return
        if j in (0, 1):
            raw[0:8, :] = jnp.where(first, 0.0, tail_scr[j])
            y = raw[8:tm + 8, :] * cw_ref[j, CONV_K - 1:CONV_K, :]
            for jj in range(CONV_K - 1):
                off = 8 - (CONV_K - 1) + jj
                y = y + raw[off:off + tm, :] * cw_ref[j, jj:jj + 1, :]
            tail_scr[j] = raw[tm:tm + 8, :]
            y = y * _sigmoid(y)
        elif j == 8:
            a = raw[8:tm + 8, :]
            y = a * _sigmoid(a)
        else:
            y = raw[8:tm + 8, :]
        if j in col_scale:
            y = y * col_scale[j]
        slot = IN_P_SLOT[j]
        p_ref[:, slot * tn:(slot + 1) * tn] = y.astype(BF16)

    matmul(IN_ORDER[0], 0)
    for n in range(1, IN_TILES):
        tail(IN_ORDER[n - 1], (n - 1) % 2)
        matmul(IN_ORDER[n], n % 2)
    tail(IN_ORDER[-1], (IN_TILES - 1) % 2)


def _inproj(x2d, mod3, norm_w, w_main, w_gate, conv_w, B, S, tm=512):
    M, D = x2d.shape
    tn = IN_TN
    tiles_per_seq = S // tm
    return pl.pallas_call(
        functools.partial(_inproj_kernel, tiles_per_seq=tiles_per_seq),
        grid=(M // tm,),
        in_specs=[
            pl.BlockSpec((tm, D), lambda i: (i, 0)),
            pl.BlockSpec((1, 1, D), lambda i: (i // tiles_per_seq, 0, 0)),
            pl.BlockSpec((1, 1, D), lambda i: (B + i // tiles_per_seq, 0, 0)),
            pl.BlockSpec((1, D), lambda i: (0, 0)),
            pl.BlockSpec((D, IN_TILES * tn), lambda i: (0, 0), pipeline_mode=pl.Buffered(1)),
            pl.BlockSpec((D, GATE_PAD), lambda i: (0, 0)),
            pl.BlockSpec((2, CONV_K, tn), lambda i: (0, 0, 0)),
        ],
        out_specs=[
            pl.BlockSpec((tm, P_COLS), lambda i: (i, 0)),
            pl.BlockSpec((tm, M_WIDTH), lambda i: (i, 0)),
            pl.BlockSpec((tm, GATE_PAD), lambda i: (i, 0)),
        ],
        out_shape=[
            jax.ShapeDtypeStruct((M, P_COLS), BF16),
            jax.ShapeDtypeStruct((M, M_WIDTH), BF16),
            jax.ShapeDtypeStruct((M, GATE_PAD), F32),
        ],
        scratch_shapes=[
            pltpu.VMEM((tm, D), BF16),
            pltpu.VMEM((tm + 8, tn), F32),
            pltpu.VMEM((tm + 8, tn), F32),
            pltpu.VMEM((2, 8, tn), F32),
        ],
        compiler_params=pltpu.CompilerParams(
            dimension_semantics=("arbitrary",), vmem_limit_bytes=VMEM_LIMIT),
        name="inproj",
    )(x2d, mod3, mod3, norm_w, w_main, w_gate, conv_w)


def _mlstm_kernel(q_ref, k_ref, v_ref, og_ref, g_ref, nw_ref, gb_ref, o_ref, ct_scr, m_scr):
    L = M_CHUNK
    c = pl.program_id(1)

    @pl.when(c == 0)
    def _():
        ct_scr[...] = jnp.zeros_like(ct_scr)
        m_scr[...] = jnp.zeros_like(m_scr)

    g = g_ref[0] + gb_ref[...]
    lf = jnp.minimum(g, 0.0) - jnp.log1p(jnp.exp(-jnp.abs(g)))
    row = lax.broadcasted_iota(jnp.int32, (L, GATE_PAD), 0)
    bc = lf
    sh = 1
    while sh < L:
        bc = bc + jnp.where(row >= sh, pltpu.roll(bc, sh, 0), 0.0)
        sh *= 2
    bsh = pltpu.roll(bc, GATE_PAD - M_HEADS, 1)
    a = g - bsh
    cm = a
    sh = 1
    while sh < L:
        cm = jnp.maximum(cm, jnp.where(row >= sh, pltpu.roll(cm, sh, 0), -jnp.inf))
        sh *= 2
    m_prev = m_scr[...]
    mm = jnp.maximum(cm, m_prev)
    iw = jnp.exp(m_prev - mm)
    eneg = jnp.exp(-(bsh + mm))
    a_t = a.T

    tri = (lax.broadcasted_iota(jnp.int32, (L, L), 1) <= lax.broadcasted_iota(jnp.int32, (L, L), 0))

    ones_col = (lax.broadcasted_iota(jnp.int32, (L, M_AUG - M_HD), 1) == 0).astype(BF16)

    hs = range(M_HEADS)
    cols = [slice(h * M_HD, (h + 1) * M_HD) for h in hs]
    qs = [q_ref[0, :, cols[h]] for h in hs]
    ks = [k_ref[0, :, cols[h]] for h in hs]
    v_aug = [jnp.concatenate([v_ref[0, :, cols[h]], ones_col], axis=1) for h in hs]
    cts = [ct_scr[h] for h in hs]

    s_raw = [_nt_dot(qs[h], ks[h]) for h in hs]
    r_state = [jnp.dot(qs[h], cts[h].astype(BF16), preferred_element_type=F32) for h in hs]
    for h in hs:
        m_last = mm[L - 1:L, h:h + 1]
        w_col = jnp.exp(a[:, h:h + 1] - m_last)
        decay = iw[L - 1:L, h:h + 1]
        wv = (w_col * v_aug[h].astype(F32)).astype(BF16)
        ct_scr[h] = decay * cts[h] + _tn_dot(ks[h], wv)

    s16 = []
    for h in hs:
        dw = jnp.where(tri, jnp.exp(a_t[h:h + 1, :] - mm[:, h:h + 1]), 0.0)
        s16.append((s_raw[h] * dw).astype(BF16))
    rs = [iw[:, h:h + 1] * r_state[h] + jnp.dot(s16[h], v_aug[h], preferred_element_type=F32) for h in hs]

    for h in hs:
        num = rs[h][:, 0:M_HD]
        d = jnp.maximum(jnp.abs(rs[h][:, M_HD:M_HD + 1]), eneg[:, h:h + 1])
        hc = num - jnp.mean(num, axis=-1, keepdims=True)
        var = jnp.mean(hc * hc, axis=-1, keepdims=True)
        hn = hc * lax.rsqrt(var + EPS * d * d) * nw_ref[:, cols[h]]
        o_ref[0, :, cols[h]] = (hn * og_ref[0, :, cols[h]].astype(F32)).astype(BF16)

    m_scr[...] = bsh[L - 1:L, :] + mm[L - 1:L, :]


def _mlstm(p3, og3, gates3, m_norm_w, gbias):
    B, S, _ = p3.shape
    L = M_CHUNK
    W = M_WIDTH

    def col(cb):
        return pl.BlockSpec((1, L, W), lambda b, c, cb=cb: (b, c, cb))

    return pl.pallas_call(
        _mlstm_kernel,
        grid=(B, S // L),
        in_specs=[
            col(0), col(1), col(2),
            pl.BlockSpec((1, L, W), lambda b, c: (b, c, 0)),
            pl.BlockSpec((1, L, GATE_PAD), lambda b, c: (b, c, 0)),
            pl.BlockSpec((1, W), lambda b, c: (0, 0)),
            pl.BlockSpec((1, GATE_PAD), lambda b, c: (0, 0)),
        ],
        out_specs=pl.BlockSpec((1, L, W), lambda b, c: (b, c, 0)),
        out_shape=jax.ShapeDtypeStruct((B, S, W), BF16),
        scratch_shapes=[
            pltpu.VMEM((M_HEADS, M_HD, M_AUG), F32),
            pltpu.VMEM((1, GATE_PAD), F32),
        ],
        compiler_params=pltpu.CompilerParams(
            dimension_semantics=("arbitrary", "arbitrary"), vmem_limit_bytes=VMEM_LIMIT),
        name="mlstm",
    )(p3, p3, p3, og3, gates3, m_norm_w, gbias)


def _bucket_tiles():
    T = ATT_BLK
    kk = np.arange(T)[:, None]
    qq = np.arange(T)[None, :]
    out = []
    for base in (0, T):
        n = base + qq - kk
        max_exact = N_BUCKETS // 2
        nf = np.maximum(n, 1).astype(np.float64)
        large = max_exact + (np.log(nf / max_exact) / math.log(MAX_DIST / max_exact)
                             * (N_BUCKETS - max_exact)).astype(np.int64)
        large = np.minimum(large, N_BUCKETS - 1)
        bk = np.where(n < max_exact, n, large)
        bk = np.where(n < 0, -1, bk)
        out.append(bk)
    return np.stack(out).astype(np.int32)


def _attn_kernel(relb_ref, bucket_ref, lam_ref, anw_ref, q_ref, k_ref, v_ref, z_ref, o_ref,
                 bias_scr, vt_scr, qq_scr, s0_scr, s1_scr, acc_scr):
    T = ATT_BLK
    HB = ATT_HB
    S = k_ref.shape[1]
    NQ = S // T
    hg = pl.program_id(0)
    b = pl.program_id(1)
    heads = [slice(hb * A_VHD, (hb + 1) * A_VHD) for hb in range(HB)]

    @pl.when(b == 0)
    def _():
        for hb in range(HB):
            h = hg * HB + hb
            far = relb_ref[N_BUCKETS - 1, h]
            for t in range(2):
                bk = bucket_ref[t]
                tile = jnp.zeros((T, T), F32)
                for bb in range(N_BUCKETS - 1):
                    tile = jnp.where(bk == bb, (relb_ref[bb, h] - far) * LOG2E, tile)
                tile = jnp.where(bk < 0, NEG, tile)
                bias_scr[hb, t] = jnp.concatenate([tile, tile], axis=1)

    lane = lax.broadcasted_iota(jnp.int32, (T, A_VHD), 1)
    ones_rows = (lax.broadcasted_iota(jnp.int32, (VT_ROWS - A_VHD, T), 0) == 0).astype(BF16)
    for hb in range(HB):
        for ci in range(NQ):
            rows = slice(ci * T, (ci + 1) * T)
            vt_scr[hb, 0:A_VHD, rows] = v_ref[0, rows, heads[hb]].astype(F32).T.astype(BF16)
            vt_scr[hb, A_VHD:VT_ROWS, rows] = ones_rows
            qs = q_ref[0, rows, heads[hb]]
            zero = jnp.zeros_like(qs)
            qq_scr[hb, ci, 0:T, :] = jnp.where(lane < A_QKD, qs, zero)
            qq_scr[hb, ci, T:2 * T, :] = jnp.where(lane >= A_QKD, qs, zero)
    acc_scr[...] = jnp.zeros_like(acc_scr)

    s_bufs = (s0_scr, s1_scr)

    def issue(buf, qb, kj, hb):
        s_bufs[buf][hb] = _nt_dot(k_ref[0, kj * T:(kj + 1) * T, heads[hb]], qq_scr[hb, qb])

    def run_stage(cur, nxt, kj, bias_idx, ms):
        start = kj * T
        out = []
        for hb in range(HB):
            s = s_bufs[cur][hb]
            if bias_idx is not None:
                s = s + bias_scr[hb, bias_idx]
            m_old = ms[hb]
            m_new = jnp.maximum(m_old, jnp.max(s, axis=0, keepdims=True))
            alpha = jnp.exp2(m_old - m_new)
            p = jnp.exp2(s - m_new).astype(BF16)
            if nxt is not None:
                issue(1 - cur, nxt[0], nxt[1], hb)
            pv = jnp.dot(vt_scr[hb, :, start:start + T], p, preferred_element_type=F32)
            acc_scr[hb] = alpha * acc_scr[hb] + pv
            out.append(m_new)
        return tuple(out)

    lam_p = lam_ref[...]
    lam = (jnp.exp(jnp.sum(lam_p[0:1] * lam_p[1:2], axis=-1, keepdims=True))
           - jnp.exp(jnp.sum(lam_p[2:3] * lam_p[3:4], axis=-1, keepdims=True)) + LAM_INIT)

    def finalize(qi):
        rows = pl.ds(pl.multiple_of(qi * T, T), T)
        for hb in range(HB):
            acc = acc_scr[hb]
            o = acc[0:A_VHD] / acc[A_VHD:A_VHD + 1]
            d = (o[:, 0:T] - lam * o[:, T:2 * T]).T
            y = d * lax.rsqrt(jnp.mean(d * d, axis=-1, keepdims=True) + EPS) * anw_ref[...] * (1.0 - LAM_INIT)
            o_ref[0, rows, heads[hb]] = (y * z_ref[0, rows, heads[hb]].astype(F32)).astype(BF16)

    m_init = tuple(jnp.full((1, 2 * T), NEG, F32) for _ in range(HB))

    pairs = [(qi, kj) for qi in range(NQ) for kj in range(qi + 1)]
    for hb in range(HB):
        issue(0, 0, 0, hb)
    ms = m_init
    for t, (qi, kj) in enumerate(pairs):
        nxt = pairs[t + 1] if t + 1 < len(pairs) else None
        bias_idx = 0 if kj == qi else (1 if kj == qi - 1 else None)
        ms = run_stage(t % 2, nxt, kj, bias_idx, ms)
        if kj == qi:
            finalize(qi)
            ms = m_init


def _attn(proj3, rel_bias, lam_p, a_norm_w):
    B, S, _ = proj3.shape
    T = ATT_BLK
    HB = ATT_HB
    W = HB * A_VHD
    q0 = 3 * IN_TN // W
    k0 = q0 + A_HEADS // HB
    v0 = k0 + A_HEADS // HB
    z0 = v0 + A_HEADS // HB
    buckets = jnp.asarray(_bucket_tiles())
    return pl.pallas_call(
        _attn_kernel,
        grid=(A_HEADS // HB, B),
        in_specs=[
            pl.BlockSpec(memory_space=pltpu.SMEM),
            pl.BlockSpec((2, T, T), lambda h, b: (0, 0, 0)),
            pl.BlockSpec((4, A_QKD), lambda h, b: (0, 0)),
            pl.BlockSpec((1, A_VHD), lambda h, b: (0, 0)),
            pl.BlockSpec((1, S, W), lambda h, b: (b, 0, q0 + h)),
            pl.BlockSpec((1, S, W), lambda h, b: (b, 0, k0 + h)),
            pl.BlockSpec((1, S, W), lambda h, b: (b, 0, v0 + h)),
            pl.BlockSpec((1, S, W), lambda h, b: (b, 0, z0 + h)),
        ],
        out_specs=pl.BlockSpec((1, S, W), lambda h, b: (b, 0, h)),
        out_shape=jax.ShapeDtypeStruct((B, S, A_WIDTH), BF16),
        scratch_shapes=[
            pltpu.VMEM((HB, 2, T, 2 * T), F32),
            pltpu.VMEM((HB, VT_ROWS, S), BF16),
            pltpu.VMEM((HB, S // T, 2 * T, A_VHD), BF16),
            pltpu.VMEM((HB, T, 2 * T), F32),
            pltpu.VMEM((HB, T, 2 * T), F32),
            pltpu.VMEM((HB, VT_ROWS, 2 * T), F32),
        ],
        compiler_params=pltpu.CompilerParams(
            dimension_semantics=("arbitrary", "arbitrary"), vmem_limit_bytes=VMEM_LIMIT),
        name="attn",
    )(rel_bias, buckets, lam_p, a_norm_w, proj3, proj3, proj3, proj3)


def _outproj_kernel(hm_ref, ha_ref, x_ref, gate_ref, w_ref, fw_ref, o_ref):
    y = jnp.dot(hm_ref[...], w_ref[0:M_WIDTH, :], preferred_element_type=F32)
    y = y + jnp.dot(ha_ref[...], w_ref[M_WIDTH:, :], preferred_element_type=F32)
    r = x_ref[...] + gate_ref[0] * y
    o_ref[...] = r * lax.rsqrt(jnp.mean(r * r, axis=-1, keepdims=True) + EPS) * fw_ref[...]


def _outproj(hm2d, ha2d, x2d, mod3, w_out, final_w, B, S, tm=512):
    M, D = x2d.shape
    tiles_per_seq = S // tm
    return pl.pallas_call(
        _outproj_kernel,
        grid=(M // tm,),
        in_specs=[
            pl.BlockSpec((tm, M_WIDTH), lambda i: (i, 0)),
            pl.BlockSpec((tm, A_WIDTH), lambda i: (i, 0)),
            pl.BlockSpec((tm, D), lambda i: (i, 0)),
            pl.BlockSpec((1, 1, D), lambda i: (2 * B + i // tiles_per_seq, 0, 0)),
            pl.BlockSpec((M_WIDTH + A_WIDTH, D), lambda i: (0, 0)),
            pl.BlockSpec((1, D), lambda i: (0, 0)),
        ],
        out_specs=pl.BlockSpec((tm, D), lambda i: (i, 0)),
        out_shape=jax.ShapeDtypeStruct((M, D), F32),
        compiler_params=pltpu.CompilerParams(
            dimension_semantics=("arbitrary",), vmem_limit_bytes=VMEM_LIMIT),
        name="outproj",
    )(hm2d, ha2d, x2d, mod3, w_out, final_w)


def kernel(x, c, norm_w, w_ada, b_ada, w_in, b_i, b_f, conv_q_w, conv_k_w, m_norm_w,
           lambda_q1, lambda_k1, lambda_q2, lambda_k2, a_norm_w, rel_bias, w_out, final_norm_w):
    B, S, D = x.shape
    assert norm_w.shape[0] == 1, "single layer only"
    x2d = x.reshape(B * S, D)

    mod = _adaln(c, w_ada[0], b_ada[0])
    mod3 = mod.reshape(3 * B, 1, D)

    w_in0 = w_in[0]
    half = M_WIDTH // 2
    mo0, mz0 = 3 * M_WIDTH, 4 * M_WIDTH
    a0 = GATE_COL0 + 2 * M_HEADS
    w_main = jnp.concatenate([
        w_in0[:, :3 * M_WIDTH],
        w_in0[:, mo0:mo0 + half], w_in0[:, mz0:mz0 + half],
        w_in0[:, mo0 + half:mo0 + M_WIDTH], w_in0[:, mz0 + half:mz0 + M_WIDTH],
        w_in0[:, a0:]], axis=1).astype(BF16)
    w_gate = jnp.pad(w_in0[:, GATE_COL0:a0], ((0, 0), (0, GATE_PAD - 2 * M_HEADS))).astype(BF16)
    conv_w = jnp.stack([conv_q_w[0], conv_k_w[0]])
    p, og, gates = _inproj(x2d, mod3, norm_w, w_main, w_gate, conv_w, B, S)
    p3 = p.reshape(B, S, P_COLS)
    gates3 = gates.reshape(B, S, GATE_PAD)

    gbias = jnp.pad(jnp.concatenate([b_i[0], b_f[0]]), (0, GATE_PAD - 2 * M_HEADS)).reshape(1, GATE_PAD)
    hm = _mlstm(p3, og.reshape(B, S, M_WIDTH), gates3, m_norm_w, gbias)

    lam_p = jnp.stack([lambda_q1[0], lambda_k1[0], lambda_q2[0], lambda_k2[0]])
    ha = _attn(p3, rel_bias, lam_p, a_norm_w)

    out = _outproj(hm.reshape(B * S, M_WIDTH), ha.reshape(B * S, A_WIDTH), x2d, mod3,
                   w_out[0].astype(BF16), final_norm_w.reshape(1, D), B, S)
    return out.reshape(B, S, D)
```

```python
import functools
import math

import numpy as np
import jax
import jax.numpy as jnp
from jax import lax
from jax.experimental import pallas as pl
from jax.experimental.pallas import tpu as pltpu

F32 = jnp.float32
BF16 = jnp.bfloat16

D_MODEL = 1024
M_WIDTH = 1024
M_HEADS = 4
M_HD = 256
CONV_K = 4
A_WIDTH = 1024
A_HEADS = 8
A_VHD = 128
A_QKD = 64
N_BUCKETS = 32
MAX_DIST = 128
EPS = 1e-6
LAM_INIT = 0.8 - 0.6 * math.exp(-0.3 * 0)

N_MAIN = 5 * M_WIDTH + 4 * A_WIDTH
GATE_COL0 = 5 * M_WIDTH
GATE_PAD = 128

M_CHUNK = 256
M_AUG = M_HD + 128
M_NB = 2
ATT_BLK = 256
ATT_HB = 2
VT_ROWS = A_VHD + 16
LOG2E = math.log2(math.e)
NEG = -1e30

VMEM_LIMIT = 56 * 1024 * 1024


def _nt_dot(a, b):
    return lax.dot_general(a, b, (((1,), (1,)), ((), ())), preferred_element_type=F32)


def _tn_dot(a, b):
    return lax.dot_general(a, b, (((0,), (0,)), ((), ())), preferred_element_type=F32)


def _sigmoid(x):
    return 1.0 / (1.0 + jnp.exp(-x))


def _adaln_kernel(c_ref, w_ref, b_ref, o_ref):
    c = c_ref[...]
    sc = (c * _sigmoid(c)).astype(BF16)
    o_ref[0] = jnp.dot(sc, w_ref[...].astype(BF16), preferred_element_type=F32) + b_ref[0]


def _adaln(c, w_ada, b_ada):
    B, D = c.shape
    return pl.pallas_call(
        _adaln_kernel,
        grid=(3,),
        in_specs=[
            pl.BlockSpec((B, D), lambda j: (0, 0)),
            pl.BlockSpec((D, D), lambda j: (0, j)),
            pl.BlockSpec((1, 1, D), lambda j: (j, 0, 0)),
        ],
        out_specs=pl.BlockSpec((1, B, D), lambda j: (j, 0, 0)),
        out_shape=jax.ShapeDtypeStruct((3, B, D), F32),
        compiler_params=pltpu.CompilerParams(
            dimension_semantics=("arbitrary",), vmem_limit_bytes=VMEM_LIMIT),
        name="adaln",
    )(c, w_ada, b_ada.reshape(3, 1, D))


IN_TN = 1024
IN_TILES = 9
P_TILES = 7
P_COLS = P_TILES * IN_TN
IN_ORDER = (0, 1, 3, 4, 8, 2, 5, 6, 7)
IN_P_SLOT = {0: 0, 1: 1, 2: 2, 5: 3, 6: 4, 7: 5, 8: 6}


def _inproj_kernel(x_ref, shift_ref, scale_ref, nw_ref, w_ref, wg_ref, cw_ref,
                   p_ref, g_ref, gates_ref, h_scr, raw0, raw1, tail_scr, *, tiles_per_seq):
    i = pl.program_id(0)
    tm = x_ref.shape[0]
    tn = IN_TN
    raws = (raw0, raw1)
    col_scale = {0: M_HD ** -0.5, 5: A_QKD ** -0.5 * LOG2E}

    x = x_ref[...]
    y = x * lax.rsqrt(jnp.mean(x * x, axis=-1, keepdims=True) + EPS) * nw_ref[...]
    h = (y * (1.0 + scale_ref[0]) + shift_ref[0]).astype(BF16)
    h_scr[...] = h
    gates_ref[...] = jnp.dot(h, wg_ref[...], preferred_element_type=F32)

    @pl.when(i == 0)
    def _():
        tail_scr[...] = jnp.zeros_like(tail_scr)

    first = (i % tiles_per_seq) == 0

    def matmul(j, buf):
        raws[buf][8:tm + 8, :] = jnp.dot(h_scr[...], w_ref[:, j * tn:(j + 1) * tn],
                                         preferred_element_type=F32)

    def tail(j, buf):
        raw = raws[buf]
        if j in (3, 4):
            o, z = raw[8:tm + 8, 0:tn // 2], raw[8:tm + 8, tn // 2:tn]
            g_ref[:, (j - 3) * (tn // 2):(j - 2) * (tn // 2)] = (_sigmoid(o) * (z * _sigmoid(z))).astype(BF16)
            return
        if j in (0, 1):
            raw[0:8, :] = jnp.where(first, 0.0, tail_scr[j])
            y = raw[8:tm + 8, :] * cw_ref[j, CONV_K - 1:CONV_K, :]
            for jj in range(CONV_K - 1):
                off = 8 - (CONV_K - 1) + jj
                y = y + raw[off:off + tm, :] * cw_ref[j, jj:jj + 1, :]
            tail_scr[j] = raw[tm:tm + 8, :]
            y = y * _sigmoid(y)
        elif j == 8:
            a = raw[8:tm + 8, :]
            y = a * _sigmoid(a)
        else:
            y = raw[8:tm + 8, :]
        if j in col_scale:
            y = y * col_scale[j]
        slot = IN_P_SLOT[j]
        p_ref[:, slot * tn:(slot + 1) * tn] = y.astype(BF16)

    matmul(IN_ORDER[0], 0)
    for n in range(1, IN_TILES):
        tail(IN_ORDER[n - 1], (n - 1) % 2)
        matmul(IN_ORDER[n], n % 2)
    tail(IN_ORDER[-1], (IN_TILES - 1) % 2)


def _inproj(x2d, mod3, norm_w, w_main, w_gate, conv_w, B, S, tm=512):
    M, D = x2d.shape
    tn = IN_TN
    tiles_per_seq = S // tm
    return pl.pallas_call(
        functools.partial(_inproj_kernel, tiles_per_seq=tiles_per_seq),
        grid=(M // tm,),
        in_specs=[
            pl.BlockSpec((tm, D), lambda i: (i, 0)),
            pl.BlockSpec((1, 1, D), lambda i: (i // tiles_per_seq, 0, 0)),
            pl.BlockSpec((1, 1, D), lambda i: (B + i // tiles_per_seq, 0, 0)),
            pl.BlockSpec((1, D), lambda i: (0, 0)),
            pl.BlockSpec((D, IN_TILES * tn), lambda i: (0, 0), pipeline_mode=pl.Buffered(1)),
            pl.BlockSpec((D, GATE_PAD), lambda i: (0, 0)),
            pl.BlockSpec((2, CONV_K, tn), lambda i: (0, 0, 0)),
        ],
        out_specs=[
            pl.BlockSpec((tm, P_COLS), lambda i: (i, 0)),
            pl.BlockSpec((tm, M_WIDTH), lambda i: (i, 0)),
            pl.BlockSpec((tm, GATE_PAD), lambda i: (i, 0)),
        ],
        out_shape=[
            jax.ShapeDtypeStruct((M, P_COLS), BF16),
            jax.ShapeDtypeStruct((M, M_WIDTH), BF16),
            jax.ShapeDtypeStruct((M, GATE_PAD), F32),
        ],
        scratch_shapes=[
            pltpu.VMEM((tm, D), BF16),
            pltpu.VMEM((tm + 8, tn), F32),
            pltpu.VMEM((tm + 8, tn), F32),
            pltpu.VMEM((2, 8, tn), F32),
        ],
        compiler_params=pltpu.CompilerParams(
            dimension_semantics=("arbitrary",), vmem_limit_bytes=VMEM_LIMIT),
        name="inproj",
    )(x2d, mod3, mod3, norm_w, w_main, w_gate, conv_w)


def _mlstm_kernel(q_ref, k_ref, v_ref, og_ref, g_ref, nw_ref, gb_ref, o_ref, ct_scr, m_scr):
    L = M_CHUNK
    NB = q_ref.shape[0]
    c = pl.program_id(1)

    @pl.when(c == 0)
    def _():
        ct_scr[...] = jnp.zeros_like(ct_scr)
        m_scr[...] = jnp.zeros_like(m_scr)

    row = lax.broadcasted_iota(jnp.int32, (L, GATE_PAD), 0)

    def gate_math(bb):
        g = g_ref[bb] + gb_ref[...]
        lf = jnp.minimum(g, 0.0) - jnp.log1p(jnp.exp(-jnp.abs(g)))
        bc = lf
        sh = 1
        while sh < L:
            bc = bc + jnp.where(row >= sh, pltpu.roll(bc, sh, 0), 0.0)
            sh *= 2
        bsh = pltpu.roll(bc, GATE_PAD - M_HEADS, 1)
        a = g - bsh
        cm = a
        sh = 1
        while sh < L:
            cm = jnp.maximum(cm, jnp.where(row >= sh, pltpu.roll(cm, sh, 0), -jnp.inf))
            sh *= 2
        m_prev = m_scr[bb]
        mm = jnp.maximum(cm, m_prev)
        iw = jnp.exp(m_prev - mm)
        eneg = jnp.exp(-(bsh + mm))
        m_scr[bb] = bsh[L - 1:L, :] + mm[L - 1:L, :]
        return dict(a=a, a_t=a.T, mm=mm, iw=iw, eneg=eneg)

    gm = [gate_math(bb) for bb in range(NB)]

    tri = (lax.broadcasted_iota(jnp.int32, (L, L), 1) <= lax.broadcasted_iota(jnp.int32, (L, L), 0))
    ones_col = (lax.broadcasted_iota(jnp.int32, (L, M_AUG - M_HD), 1) == 0).astype(BF16)

    chains = [(bb, h) for bb in range(NB) for h in range(M_HEADS)]
    cols = [slice(h * M_HD, (h + 1) * M_HD) for h in range(M_HEADS)]
    qs = {ch: q_ref[ch[0], :, cols[ch[1]]] for ch in chains}
    ks = {ch: k_ref[ch[0], :, cols[ch[1]]] for ch in chains}
    v_aug = {ch: jnp.concatenate([v_ref[ch[0], :, cols[ch[1]]], ones_col], axis=1) for ch in chains}
    cts = {ch: ct_scr[ch[0], ch[1]] for ch in chains}

    s_raw = {ch: _nt_dot(qs[ch], ks[ch]) for ch in chains}
    r_state = {ch: jnp.dot(qs[ch], cts[ch].astype(BF16), preferred_element_type=F32) for ch in chains}
    for ch in chains:
        bb, h = ch
        m_last = gm[bb]["mm"][L - 1:L, h:h + 1]
        w_col = jnp.exp(gm[bb]["a"][:, h:h + 1] - m_last)
        decay = gm[bb]["iw"][L - 1:L, h:h + 1]
        wv = (w_col * v_aug[ch].astype(F32)).astype(BF16)
        ct_scr[bb, h] = decay * cts[ch] + _tn_dot(ks[ch], wv)

    s16 = {}
    for ch in chains:
        bb, h = ch
        dw = jnp.where(tri, jnp.exp(gm[bb]["a_t"][h:h + 1, :] - gm[bb]["mm"][:, h:h + 1]), 0.0)
        s16[ch] = (s_raw[ch] * dw).astype(BF16)
    rs = {ch: gm[ch[0]]["iw"][:, ch[1]:ch[1] + 1] * r_state[ch]
          + jnp.dot(s16[ch], v_aug[ch], preferred_element_type=F32) for ch in chains}

    for ch in chains:
        bb, h = ch
        num = rs[ch][:, 0:M_HD]
        d = jnp.maximum(jnp.abs(rs[ch][:, M_HD:M_HD + 1]), gm[bb]["eneg"][:, h:h + 1])
        hc = num - jnp.mean(num, axis=-1, keepdims=True)
        var = jnp.mean(hc * hc, axis=-1, keepdims=True)
        hn = hc * lax.rsqrt(var + EPS * d * d) * nw_ref[:, cols[h]]
        o_ref[bb, :, cols[h]] = (hn * og_ref[bb, :, cols[h]].astype(F32)).astype(BF16)


def _mlstm(p3, og3, gates3, m_norm_w, gbias):
    B, S, _ = p3.shape
    L = M_CHUNK
    W = M_WIDTH

    NB = M_NB

    def col(cb):
        return pl.BlockSpec((NB, L, W), lambda b, c, cb=cb: (b, c, cb))

    return pl.pallas_call(
        _mlstm_kernel,
        grid=(B // NB, S // L),
        in_specs=[
            col(0), col(1), col(2),
            pl.BlockSpec((NB, L, W), lambda b, c: (b, c, 0)),
            pl.BlockSpec((NB, L, GATE_PAD), lambda b, c: (b, c, 0)),
            pl.BlockSpec((1, W), lambda b, c: (0, 0)),
            pl.BlockSpec((1, GATE_PAD), lambda b, c: (0, 0)),
        ],
        out_specs=pl.BlockSpec((NB, L, W), lambda b, c: (b, c, 0)),
        out_shape=jax.ShapeDtypeStruct((B, S, W), BF16),
        scratch_shapes=[
            pltpu.VMEM((NB, M_HEADS, M_HD, M_AUG), F32),
            pltpu.VMEM((NB, 1, GATE_PAD), F32),
        ],
        compiler_params=pltpu.CompilerParams(
            dimension_semantics=("arbitrary", "arbitrary"), vmem_limit_bytes=VMEM_LIMIT),
        name="mlstm",
    )(p3, p3, p3, og3, gates3, m_norm_w, gbias)


def _bucket_tiles():
    T = ATT_BLK
    kk = np.arange(T)[:, None]
    qq = np.arange(T)[None, :]
    out = []
    for base in (0, T):
        n = base + qq - kk
        max_exact = N_BUCKETS // 2
        nf = np.maximum(n, 1).astype(np.float64)
        large = max_exact + (np.log(nf / max_exact) / math.log(MAX_DIST / max_exact)
                             * (N_BUCKETS - max_exact)).astype(np.int64)
        large = np.minimum(large, N_BUCKETS - 1)
        bk = np.where(n < max_exact, n, large)
        bk = np.where(n < 0, -1, bk)
        out.append(bk)
    return np.stack(out).astype(np.int32)


def _attn_kernel(relb_ref, bucket_ref, lam_ref, anw_ref, q_ref, k_ref, v_ref, z_ref, o_ref,
                 bias_scr, vt_scr, qq_scr, s0_scr, s1_scr, acc_scr):
    T = ATT_BLK
    HB = ATT_HB
    S = k_ref.shape[1]
    NQ = S // T
    hg = pl.program_id(0)
    b = pl.program_id(1)
    heads = [slice(hb * A_VHD, (hb + 1) * A_VHD) for hb in range(HB)]

    @pl.when(b == 0)
    def _():
        for hb in range(HB):
            h = hg * HB + hb
            far = relb_ref[N_BUCKETS - 1, h]
            for t in range(2):
                bk = bucket_ref[t]
                tile = jnp.zeros((T, T), F32)
                for bb in range(N_BUCKETS - 1):
                    tile = jnp.where(bk == bb, (relb_ref[bb, h] - far) * LOG2E, tile)
                tile = jnp.where(bk < 0, NEG, tile)
                bias_scr[hb, t] = jnp.concatenate([tile, tile], axis=1)

    lane = lax.broadcasted_iota(jnp.int32, (T, A_VHD), 1)
    ones_rows = (lax.broadcasted_iota(jnp.int32, (VT_ROWS - A_VHD, T), 0) == 0).astype(BF16)
    for hb in range(HB):
        for ci in range(NQ):
            rows = slice(ci * T, (ci + 1) * T)
            vt_scr[hb, 0:A_VHD, rows] = v_ref[0, rows, heads[hb]].astype(F32).T.astype(BF16)
            vt_scr[hb, A_VHD:VT_ROWS, rows] = ones_rows
            qs = q_ref[0, rows, heads[hb]]
            zero = jnp.zeros_like(qs)
            qq_scr[hb, ci, 0:T, :] = jnp.where(lane < A_QKD, qs, zero)
            qq_scr[hb, ci, T:2 * T, :] = jnp.where(lane >= A_QKD, qs, zero)
    acc_scr[...] = jnp.zeros_like(acc_scr)

    s_bufs = (s0_scr, s1_scr)

    def issue(buf, qb, kj, hb):
        s_bufs[buf][hb] = _nt_dot(k_ref[0, kj * T:(kj + 1) * T, heads[hb]], qq_scr[hb, qb])

    def run_stage(cur, nxt, kj, bias_idx, ms):
        start = kj * T
        out = []
        for hb in range(HB):
            s = s_bufs[cur][hb]
            if bias_idx is not None:
                s = s + bias_scr[hb, bias_idx]
            m_old = ms[hb]
            m_new = jnp.maximum(m_old, jnp.max(s, axis=0, keepdims=True))
            alpha = jnp.exp2(m_old - m_new)
            p = jnp.exp2(s - m_new).astype(BF16)
            if nxt is not None:
                issue(1 - cur, nxt[0], nxt[1], hb)
            pv = jnp.dot(vt_scr[hb, :, start:start + T], p, preferred_element_type=F32)
            acc_scr[hb] = alpha * acc_scr[hb] + pv
            out.append(m_new)
        return tuple(out)

    lam_p = lam_ref[...]
    lam = (jnp.exp(jnp.sum(lam_p[0:1] * lam_p[1:2], axis=-1, keepdims=True))
           - jnp.exp(jnp.sum(lam_p[2:3] * lam_p[3:4], axis=-1, keepdims=True)) + LAM_INIT)

    def finalize(qi):
        rows = pl.ds(pl.multiple_of(qi * T, T), T)
        for hb in range(HB):
            acc = acc_scr[hb]
            o = acc[0:A_VHD] / acc[A_VHD:A_VHD + 1]
            d = (o[:, 0:T] - lam * o[:, T:2 * T]).T
            y = d * lax.rsqrt(jnp.mean(d * d, axis=-1, keepdims=True) + EPS) * anw_ref[...] * (1.0 - LAM_INIT)
            o_ref[0, rows, heads[hb]] = (y * z_ref[0, rows, heads[hb]].astype(F32)).astype(BF16)

    m_init = tuple(jnp.full((1, 2 * T), NEG, F32) for _ in range(HB))

    pairs = [(qi, kj) for qi in range(NQ) for kj in range(qi + 1)]
    for hb in range(HB):
        issue(0, 0, 0, hb)
    ms = m_init
    for t, (qi, kj) in enumerate(pairs):
        nxt = pairs[t + 1] if t + 1 < len(pairs) else None
        bias_idx = 0 if kj == qi else (1 if kj == qi - 1 else None)
        ms = run_stage(t % 2, nxt, kj, bias_idx, ms)
        if kj == qi:
            finalize(qi)
            ms = m_init


def _attn(proj3, rel_bias, lam_p, a_norm_w):
    B, S, _ = proj3.shape
    T = ATT_BLK
    HB = ATT_HB
    W = HB * A_VHD
    q0 = 3 * IN_TN // W
    k0 = q0 + A_HEADS // HB
    v0 = k0 + A_HEADS // HB
    z0 = v0 + A_HEADS // HB
    buckets = jnp.asarray(_bucket_tiles())
    return pl.pallas_call(
        _attn_kernel,
        grid=(A_HEADS // HB, B),
        in_specs=[
            pl.BlockSpec(memory_space=pltpu.SMEM),
            pl.BlockSpec((2, T, T), lambda h, b: (0, 0, 0)),
            pl.BlockSpec((4, A_QKD), lambda h, b: (0, 0)),
            pl.BlockSpec((1, A_VHD), lambda h, b: (0, 0)),
            pl.BlockSpec((1, S, W), lambda h, b: (b, 0, q0 + h)),
            pl.BlockSpec((1, S, W), lambda h, b: (b, 0, k0 + h)),
            pl.BlockSpec((1, S, W), lambda h, b: (b, 0, v0 + h)),
            pl.BlockSpec((1, S, W), lambda h, b: (b, 0, z0 + h)),
        ],
        out_specs=pl.BlockSpec((1, S, W), lambda h, b: (b, 0, h)),
        out_shape=jax.ShapeDtypeStruct((B, S, A_WIDTH), BF16),
        scratch_shapes=[
            pltpu.VMEM((HB, 2, T, 2 * T), F32),
            pltpu.VMEM((HB, VT_ROWS, S), BF16),
            pltpu.VMEM((HB, S // T, 2 * T, A_VHD), BF16),
            pltpu.VMEM((HB, T, 2 * T), F32),
            pltpu.VMEM((HB, T, 2 * T), F32),
            pltpu.VMEM((HB, VT_ROWS, 2 * T), F32),
        ],
        compiler_params=pltpu.CompilerParams(
            dimension_semantics=("arbitrary", "arbitrary"), vmem_limit_bytes=VMEM_LIMIT),
        name="attn",
    )(rel_bias, buckets, lam_p, a_norm_w, proj3, proj3, proj3, proj3)


def _outproj_kernel(hm_ref, ha_ref, x_ref, gate_ref, w_ref, fw_ref, o_ref, raw0, raw1):
    sub = raw0.shape[0]
    nsub = x_ref.shape[0] // sub
    raws = (raw0, raw1)

    def matmul(r, buf):
        rows = slice(r * sub, (r + 1) * sub)
        y = jnp.dot(hm_ref[rows, :], w_ref[0:M_WIDTH, :], preferred_element_type=F32)
        raws[buf][...] = y + jnp.dot(ha_ref[rows, :], w_ref[M_WIDTH:, :], preferred_element_type=F32)

    def tail(r, buf):
        rows = slice(r * sub, (r + 1) * sub)
        res = x_ref[rows, :] + gate_ref[0] * raws[buf][...]
        o_ref[rows, :] = res * lax.rsqrt(jnp.mean(res * res, axis=-1, keepdims=True) + EPS) * fw_ref[...]

    matmul(0, 0)
    for r in range(1, nsub):
        tail(r - 1, (r - 1) % 2)
        matmul(r, r % 2)
    tail(nsub - 1, (nsub - 1) % 2)


def _outproj(hm2d, ha2d, x2d, mod3, w_out, final_w, B, S, tm=1024, sub=512):
    M, D = x2d.shape
    tiles_per_seq = S // tm
    return pl.pallas_call(
        _outproj_kernel,
        grid=(M // tm,),
        scratch_shapes=[pltpu.VMEM((sub, D), F32), pltpu.VMEM((sub, D), F32)],
        in_specs=[
            pl.BlockSpec((tm, M_WIDTH), lambda i: (i, 0)),
            pl.BlockSpec((tm, A_WIDTH), lambda i: (i, 0)),
            pl.BlockSpec((tm, D), lambda i: (i, 0)),
            pl.BlockSpec((1, 1, D), lambda i: (2 * B + i // tiles_per_seq, 0, 0)),
            pl.BlockSpec((M_WIDTH + A_WIDTH, D), lambda i: (0, 0)),
            pl.BlockSpec((1, D), lambda i: (0, 0)),
        ],
        out_specs=pl.BlockSpec((tm, D), lambda i: (i, 0)),
        out_shape=jax.ShapeDtypeStruct((M, D), F32),
        compiler_params=pltpu.CompilerParams(
            dimension_semantics=("arbitrary",), vmem_limit_bytes=VMEM_LIMIT),
        name="outproj",
    )(hm2d, ha2d, x2d, mod3, w_out, final_w)


def kernel(x, c, norm_w, w_ada, b_ada, w_in, b_i, b_f, conv_q_w, conv_k_w, m_norm_w,
           lambda_q1, lambda_k1, lambda_q2, lambda_k2, a_norm_w, rel_bias, w_out, final_norm_w):
    B, S, D = x.shape
    assert norm_w.shape[0] == 1, "single layer only"
    x2d = x.reshape(B * S, D)

    mod = _adaln(c, w_ada[0], b_ada[0])
    mod3 = mod.reshape(3 * B, 1, D)

    w_in0 = w_in[0]
    half = M_WIDTH // 2
    mo0, mz0 = 3 * M_WIDTH, 4 * M_WIDTH
    a0 = GATE_COL0 + 2 * M_HEADS
    w_main = jnp.concatenate([
        w_in0[:, :3 * M_WIDTH],
        w_in0[:, mo0:mo0 + half], w_in0[:, mz0:mz0 + half],
        w_in0[:, mo0 + half:mo0 + M_WIDTH], w_in0[:, mz0 + half:mz0 + M_WIDTH],
        w_in0[:, a0:]], axis=1).astype(BF16)
    w_gate = jnp.pad(w_in0[:, GATE_COL0:a0], ((0, 0), (0, GATE_PAD - 2 * M_HEADS))).astype(BF16)
    conv_w = jnp.stack([conv_q_w[0], conv_k_w[0]])
    p, og, gates = _inproj(x2d, mod3, norm_w, w_main, w_gate, conv_w, B, S)
    p3 = p.reshape(B, S, P_COLS)
    gates3 = gates.reshape(B, S, GATE_PAD)

    gbias = jnp.pad(jnp.concatenate([b_i[0], b_f[0]]), (0, GATE_PAD - 2 * M_HEADS)).reshape(1, GATE_PAD)
    hm = _mlstm(p3, og.reshape(B, S, M_WIDTH), gates3, m_norm_w, gbias)

    lam_p = jnp.stack([lambda_q1[0], lambda_k1[0], lambda_q2[0], lambda_k2[0]])
    ha = _attn(p3, rel_bias, lam_p, a_norm_w)

    out = _outproj(hm.reshape(B * S, M_WIDTH), ha.reshape(B * S, A_WIDTH), x2d, mod3,
                   w_out[0].astype(BF16), final_norm_w.reshape(1, D), B, S)
    return out.reshape(B, S, D)
```

```python
import functools
import math

import numpy as np
import jax
import jax.numpy as jnp
from jax import lax
from jax.experimental import pallas as pl
from jax.experimental.pallas import tpu as pltpu

F32 = jnp.float32
BF16 = jnp.bfloat16

D_MODEL = 1024
M_WIDTH = 1024
M_HEADS = 4
M_HD = 256
CONV_K = 4
A_WIDTH = 1024
A_HEADS = 8
A_VHD = 128
A_QKD = 64
N_BUCKETS = 32
MAX_DIST = 128
EPS = 1e-6
LAM_INIT = 0.8 - 0.6 * math.exp(-0.3 * 0)

N_MAIN = 5 * M_WIDTH + 4 * A_WIDTH
GATE_COL0 = 5 * M_WIDTH
GATE_PAD = 128

M_CHUNK = 256
M_AUG = M_HD + 128
M_NB = 2
ATT_BLK = 256
ATT_HB = 2
VT_ROWS = A_VHD + 16
LOG2E = math.log2(math.e)
NEG = -1e30

VMEM_LIMIT = 56 * 1024 * 1024


def _nt_dot(a, b):
    return lax.dot_general(a, b, (((1,), (1,)), ((), ())), preferred_element_type=F32)


def _tn_dot(a, b):
    return lax.dot_general(a, b, (((0,), (0,)), ((), ())), preferred_element_type=F32)


def _sigmoid(x):
    return 1.0 / (1.0 + jnp.exp(-x))


def _adaln_kernel(c_ref, w_ref, b_ref, o_ref):
    c = c_ref[...]
    sc = (c * _sigmoid(c)).astype(BF16)
    o_ref[0] = jnp.dot(sc, w_ref[...].astype(BF16), preferred_element_type=F32) + b_ref[0]


def _adaln(c, w_ada, b_ada):
    B, D = c.shape
    return pl.pallas_call(
        _adaln_kernel,
        grid=(3,),
        in_specs=[
            pl.BlockSpec((B, D), lambda j: (0, 0)),
            pl.BlockSpec((D, D), lambda j: (0, j)),
            pl.BlockSpec((1, 1, D), lambda j: (j, 0, 0)),
        ],
        out_specs=pl.BlockSpec((1, B, D), lambda j: (j, 0, 0)),
        out_shape=jax.ShapeDtypeStruct((3, B, D), F32),
        compiler_params=pltpu.CompilerParams(
            dimension_semantics=("arbitrary",), vmem_limit_bytes=VMEM_LIMIT),
        name="adaln",
    )(c, w_ada, b_ada.reshape(3, 1, D))


IN_TN = 1024
IN_TILES = 9
P_TILES = 7
P_COLS = P_TILES * IN_TN
IN_ORDER = (0, 1, 3, 4, 8, 2, 5, 6, 7)
IN_P_SLOT = {0: 0, 1: 1, 2: 2, 5: 3, 6: 4, 7: 5, 8: 6}


def _inproj_kernel(x_ref, shift_ref, scale_ref, nw_ref, w_ref, wg_ref, cw_ref,
                   p_ref, g_ref, gates_ref, h_scr, raw0, raw1, tail_scr, *, tiles_per_seq):
    i = pl.program_id(0)
    tm = x_ref.shape[0]
    tn = IN_TN
    raws = (raw0, raw1)
    col_scale = {0: M_HD ** -0.5, 5: A_QKD ** -0.5 * LOG2E}

    x = x_ref[...]
    y = x * lax.rsqrt(jnp.mean(x * x, axis=-1, keepdims=True) + EPS) * nw_ref[...]
    h = (y * (1.0 + scale_ref[0]) + shift_ref[0]).astype(BF16)
    h_scr[...] = h
    gates_ref[...] = jnp.dot(h, wg_ref[...], preferred_element_type=F32)

    @pl.when(i == 0)
    def _():
        tail_scr[...] = jnp.zeros_like(tail_scr)

    first = (i % tiles_per_seq) == 0

    def matmul(j, buf):
        raws[buf][8:tm + 8, :] = jnp.dot(h_scr[...], w_ref[:, j * tn:(j + 1) * tn],
                                         preferred_element_type=F32)

    def tail(j, buf):
        raw = raws[buf]
        if j in (3, 4):
            o, z = raw[8:tm + 8, 0:tn // 2], raw[8:tm + 8, tn // 2:tn]
            g_ref[:, (j - 3) * (tn // 2):(j - 2) * (tn // 2)] = (_sigmoid(o) * (z * _sigmoid(z))).astype(BF16)
            return
        if j in (0, 1):
            raw[0:8, :] = jnp.where(first, 0.0, tail_scr[j])
            y = raw[8:tm + 8, :] * cw_ref[j, CONV_K - 1:CONV_K, :]
            for jj in range(CONV_K - 1):
                off = 8 - (CONV_K - 1) + jj
                y = y + raw[off:off + tm, :] * cw_ref[j, jj:jj + 1, :]
            tail_scr[j] = raw[tm:tm + 8, :]
            y = y * _sigmoid(y)
        elif j == 8:
            a = raw[8:tm + 8, :]
            y = a * _sigmoid(a)
        else:
            y = raw[8:tm + 8, :]
        if j in col_scale:
            y = y * col_scale[j]
        slot = IN_P_SLOT[j]
        p_ref[:, slot * tn:(slot + 1) * tn] = y.astype(BF16)

    matmul(IN_ORDER[0], 0)
    for n in range(1, IN_TILES):
        tail(IN_ORDER[n - 1], (n - 1) % 2)
        matmul(IN_ORDER[n], n % 2)
    tail(IN_ORDER[-1], (IN_TILES - 1) % 2)


def _inproj(x2d, mod3, norm_w, w_main, w_gate, conv_w, B, S, tm=512):
    M, D = x2d.shape
    tn = IN_TN
    tiles_per_seq = S // tm
    return pl.pallas_call(
        functools.partial(_inproj_kernel, tiles_per_seq=tiles_per_seq),
        grid=(M // tm,),
        in_specs=[
            pl.BlockSpec((tm, D), lambda i: (i, 0)),
            pl.BlockSpec((1, 1, D), lambda i: (i // tiles_per_seq, 0, 0)),
            pl.BlockSpec((1, 1, D), lambda i: (B + i // tiles_per_seq, 0, 0)),
            pl.BlockSpec((1, D), lambda i: (0, 0)),
            pl.BlockSpec((D, IN_TILES * tn), lambda i: (0, 0), pipeline_mode=pl.Buffered(1)),
            pl.BlockSpec((D, GATE_PAD), lambda i: (0, 0)),
            pl.BlockSpec((2, CONV_K, tn), lambda i: (0, 0, 0)),
        ],
        out_specs=[
            pl.BlockSpec((tm, P_COLS), lambda i: (i, 0)),
            pl.BlockSpec((tm, M_WIDTH), lambda i: (i, 0)),
            pl.BlockSpec((tm, GATE_PAD), lambda i: (i, 0)),
        ],
        out_shape=[
            jax.ShapeDtypeStruct((M, P_COLS), BF16),
            jax.ShapeDtypeStruct((M, M_WIDTH), BF16),
            jax.ShapeDtypeStruct((M, GATE_PAD), F32),
        ],
        scratch_shapes=[
            pltpu.VMEM((tm, D), BF16),
            pltpu.VMEM((tm + 8, tn), F32),
            pltpu.VMEM((tm + 8, tn), F32),
            pltpu.VMEM((2, 8, tn), F32),
        ],
        compiler_params=pltpu.CompilerParams(
            dimension_semantics=("arbitrary",), vmem_limit_bytes=VMEM_LIMIT),
        name="inproj",
    )(x2d, mod3, mod3, norm_w, w_main, w_gate, conv_w)


def _mlstm_kernel(q_ref, k_ref, v_ref, og_ref, g_ref, nw_ref, gb_ref, o_ref, ct_scr, m_scr):
    L = M_CHUNK
    NB = q_ref.shape[0]
    c = pl.program_id(1)

    @pl.when(c == 0)
    def _():
        ct_scr[...] = jnp.zeros_like(ct_scr)
        m_scr[...] = jnp.zeros_like(m_scr)

    row = lax.broadcasted_iota(jnp.int32, (L, GATE_PAD), 0)

    def gate_math(bb):
        g = g_ref[bb] + gb_ref[...]
        lf = jnp.minimum(g, 0.0) - jnp.log1p(jnp.exp(-jnp.abs(g)))
        bc = lf
        sh = 1
        while sh < L:
            bc = bc + jnp.where(row >= sh, pltpu.roll(bc, sh, 0), 0.0)
            sh *= 2
        bsh = pltpu.roll(bc, GATE_PAD - M_HEADS, 1)
        a = g - bsh
        cm = a
        sh = 1
        while sh < L:
            cm = jnp.maximum(cm, jnp.where(row >= sh, pltpu.roll(cm, sh, 0), -jnp.inf))
            sh *= 2
        m_prev = m_scr[bb]
        mm = jnp.maximum(cm, m_prev)
        iw = jnp.exp(m_prev - mm)
        eneg = jnp.exp(-(bsh + mm))
        m_scr[bb] = bsh[L - 1:L, :] + mm[L - 1:L, :]
        return dict(a=a, a_t=a.T, mm=mm, iw=iw, eneg=eneg)

    gm = [gate_math(bb) for bb in range(NB)]

    tri = (lax.broadcasted_iota(jnp.int32, (L, L), 1) <= lax.broadcasted_iota(jnp.int32, (L, L), 0))
    ones_col = (lax.broadcasted_iota(jnp.int32, (L, M_AUG - M_HD), 1) == 0).astype(BF16)

    chains = [(bb, h) for bb in range(NB) for h in range(M_HEADS)]
    cols = [slice(h * M_HD, (h + 1) * M_HD) for h in range(M_HEADS)]
    qs = {ch: q_ref[ch[0], :, cols[ch[1]]] for ch in chains}
    ks = {ch: k_ref[ch[0], :, cols[ch[1]]] for ch in chains}
    v_aug = {ch: jnp.concatenate([v_ref[ch[0], :, cols[ch[1]]], ones_col], axis=1) for ch in chains}
    cts = {ch: ct_scr[ch[0], ch[1]] for ch in chains}

    s_raw = {ch: _nt_dot(qs[ch], ks[ch]) for ch in chains}
    r_state = {ch: jnp.dot(qs[ch], cts[ch].astype(BF16), preferred_element_type=F32) for ch in chains}
    for ch in chains:
        bb, h = ch
        m_last = gm[bb]["mm"][L - 1:L, h:h + 1]
        w_col = jnp.exp(gm[bb]["a"][:, h:h + 1] - m_last)
        decay = gm[bb]["iw"][L - 1:L, h:h + 1]
        wv = (w_col * v_aug[ch].astype(F32)).astype(BF16)
        ct_scr[bb, h] = decay * cts[ch] + _tn_dot(ks[ch], wv)

    s16 = {}
    for ch in chains:
        bb, h = ch
        dw = jnp.where(tri, jnp.exp(gm[bb]["a_t"][h:h + 1, :] - gm[bb]["mm"][:, h:h + 1]), 0.0)
        s16[ch] = (s_raw[ch] * dw).astype(BF16)
    rs = {ch: gm[ch[0]]["iw"][:, ch[1]:ch[1] + 1] * r_state[ch]
          + jnp.dot(s16[ch], v_aug[ch], preferred_element_type=F32) for ch in chains}

    for ch in chains:
        bb, h = ch
        num = rs[ch][:, 0:M_HD]
        d = jnp.maximum(jnp.abs(rs[ch][:, M_HD:M_HD + 1]), gm[bb]["eneg"][:, h:h + 1])
        hc = num - jnp.mean(num, axis=-1, keepdims=True)
        var = jnp.mean(hc * hc, axis=-1, keepdims=True)
        hn = hc * lax.rsqrt(var + EPS * d * d) * nw_ref[:, cols[h]]
        o_ref[bb, :, cols[h]] = (hn * og_ref[bb, :, cols[h]].astype(F32)).astype(BF16)


def _mlstm(p3, og3, gates3, m_norm_w, gbias):
    B, S, _ = p3.shape
    L = M_CHUNK
    W = M_WIDTH

    NB = M_NB

    def col(cb):
        return pl.BlockSpec((NB, L, W), lambda b, c, cb=cb: (b, c, cb))

    return pl.pallas_call(
        _mlstm_kernel,
        grid=(B // NB, S // L),
        in_specs=[
            col(0), col(1), col(2),
            pl.BlockSpec((NB, L, W), lambda b, c: (b, c, 0)),
            pl.BlockSpec((NB, L, GATE_PAD), lambda b, c: (b, c, 0)),
            pl.BlockSpec((1, W), lambda b, c: (0, 0)),
            pl.BlockSpec((1, GATE_PAD), lambda b, c: (0, 0)),
        ],
        out_specs=pl.BlockSpec((NB, L, W), lambda b, c: (b, c, 0)),
        out_shape=jax.ShapeDtypeStruct((B, S, W), BF16),
        scratch_shapes=[
            pltpu.VMEM((NB, M_HEADS, M_HD, M_AUG), F32),
            pltpu.VMEM((NB, 1, GATE_PAD), F32),
        ],
        compiler_params=pltpu.CompilerParams(
            dimension_semantics=("arbitrary", "arbitrary"), vmem_limit_bytes=VMEM_LIMIT),
        name="mlstm",
    )(p3, p3, p3, og3, gates3, m_norm_w, gbias)


def _bucket_tiles():
    T = ATT_BLK
    kk = np.arange(T)[:, None]
    qq = np.arange(T)[None, :]
    out = []
    for base in (0, T):
        n = base + qq - kk
        max_exact = N_BUCKETS // 2
        nf = np.maximum(n, 1).astype(np.float64)
        large = max_exact + (np.log(nf / max_exact) / math.log(MAX_DIST / max_exact)
                             * (N_BUCKETS - max_exact)).astype(np.int64)
        large = np.minimum(large, N_BUCKETS - 1)
        bk = np.where(n < max_exact, n, large)
        bk = np.where(n < 0, -1, bk)
        out.append(bk)
    return np.stack(out).astype(np.int32)


def _attn_kernel(relb_ref, bucket_ref, lam_ref, anw_ref, q_ref, k_ref, v_ref, z_ref, o_ref,
                 bias_scr, vt_scr, qq_scr, s0_scr, s1_scr, acc_scr):
    T = ATT_BLK
    HB = ATT_HB
    S = k_ref.shape[1]
    NQ = S // T
    hg = pl.program_id(0)
    b = pl.program_id(1)
    heads = [slice(hb * A_VHD, (hb + 1) * A_VHD) for hb in range(HB)]

    @pl.when(b == 0)
    def _():
        for hb in range(HB):
            h = hg * HB + hb
            far = relb_ref[N_BUCKETS - 1, h]
            for t in range(2):
                bk = bucket_ref[t]
                tile = jnp.zeros((T, T), F32)
                for bb in range(N_BUCKETS - 1):
                    tile = jnp.where(bk == bb, (relb_ref[bb, h] - far) * LOG2E, tile)
                tile = jnp.where(bk < 0, NEG, tile)
                bias_scr[hb, t] = jnp.concatenate([tile, tile], axis=1)

    lane = lax.broadcasted_iota(jnp.int32, (T, A_VHD), 1)
    ones_rows = (lax.broadcasted_iota(jnp.int32, (VT_ROWS - A_VHD, T), 0) == 0).astype(BF16)
    for hb in range(HB):
        for ci in range(NQ):
            rows = slice(ci * T, (ci + 1) * T)
            vt_scr[hb, 0:A_VHD, rows] = v_ref[0, rows, heads[hb]].astype(F32).T.astype(BF16)
            vt_scr[hb, A_VHD:VT_ROWS, rows] = ones_rows
            qs = q_ref[0, rows, heads[hb]]
            zero = jnp.zeros_like(qs)
            qq_scr[hb, ci, 0:T, :] = jnp.where(lane < A_QKD, qs, zero)
            qq_scr[hb, ci, T:2 * T, :] = jnp.where(lane >= A_QKD, qs, zero)
    acc_scr[...] = jnp.zeros_like(acc_scr)

    s_bufs = (s0_scr, s1_scr)

    def issue(buf, qb, kj, hb):
        s = _nt_dot(k_ref[0, kj * T:(kj + 1) * T, heads[hb]], qq_scr[hb, qb])
        if kj >= qb - 1:
            s = s + bias_scr[hb, qb - kj]
        s_bufs[buf][hb] = s.astype(BF16)

    def run_stage(cur, nxt, kj, ms):
        start = kj * T
        out = []
        for hb in range(HB):
            s = s_bufs[cur][hb]
            m_old = ms[hb]
            m_new = jnp.maximum(m_old, jnp.max(s, axis=0, keepdims=True).astype(F32))
            alpha = jnp.exp2(m_old - m_new)
            p = jnp.exp2(s - m_new.astype(BF16))
            if nxt is not None:
                issue(1 - cur, nxt[0], nxt[1], hb)
            pv = jnp.dot(vt_scr[hb, :, start:start + T], p, preferred_element_type=F32)
            acc_scr[hb] = alpha * acc_scr[hb] + pv
            out.append(m_new)
        return tuple(out)

    lam_p = lam_ref[...]
    lam = (jnp.exp(jnp.sum(lam_p[0:1] * lam_p[1:2], axis=-1, keepdims=True))
           - jnp.exp(jnp.sum(lam_p[2:3] * lam_p[3:4], axis=-1, keepdims=True)) + LAM_INIT)

    def finalize(qi):
        rows = pl.ds(pl.multiple_of(qi * T, T), T)
        for hb in range(HB):
            acc = acc_scr[hb]
            o = acc[0:A_VHD] / acc[A_VHD:A_VHD + 1]
            d = (o[:, 0:T] - lam * o[:, T:2 * T]).T
            y = d * lax.rsqrt(jnp.mean(d * d, axis=-1, keepdims=True) + EPS) * anw_ref[...] * (1.0 - LAM_INIT)
            o_ref[0, rows, heads[hb]] = (y * z_ref[0, rows, heads[hb]].astype(F32)).astype(BF16)

    m_init = tuple(jnp.full((1, 2 * T), NEG, F32) for _ in range(HB))

    pairs = [(qi, kj) for qi in range(NQ) for kj in range(qi + 1)]
    for hb in range(HB):
        issue(0, 0, 0, hb)
    ms = m_init
    for t, (qi, kj) in enumerate(pairs):
        nxt = pairs[t + 1] if t + 1 < len(pairs) else None
        ms = run_stage(t % 2, nxt, kj, ms)
        if kj == qi:
            finalize(qi)
            ms = m_init


def _attn(proj3, rel_bias, lam_p, a_norm_w):
    B, S, _ = proj3.shape
    T = ATT_BLK
    HB = ATT_HB
    W = HB * A_VHD
    q0 = 3 * IN_TN // W
    k0 = q0 + A_HEADS // HB
    v0 = k0 + A_HEADS // HB
    z0 = v0 + A_HEADS // HB
    buckets = jnp.asarray(_bucket_tiles())
    return pl.pallas_call(
        _attn_kernel,
        grid=(A_HEADS // HB, B),
        in_specs=[
            pl.BlockSpec(memory_space=pltpu.SMEM),
            pl.BlockSpec((2, T, T), lambda h, b: (0, 0, 0)),
            pl.BlockSpec((4, A_QKD), lambda h, b: (0, 0)),
            pl.BlockSpec((1, A_VHD), lambda h, b: (0, 0)),
            pl.BlockSpec((1, S, W), lambda h, b: (b, 0, q0 + h)),
            pl.BlockSpec((1, S, W), lambda h, b: (b, 0, k0 + h)),
            pl.BlockSpec((1, S, W), lambda h, b: (b, 0, v0 + h)),
            pl.BlockSpec((1, S, W), lambda h, b: (b, 0, z0 + h)),
        ],
        out_specs=pl.BlockSpec((1, S, W), lambda h, b: (b, 0, h)),
        out_shape=jax.ShapeDtypeStruct((B, S, A_WIDTH), BF16),
        scratch_shapes=[
            pltpu.VMEM((HB, 2, T, 2 * T), F32),
            pltpu.VMEM((HB, VT_ROWS, S), BF16),
            pltpu.VMEM((HB, S // T, 2 * T, A_VHD), BF16),
            pltpu.VMEM((HB, T, 2 * T), BF16),
            pltpu.VMEM((HB, T, 2 * T), BF16),
            pltpu.VMEM((HB, VT_ROWS, 2 * T), F32),
        ],
        compiler_params=pltpu.CompilerParams(
            dimension_semantics=("arbitrary", "arbitrary"), vmem_limit_bytes=VMEM_LIMIT),
        name="attn",
    )(rel_bias, buckets, lam_p, a_norm_w, proj3, proj3, proj3, proj3)


def _outproj_kernel(hm_ref, ha_ref, x_ref, gate_ref, w_ref, fw_ref, o_ref, raw0, raw1):
    sub = raw0.shape[0]
    nsub = x_ref.shape[0] // sub
    raws = (raw0, raw1)

    def matmul(r, buf):
        rows = slice(r * sub, (r + 1) * sub)
        y = jnp.dot(hm_ref[rows, :], w_ref[0:M_WIDTH, :], preferred_element_type=F32)
        raws[buf][...] = y + jnp.dot(ha_ref[rows, :], w_ref[M_WIDTH:, :], preferred_element_type=F32)

    def tail(r, buf):
        rows = slice(r * sub, (r + 1) * sub)
        res = x_ref[rows, :] + gate_ref[0] * raws[buf][...]
        o_ref[rows, :] = res * lax.rsqrt(jnp.mean(res * res, axis=-1, keepdims=True) + EPS) * fw_ref[...]

    matmul(0, 0)
    for r in range(1, nsub):
        tail(r - 1, (r - 1) % 2)
        matmul(r, r % 2)
    tail(nsub - 1, (nsub - 1) % 2)


def _outproj(hm2d, ha2d, x2d, mod3, w_out, final_w, B, S, tm=1024, sub=512):
    M, D = x2d.shape
    tiles_per_seq = S // tm
    return pl.pallas_call(
        _outproj_kernel,
        grid=(M // tm,),
        scratch_shapes=[pltpu.VMEM((sub, D), F32), pltpu.VMEM((sub, D), F32)],
        in_specs=[
            pl.BlockSpec((tm, M_WIDTH), lambda i: (i, 0)),
            pl.BlockSpec((tm, A_WIDTH), lambda i: (i, 0)),
            pl.BlockSpec((tm, D), lambda i: (i, 0)),
            pl.BlockSpec((1, 1, D), lambda i: (2 * B + i // tiles_per_seq, 0, 0)),
            pl.BlockSpec((M_WIDTH + A_WIDTH, D), lambda i: (0, 0)),
            pl.BlockSpec((1, D), lambda i: (0, 0)),
        ],
        out_specs=pl.BlockSpec((tm, D), lambda i: (i, 0)),
        out_shape=jax.ShapeDtypeStruct((M, D), F32),
        compiler_params=pltpu.CompilerParams(
            dimension_semantics=("arbitrary",), vmem_limit_bytes=VMEM_LIMIT),
        name="outproj",
    )(hm2d, ha2d, x2d, mod3, w_out, final_w)


def kernel(x, c, norm_w, w_ada, b_ada, w_in, b_i, b_f, conv_q_w, conv_k_w, m_norm_w,
           lambda_q1, lambda_k1, lambda_q2, lambda_k2, a_norm_w, rel_bias, w_out, final_norm_w):
    B, S, D = x.shape
    assert norm_w.shape[0] == 1, "single layer only"
    x2d = x.reshape(B * S, D)

    mod = _adaln(c, w_ada[0], b_ada[0])
    mod3 = mod.reshape(3 * B, 1, D)

    w_in0 = w_in[0]
    half = M_WIDTH // 2
    mo0, mz0 = 3 * M_WIDTH, 4 * M_WIDTH
    a0 = GATE_COL0 + 2 * M_HEADS
    w_main = jnp.concatenate([
        w_in0[:, :3 * M_WIDTH],
        w_in0[:, mo0:mo0 + half], w_in0[:, mz0:mz0 + half],
        w_in0[:, mo0 + half:mo0 + M_WIDTH], w_in0[:, mz0 + half:mz0 + M_WIDTH],
        w_in0[:, a0:]], axis=1).astype(BF16)
    w_gate = jnp.pad(w_in0[:, GATE_COL0:a0], ((0, 0), (0, GATE_PAD - 2 * M_HEADS))).astype(BF16)
    conv_w = jnp.stack([conv_q_w[0], conv_k_w[0]])
    p, og, gates = _inproj(x2d, mod3, norm_w, w_main, w_gate, conv_w, B, S)
    p3 = p.reshape(B, S, P_COLS)
    gates3 = gates.reshape(B, S, GATE_PAD)

    gbias = jnp.pad(jnp.concatenate([b_i[0], b_f[0]]), (0, GATE_PAD - 2 * M_HEADS)).reshape(1, GATE_PAD)
    hm = _mlstm(p3, og.reshape(B, S, M_WIDTH), gates3, m_norm_w, gbias)

    lam_p = jnp.stack([lambda_q1[0], lambda_k1[0], lambda_q2[0], lambda_k2[0]])
    ha = _attn(p3, rel_bias, lam_p, a_norm_w)

    out = _outproj(hm.reshape(B * S, M_WIDTH), ha.reshape(B * S, A_WIDTH), x2d, mod3,
                   w_out[0].astype(BF16), final_norm_w.reshape(1, D), B, S)
    return out.reshape(B, S, D)
```

```python
import functools
import math

import numpy as np
import jax
import jax.numpy as jnp
from jax import lax
from jax.experimental import pallas as pl
from jax.experimental.pallas import tpu as pltpu

F32 = jnp.float32
BF16 = jnp.bfloat16

D_MODEL = 1024
M_WIDTH = 1024
M_HEADS = 4
M_HD = 256
CONV_K = 4
A_WIDTH = 1024
A_HEADS = 8
A_VHD = 128
A_QKD = 64
N_BUCKETS = 32
MAX_DIST = 128
EPS = 1e-6
LAM_INIT = 0.8 - 0.6 * math.exp(-0.3 * 0)

N_MAIN = 5 * M_WIDTH + 4 * A_WIDTH
GATE_COL0 = 5 * M_WIDTH
GATE_PAD = 128

M_CHUNK = 256
M_AUG = M_HD + 128
M_NB = 2
ATT_BLK = 256
ATT_HB = 2
VT_ROWS = A_VHD + 16
LOG2E = math.log2(math.e)
NEG = -1e30

VMEM_LIMIT = 56 * 1024 * 1024


def _nt_dot(a, b):
    return lax.dot_general(a, b, (((1,), (1,)), ((), ())), preferred_element_type=F32)


def _tn_dot(a, b):
    return lax.dot_general(a, b, (((0,), (0,)), ((), ())), preferred_element_type=F32)


def _sigmoid(x):
    return 1.0 / (1.0 + jnp.exp(-x))


def _adaln_kernel(c_ref, w_ref, b_ref, o_ref):
    c = c_ref[...]
    sc = (c * _sigmoid(c)).astype(BF16)
    o_ref[0] = jnp.dot(sc, w_ref[...].astype(BF16), preferred_element_type=F32) + b_ref[0]


def _adaln(c, w_ada, b_ada):
    B, D = c.shape
    return pl.pallas_call(
        _adaln_kernel,
        grid=(3,),
        in_specs=[
            pl.BlockSpec((B, D), lambda j: (0, 0)),
            pl.BlockSpec((D, D), lambda j: (0, j)),
            pl.BlockSpec((1, 1, D), lambda j: (j, 0, 0)),
        ],
        out_specs=pl.BlockSpec((1, B, D), lambda j: (j, 0, 0)),
        out_shape=jax.ShapeDtypeStruct((3, B, D), F32),
        compiler_params=pltpu.CompilerParams(
            dimension_semantics=("arbitrary",), vmem_limit_bytes=VMEM_LIMIT),
        name="adaln",
    )(c, w_ada, b_ada.reshape(3, 1, D))


IN_TN = 1024
IN_TILES = 9
P_TILES = 7
P_COLS = P_TILES * IN_TN
IN_ORDER = (0, 1, 3, 4, 8, 2, 5, 6, 7)
IN_P_SLOT = {0: 0, 1: 1, 2: 2, 5: 3, 6: 4, 7: 5, 8: 6}


def _inproj_kernel(x_ref, shift_ref, scale_ref, nw_ref, wa_ref, wb_ref, wg_ref, cw_ref,
                   p_ref, g_ref, gates_ref, h_scr, raw0, raw1, tail_scr, *, tiles_per_seq):
    i = pl.program_id(0)
    tm = x_ref.shape[0]
    tn = IN_TN
    raws = (raw0, raw1)
    col_scale = {0: M_HD ** -0.5, 5: A_QKD ** -0.5 * LOG2E}

    x = x_ref[...]
    y = x * lax.rsqrt(jnp.mean(x * x, axis=-1, keepdims=True) + EPS)
    h_scr[...] = (y * (nw_ref[...] * (1.0 + scale_ref[0])) + shift_ref[0]).astype(BF16)
    gates_ref[...] = jnp.dot(h_scr[...], wg_ref[...], preferred_element_type=F32)

    @pl.when(i == 0)
    def _():
        tail_scr[...] = jnp.zeros_like(tail_scr)

    first = (i % tiles_per_seq) == 0

    def matmul(j, buf):
        raw = raws[buf]

        def dot_cols(w_ref, c0, width):
            return jnp.dot(h_scr[...], w_ref[:, c0:c0 + width], preferred_element_type=F32)

        if j in (3, 4):
            half = tn // 2
            raw[8:tm + 8, 0:half] = dot_cols(wa_ref, 3 * tn + (j - 3) * half, half)
            raw[8:tm + 8, half:tn] = dot_cols(wa_ref, 4 * tn + (j - 3) * half, half)
        elif j < 3:
            raw[8:tm + 8, :] = dot_cols(wa_ref, j * tn, tn)
        else:
            raw[8:tm + 8, :] = dot_cols(wb_ref, (j - 5) * tn, tn)


    def tail(j, buf):
        raw = raws[buf]
        if j in (3, 4):
            o, z = raw[8:tm + 8, 0:tn // 2], raw[8:tm + 8, tn // 2:tn]
            g_ref[:, (j - 3) * (tn // 2):(j - 2) * (tn // 2)] = (_sigmoid(o) * (z * _sigmoid(z))).astype(BF16)
            return
        if j in (0, 1):
            raw[0:8, :] = jnp.where(first, 0.0, tail_scr[j])
            y = raw[8:tm + 8, :] * cw_ref[j, CONV_K - 1:CONV_K, :]
            for jj in range(CONV_K - 1):
                off = 8 - (CONV_K - 1) + jj
                y = y + raw[off:off + tm, :] * cw_ref[j, jj:jj + 1, :]
            tail_scr[j] = raw[tm:tm + 8, :]
            y = y * _sigmoid(y)
        elif j == 8:
            a = raw[8:tm + 8, :]
            y = a * _sigmoid(a)
        else:
            y = raw[8:tm + 8, :]
        if j in col_scale:
            y = y * col_scale[j]
        slot = IN_P_SLOT[j]
        p_ref[:, slot * tn:(slot + 1) * tn] = y.astype(BF16)

    matmul(IN_ORDER[0], 0)
    for n in range(1, IN_TILES):
        tail(IN_ORDER[n - 1], (n - 1) % 2)
        matmul(IN_ORDER[n], n % 2)
    tail(IN_ORDER[-1], (IN_TILES - 1) % 2)


def _inproj(x2d, mod3, norm_w, wa, wb, w_gate, conv_w, B, S, tm=512):
    M, D = x2d.shape
    tn = IN_TN
    tiles_per_seq = S // tm
    return pl.pallas_call(
        functools.partial(_inproj_kernel, tiles_per_seq=tiles_per_seq),
        grid=(M // tm,),
        in_specs=[
            pl.BlockSpec((tm, D), lambda i: (i, 0)),
            pl.BlockSpec((1, 1, D), lambda i: (i // tiles_per_seq, 0, 0)),
            pl.BlockSpec((1, 1, D), lambda i: (B + i // tiles_per_seq, 0, 0)),
            pl.BlockSpec((1, D), lambda i: (0, 0)),
            pl.BlockSpec((D, 5 * tn), lambda i: (0, 0), pipeline_mode=pl.Buffered(1)),
            pl.BlockSpec((D, 4 * tn), lambda i: (0, 0), pipeline_mode=pl.Buffered(1)),
            pl.BlockSpec((D, GATE_PAD), lambda i: (0, 0)),
            pl.BlockSpec((2, CONV_K, tn), lambda i: (0, 0, 0)),
        ],
        out_specs=[
            pl.BlockSpec((tm, P_COLS), lambda i: (i, 0)),
            pl.BlockSpec((tm, M_WIDTH), lambda i: (i, 0)),
            pl.BlockSpec((tm, GATE_PAD), lambda i: (i, 0)),
        ],
        out_shape=[
            jax.ShapeDtypeStruct((M, P_COLS), BF16),
            jax.ShapeDtypeStruct((M, M_WIDTH), BF16),
            jax.ShapeDtypeStruct((M, GATE_PAD), F32),
        ],
        scratch_shapes=[
            pltpu.VMEM((tm, D), BF16),
            pltpu.VMEM((tm + 8, tn), F32),
            pltpu.VMEM((tm + 8, tn), F32),
            pltpu.VMEM((2, 8, tn), F32),
        ],
        compiler_params=pltpu.CompilerParams(
            dimension_semantics=("arbitrary",), vmem_limit_bytes=VMEM_LIMIT),
        name="inproj",
    )(x2d, mod3, mod3, norm_w, wa, wb, w_gate, conv_w)


def _mlstm_kernel(q_ref, k_ref, v_ref, og_ref, g_ref, nw_ref, gb_ref, o_ref, ct_scr, m_scr):
    L = M_CHUNK
    NB = q_ref.shape[0]
    c = pl.program_id(1)

    @pl.when(c == 0)
    def _():
        ct_scr[...] = jnp.zeros_like(ct_scr)
        m_scr[...] = jnp.zeros_like(m_scr)

    row = lax.broadcasted_iota(jnp.int32, (L, GATE_PAD), 0)

    def gate_math(bb):
        g = g_ref[bb] + gb_ref[...]
        lf = jnp.minimum(g, 0.0) - jnp.log1p(jnp.exp(-jnp.abs(g)))
        bc = lf
        sh = 1
        while sh < L:
            bc = bc + jnp.where(row >= sh, pltpu.roll(bc, sh, 0), 0.0)
            sh *= 2
        bsh = pltpu.roll(bc, GATE_PAD - M_HEADS, 1)
        a = g - bsh
        cm = a
        sh = 1
        while sh < L:
            cm = jnp.maximum(cm, jnp.where(row >= sh, pltpu.roll(cm, sh, 0), -jnp.inf))
            sh *= 2
        m_prev = m_scr[bb]
        mm = jnp.maximum(cm, m_prev)
        iw = jnp.exp(m_prev - mm)
        eneg = jnp.exp(-(bsh + mm))
        m_scr[bb] = bsh[L - 1:L, :] + mm[L - 1:L, :]
        return dict(a=a, a_t=a.T, mm=mm, iw=iw, eneg=eneg)

    gm = [gate_math(bb) for bb in range(NB)]

    tri = (lax.broadcasted_iota(jnp.int32, (L, L), 1) <= lax.broadcasted_iota(jnp.int32, (L, L), 0))
    ones_col = (lax.broadcasted_iota(jnp.int32, (L, M_AUG - M_HD), 1) == 0).astype(BF16)

    chains = [(bb, h) for bb in range(NB) for h in range(M_HEADS)]
    cols = [slice(h * M_HD, (h + 1) * M_HD) for h in range(M_HEADS)]
    qs = {ch: q_ref[ch[0], :, cols[ch[1]]] for ch in chains}
    ks = {ch: k_ref[ch[0], :, cols[ch[1]]] for ch in chains}
    v_aug = {ch: jnp.concatenate([v_ref[ch[0], :, cols[ch[1]]], ones_col], axis=1) for ch in chains}
    cts = {ch: ct_scr[ch[0], ch[1]] for ch in chains}

    s_raw = {ch: _nt_dot(qs[ch], ks[ch]) for ch in chains}
    for ch in chains:
        bb, h = ch
        m_last = gm[bb]["mm"][L - 1:L, h:h + 1]
        w_col = jnp.exp(gm[bb]["a"][:, h:h + 1] - m_last)
        decay = gm[bb]["iw"][L - 1:L, h:h + 1]
        wv = (w_col * v_aug[ch].astype(F32)).astype(BF16)
        ct_scr[bb, h] = decay * cts[ch] + _tn_dot(ks[ch], wv)

    s16 = {}
    for ch in chains:
        bb, h = ch
        dw = jnp.where(tri, jnp.exp(gm[bb]["a_t"][h:h + 1, :] - gm[bb]["mm"][:, h:h + 1]), 0.0)
        s16[ch] = (s_raw[ch] * dw).astype(BF16)
    rs = {}
    for ch in chains:
        bb, h = ch
        q_scaled = gm[bb]["iw"][:, h:h + 1].astype(BF16) * qs[ch]
        lhs = jnp.concatenate([q_scaled, s16[ch]], axis=1)
        rhs = jnp.concatenate([cts[ch].astype(BF16), v_aug[ch]], axis=0)
        rs[ch] = jnp.dot(lhs, rhs, preferred_element_type=F32)

    for ch in chains:
        bb, h = ch
        num = rs[ch][:, 0:M_HD]
        d = jnp.maximum(jnp.abs(rs[ch][:, M_HD:M_HD + 1]), gm[bb]["eneg"][:, h:h + 1])
        hc = num - jnp.mean(num, axis=-1, keepdims=True)
        var = jnp.mean(hc * hc, axis=-1, keepdims=True)
        hn = hc * lax.rsqrt(var + EPS * d * d) * nw_ref[:, cols[h]]
        o_ref[bb, :, cols[h]] = (hn * og_ref[bb, :, cols[h]].astype(F32)).astype(BF16)


def _mlstm(p3, og3, gates3, m_norm_w, gbias):
    B, S, _ = p3.shape
    L = M_CHUNK
    W = M_WIDTH

    NB = M_NB

    def col(cb):
        return pl.BlockSpec((NB, L, W), lambda b, c, cb=cb: (b, c, cb))

    return pl.pallas_call(
        _mlstm_kernel,
        grid=(B // NB, S // L),
        in_specs=[
            col(0), col(1), col(2),
            pl.BlockSpec((NB, L, W), lambda b, c: (b, c, 0)),
            pl.BlockSpec((NB, L, GATE_PAD), lambda b, c: (b, c, 0)),
            pl.BlockSpec((1, W), lambda b, c: (0, 0)),
            pl.BlockSpec((1, GATE_PAD), lambda b, c: (0, 0)),
        ],
        out_specs=pl.BlockSpec((NB, L, W), lambda b, c: (b, c, 0)),
        out_shape=jax.ShapeDtypeStruct((B, S, W), BF16),
        scratch_shapes=[
            pltpu.VMEM((NB, M_HEADS, M_HD, M_AUG), F32),
            pltpu.VMEM((NB, 1, GATE_PAD), F32),
        ],
        compiler_params=pltpu.CompilerParams(
            dimension_semantics=("arbitrary", "arbitrary"), vmem_limit_bytes=VMEM_LIMIT),
        name="mlstm",
    )(p3, p3, p3, og3, gates3, m_norm_w, gbias)


def _bucket_tiles():
    T = ATT_BLK
    kk = np.arange(T)[:, None]
    qq = np.arange(T)[None, :]
    out = []
    for base in (0, T):
        n = base + qq - kk
        max_exact = N_BUCKETS // 2
        nf = np.maximum(n, 1).astype(np.float64)
        large = max_exact + (np.log(nf / max_exact) / math.log(MAX_DIST / max_exact)
                             * (N_BUCKETS - max_exact)).astype(np.int64)
        large = np.minimum(large, N_BUCKETS - 1)
        bk = np.where(n < max_exact, n, large)
        bk = np.where(n < 0, -1, bk)
        out.append(bk)
    return np.stack(out).astype(np.int32)


def _attn_kernel(relb_ref, bucket_ref, lam_ref, anw_ref, q_ref, k_ref, v_ref, z_ref, o_ref,
                 bias_scr, vt_scr, qq_scr, s0_scr, s1_scr, acc_scr):
    T = ATT_BLK
    HB = ATT_HB
    S = k_ref.shape[1]
    NQ = S // T
    hg = pl.program_id(0)
    b = pl.program_id(1)
    heads = [slice(hb * A_VHD, (hb + 1) * A_VHD) for hb in range(HB)]

    @pl.when(b == 0)
    def _():
        for hb in range(HB):
            h = hg * HB + hb
            far = relb_ref[N_BUCKETS - 1, h]
            for t in range(2):
                bk = bucket_ref[t]
                tile = jnp.zeros((T, T), F32)
                for bb in range(N_BUCKETS - 1):
                    tile = jnp.where(bk == bb, (relb_ref[bb, h] - far) * LOG2E, tile)
                tile = jnp.where(bk < 0, NEG, tile)
                bias_scr[hb, t] = jnp.concatenate([tile, tile], axis=1)

    lane = lax.broadcasted_iota(jnp.int32, (T, A_VHD), 1)
    ones_rows = (lax.broadcasted_iota(jnp.int32, (VT_ROWS - A_VHD, T), 0) == 0).astype(BF16)
    for hb in range(HB):
        for ci in range(NQ):
            rows = slice(ci * T, (ci + 1) * T)
            vt_scr[hb, 0:A_VHD, rows] = v_ref[0, rows, heads[hb]].astype(F32).T.astype(BF16)
            vt_scr[hb, A_VHD:VT_ROWS, rows] = ones_rows
            qs = q_ref[0, rows, heads[hb]]
            zero = jnp.zeros_like(qs)
            qq_scr[hb, ci, 0:T, :] = jnp.where(lane < A_QKD, qs, zero)
            qq_scr[hb, ci, T:2 * T, :] = jnp.where(lane >= A_QKD, qs, zero)
    acc_scr[...] = jnp.zeros_like(acc_scr)

    s_bufs = (s0_scr, s1_scr)

    def issue(buf, qb, kj, hb):
        s_bufs[buf][hb] = _nt_dot(k_ref[0, kj * T:(kj + 1) * T, heads[hb]], qq_scr[hb, qb])

    def run_stage(cur, nxt, kj, bias_idx, ms):
        start = kj * T
        out = []
        for hb in range(HB):
            s = s_bufs[cur][hb]
            if bias_idx is not None:
                s = s + bias_scr[hb, bias_idx]
            m_old = ms[hb]
            m_new = jnp.maximum(m_old, jnp.max(s, axis=0, keepdims=True))
            alpha = jnp.exp2(m_old - m_new)
            p = jnp.exp2(s - m_new).astype(BF16)
            if nxt is not None:
                issue(1 - cur, nxt[0], nxt[1], hb)
            pv = jnp.dot(vt_scr[hb, :, start:start + T], p, preferred_element_type=F32)
            acc_scr[hb] = alpha * acc_scr[hb] + pv
            out.append(m_new)
        return tuple(out)

    lam_p = lam_ref[...]
    lam = (jnp.exp(jnp.sum(lam_p[0:1] * lam_p[1:2], axis=-1, keepdims=True))
           - jnp.exp(jnp.sum(lam_p[2:3] * lam_p[3:4], axis=-1, keepdims=True)) + LAM_INIT)

    def finalize(qi):
        rows = pl.ds(pl.multiple_of(qi * T, T), T)
        for hb in range(HB):
            acc = acc_scr[hb]
            o = acc[0:A_VHD] / acc[A_VHD:A_VHD + 1]
            d = (o[:, 0:T] - lam * o[:, T:2 * T]).T
            y = d * lax.rsqrt(jnp.mean(d * d, axis=-1, keepdims=True) + EPS) * anw_ref[...] * (1.0 - LAM_INIT)
            o_ref[0, rows, heads[hb]] = (y * z_ref[0, rows, heads[hb]].astype(F32)).astype(BF16)

    m_init = tuple(jnp.full((1, 2 * T), NEG, F32) for _ in range(HB))

    pairs = [(qi, kj) for qi in range(NQ) for kj in range(qi + 1)]
    for hb in range(HB):
        issue(0, 0, 0, hb)
    ms = m_init
    for t, (qi, kj) in enumerate(pairs):
        nxt = pairs[t + 1] if t + 1 < len(pairs) else None
        bias_idx = 0 if kj == qi else (1 if kj == qi - 1 else None)
        ms = run_stage(t % 2, nxt, kj, bias_idx, ms)
        if kj == qi:
            finalize(qi)
            ms = m_init


def _attn(proj3, rel_bias, lam_p, a_norm_w):
    B, S, _ = proj3.shape
    T = ATT_BLK
    HB = ATT_HB
    W = HB * A_VHD
    q0 = 3 * IN_TN // W
    k0 = q0 + A_HEADS // HB
    v0 = k0 + A_HEADS // HB
    z0 = v0 + A_HEADS // HB
    buckets = jnp.asarray(_bucket_tiles())
    return pl.pallas_call(
        _attn_kernel,
        grid=(A_HEADS // HB, B),
        in_specs=[
            pl.BlockSpec(memory_space=pltpu.SMEM),
            pl.BlockSpec((2, T, T), lambda h, b: (0, 0, 0)),
            pl.BlockSpec((4, A_QKD), lambda h, b: (0, 0)),
            pl.BlockSpec((1, A_VHD), lambda h, b: (0, 0)),
            pl.BlockSpec((1, S, W), lambda h, b: (b, 0, q0 + h)),
            pl.BlockSpec((1, S, W), lambda h, b: (b, 0, k0 + h)),
            pl.BlockSpec((1, S, W), lambda h, b: (b, 0, v0 + h)),
            pl.BlockSpec((1, S, W), lambda h, b: (b, 0, z0 + h)),
        ],
        out_specs=pl.BlockSpec((1, S, W), lambda h, b: (b, 0, h)),
        out_shape=jax.ShapeDtypeStruct((B, S, A_WIDTH), BF16),
        scratch_shapes=[
            pltpu.VMEM((HB, 2, T, 2 * T), F32),
            pltpu.VMEM((HB, VT_ROWS, S), BF16),
            pltpu.VMEM((HB, S // T, 2 * T, A_VHD), BF16),
            pltpu.VMEM((HB, T, 2 * T), F32),
            pltpu.VMEM((HB, T, 2 * T), F32),
            pltpu.VMEM((HB, VT_ROWS, 2 * T), F32),
        ],
        compiler_params=pltpu.CompilerParams(
            dimension_semantics=("arbitrary", "arbitrary"), vmem_limit_bytes=VMEM_LIMIT),
        name="attn",
    )(rel_bias, buckets, lam_p, a_norm_w, proj3, proj3, proj3, proj3)


def _outproj_kernel(hm_ref, ha_ref, x_ref, gate_ref, w_ref, fw_ref, o_ref, raw0, raw1):
    sub = raw0.shape[0]
    nsub = x_ref.shape[0] // sub
    raws = (raw0, raw1)

    def matmul(r, buf):
        rows = slice(r * sub, (r + 1) * sub)
        y = jnp.dot(hm_ref[rows, :], w_ref[0:M_WIDTH, :], preferred_element_type=F32)
        raws[buf][...] = y + jnp.dot(ha_ref[rows, :], w_ref[M_WIDTH:, :], preferred_element_type=F32)

    def tail(r, buf):
        rows = slice(r * sub, (r + 1) * sub)
        res = x_ref[rows, :] + gate_ref[0] * raws[buf][...]
        o_ref[rows, :] = res * lax.rsqrt(jnp.mean(res * res, axis=-1, keepdims=True) + EPS) * fw_ref[...]

    matmul(0, 0)
    for r in range(1, nsub):
        tail(r - 1, (r - 1) % 2)
        matmul(r, r % 2)
    tail(nsub - 1, (nsub - 1) % 2)


def _outproj(hm2d, ha2d, x2d, mod3, w_out, final_w, B, S, tm=1024, sub=512):
    M, D = x2d.shape
    tiles_per_seq = S // tm
    return pl.pallas_call(
        _outproj_kernel,
        grid=(M // tm,),
        scratch_shapes=[pltpu.VMEM((sub, D), F32), pltpu.VMEM((sub, D), F32)],
        in_specs=[
            pl.BlockSpec((tm, M_WIDTH), lambda i: (i, 0)),
            pl.BlockSpec((tm, A_WIDTH), lambda i: (i, 0)),
            pl.BlockSpec((tm, D), lambda i: (i, 0)),
            pl.BlockSpec((1, 1, D), lambda i: (2 * B + i // tiles_per_seq, 0, 0)),
            pl.BlockSpec((M_WIDTH + A_WIDTH, D), lambda i: (0, 0)),
            pl.BlockSpec((1, D), lambda i: (0, 0)),
        ],
        out_specs=pl.BlockSpec((tm, D), lambda i: (i, 0)),
        out_shape=jax.ShapeDtypeStruct((M, D), F32),
        compiler_params=pltpu.CompilerParams(
            dimension_semantics=("arbitrary",), vmem_limit_bytes=VMEM_LIMIT),
        name="outproj",
    )(hm2d, ha2d, x2d, mod3, w_out, final_w)


def kernel(x, c, norm_w, w_ada, b_ada, w_in, b_i, b_f, conv_q_w, conv_k_w, m_norm_w,
           lambda_q1, lambda_k1, lambda_q2, lambda_k2, a_norm_w, rel_bias, w_out, final_norm_w):
    B, S, D = x.shape
    assert norm_w.shape[0] == 1, "single layer only"
    x2d = x.reshape(B * S, D)

    mod = _adaln(c, w_ada[0], b_ada[0])
    mod3 = mod.reshape(3 * B, 1, D)

    w_in0 = w_in[0]
    a0 = GATE_COL0 + 2 * M_HEADS
    wa = w_in0[:, :GATE_COL0].astype(BF16)
    wb = w_in0[:, a0:].astype(BF16)
    w_gate = jnp.pad(w_in0[:, GATE_COL0:a0], ((0, 0), (0, GATE_PAD - 2 * M_HEADS))).astype(BF16)
    conv_w = jnp.stack([conv_q_w[0], conv_k_w[0]])
    p, og, gates = _inproj(x2d, mod3, norm_w, wa, wb, w_gate, conv_w, B, S)
    p3 = p.reshape(B, S, P_COLS)
    gates3 = gates.reshape(B, S, GATE_PAD)

    gbias = jnp.pad(jnp.concatenate([b_i[0], b_f[0]]), (0, GATE_PAD - 2 * M_HEADS)).reshape(1, GATE_PAD)
    hm = _mlstm(p3, og.reshape(B, S, M_WIDTH), gates3, m_norm_w, gbias)

    lam_p = jnp.stack([lambda_q1[0], lambda_k1[0], lambda_q2[0], lambda_k2[0]])
    ha = _attn(p3, rel_bias, lam_p, a_norm_w)

    out = _outproj(hm.reshape(B * S, M_WIDTH), ha.reshape(B * S, A_WIDTH), x2d, mod3,
                   w_out[0].astype(BF16), final_norm_w.reshape(1, D), B, S)
    return out.reshape(B, S, D)
```

```python
import functools
import math

import numpy as np
import jax
import jax.numpy as jnp
from jax import lax
from jax.experimental import pallas as pl
from jax.experimental.pallas import tpu as pltpu

F32 = jnp.float32
BF16 = jnp.bfloat16

D_MODEL = 1024
M_WIDTH = 1024
M_HEADS = 4
M_HD = 256
CONV_K = 4
A_WIDTH = 1024
A_HEADS = 8
A_VHD = 128
A_QKD = 64
N_BUCKETS = 32
MAX_DIST = 128
EPS = 1e-6
LAM_INIT = 0.8 - 0.6 * math.exp(-0.3 * 0)

N_MAIN = 5 * M_WIDTH + 4 * A_WIDTH
GATE_COL0 = 5 * M_WIDTH
GATE_PAD = 128

M_CHUNK = 256
M_AUG = M_HD + 128
M_NB = 2
ATT_BLK = 256
ATT_HB = 2
VT_ROWS = A_VHD + 16
LOG2E = math.log2(math.e)
NEG = -1e30

VMEM_LIMIT = 56 * 1024 * 1024


def _nt_dot(a, b):
    return lax.dot_general(a, b, (((1,), (1,)), ((), ())), preferred_element_type=F32)


def _tn_dot(a, b):
    return lax.dot_general(a, b, (((0,), (0,)), ((), ())), preferred_element_type=F32)


def _sigmoid(x):
    return 1.0 / (1.0 + jnp.exp(-x))


def _adaln_kernel(c_ref, w_ref, b_ref, o_ref):
    c = c_ref[...]
    sc = (c * _sigmoid(c)).astype(BF16)
    o_ref[0] = jnp.dot(sc, w_ref[...].astype(BF16), preferred_element_type=F32) + b_ref[0]


def _adaln(c, w_ada, b_ada):
    B, D = c.shape
    return pl.pallas_call(
        _adaln_kernel,
        grid=(3,),
        in_specs=[
            pl.BlockSpec((B, D), lambda j: (0, 0)),
            pl.BlockSpec((D, D), lambda j: (0, j)),
            pl.BlockSpec((1, 1, D), lambda j: (j, 0, 0)),
        ],
        out_specs=pl.BlockSpec((1, B, D), lambda j: (j, 0, 0)),
        out_shape=jax.ShapeDtypeStruct((3, B, D), F32),
        compiler_params=pltpu.CompilerParams(
            dimension_semantics=("arbitrary",), vmem_limit_bytes=VMEM_LIMIT),
        name="adaln",
    )(c, w_ada, b_ada.reshape(3, 1, D))


IN_TN = 1024
IN_TILES = 9
P_TILES = 7
P_COLS = P_TILES * IN_TN
IN_ORDER = (0, 1, 3, 4, 8, 2, 5, 6, 7)
IN_P_SLOT = {0: 0, 1: 1, 2: 2, 5: 3, 6: 4, 7: 5, 8: 6}


def _inproj_kernel(x_ref, xn_ref, shift_ref, scale_ref, shiftn_ref, scalen_ref, nw_ref,
                   wa_ref, wb_ref, wg_ref, cw_ref,
                   p_ref, g_ref, gates_ref, h_scr, hn_scr, gn_scr, raw0, raw1, tail_scr, *, tiles_per_seq):
    i = pl.program_id(0)
    tm = x_ref.shape[0]
    tn = IN_TN
    raws = (raw0, raw1)
    col_scale = {0: M_HD ** -0.5, 5: A_QKD ** -0.5 * LOG2E}

    def normed(xr, shr, scr):
        x = xr[...]
        y = x * lax.rsqrt(jnp.mean(x * x, axis=-1, keepdims=True) + EPS)
        return (y * (nw_ref[...] * (1.0 + scr[0])) + shr[0]).astype(BF16)

    @pl.when(i == 0)
    def _():
        h = normed(x_ref, shift_ref, scale_ref)
        h_scr[...] = h
        gates_ref[...] = jnp.dot(h, wg_ref[...], preferred_element_type=F32)
        tail_scr[...] = jnp.zeros_like(tail_scr)

    @pl.when(i > 0)
    def _():
        h_scr[...] = hn_scr[...]
        gates_ref[...] = gn_scr[...]

    first = (i % tiles_per_seq) == 0

    def matmul(j, buf):
        raw = raws[buf]

        def dot_cols(w_ref, c0, width):
            return jnp.dot(h_scr[...], w_ref[:, c0:c0 + width], preferred_element_type=F32)

        if j in (3, 4):
            half = tn // 2
            raw[8:tm + 8, 0:half] = dot_cols(wa_ref, 3 * tn + (j - 3) * half, half)
            raw[8:tm + 8, half:tn] = dot_cols(wa_ref, 4 * tn + (j - 3) * half, half)
        elif j < 3:
            raw[8:tm + 8, :] = dot_cols(wa_ref, j * tn, tn)
        else:
            raw[8:tm + 8, :] = dot_cols(wb_ref, (j - 5) * tn, tn)


    def tail(j, buf):
        raw = raws[buf]
        if j in (3, 4):
            o, z = raw[8:tm + 8, 0:tn // 2], raw[8:tm + 8, tn // 2:tn]
            g_ref[:, (j - 3) * (tn // 2):(j - 2) * (tn // 2)] = (_sigmoid(o) * (z * _sigmoid(z))).astype(BF16)
            return
        if j in (0, 1):
            raw[0:8, :] = jnp.where(first, 0.0, tail_scr[j])
            y = raw[8:tm + 8, :] * cw_ref[j, CONV_K - 1:CONV_K, :]
            for jj in range(CONV_K - 1):
                off = 8 - (CONV_K - 1) + jj
                y = y + raw[off:off + tm, :] * cw_ref[j, jj:jj + 1, :]
            tail_scr[j] = raw[tm:tm + 8, :]
            y = y * _sigmoid(y)
        elif j == 8:
            a = raw[8:tm + 8, :]
            y = a * _sigmoid(a)
        else:
            y = raw[8:tm + 8, :]
        if j in col_scale:
            y = y * col_scale[j]
        slot = IN_P_SLOT[j]
        p_ref[:, slot * tn:(slot + 1) * tn] = y.astype(BF16)

    matmul(IN_ORDER[0], 0)
    for n in range(1, IN_TILES):
        tail(IN_ORDER[n - 1], (n - 1) % 2)
        matmul(IN_ORDER[n], n % 2)
    tail(IN_ORDER[-1], (IN_TILES - 1) % 2)
    hn = normed(xn_ref, shiftn_ref, scalen_ref)
    hn_scr[...] = hn
    gn_scr[...] = jnp.dot(hn, wg_ref[...], preferred_element_type=F32)


def _inproj(x2d, mod3, norm_w, wa, wb, w_gate, conv_w, B, S, tm=512):
    M, D = x2d.shape
    tn = IN_TN
    tiles_per_seq = S // tm
    last = M // tm - 1

    def nxt(i):
        return jnp.minimum(i + 1, last)

    return pl.pallas_call(
        functools.partial(_inproj_kernel, tiles_per_seq=tiles_per_seq),
        grid=(M // tm,),
        in_specs=[
            pl.BlockSpec((tm, D), lambda i: (i, 0)),
            pl.BlockSpec((tm, D), lambda i: (nxt(i), 0)),
            pl.BlockSpec((1, 1, D), lambda i: (i // tiles_per_seq, 0, 0)),
            pl.BlockSpec((1, 1, D), lambda i: (B + i // tiles_per_seq, 0, 0)),
            pl.BlockSpec((1, 1, D), lambda i: (nxt(i) // tiles_per_seq, 0, 0)),
            pl.BlockSpec((1, 1, D), lambda i: (B + nxt(i) // tiles_per_seq, 0, 0)),
            pl.BlockSpec((1, D), lambda i: (0, 0)),
            pl.BlockSpec((D, 5 * tn), lambda i: (0, 0), pipeline_mode=pl.Buffered(1)),
            pl.BlockSpec((D, 4 * tn), lambda i: (0, 0), pipeline_mode=pl.Buffered(1)),
            pl.BlockSpec((D, GATE_PAD), lambda i: (0, 0)),
            pl.BlockSpec((2, CONV_K, tn), lambda i: (0, 0, 0)),
        ],
        out_specs=[
            pl.BlockSpec((tm, P_COLS), lambda i: (i, 0)),
            pl.BlockSpec((tm, M_WIDTH), lambda i: (i, 0)),
            pl.BlockSpec((tm, GATE_PAD), lambda i: (i, 0)),
        ],
        out_shape=[
            jax.ShapeDtypeStruct((M, P_COLS), BF16),
            jax.ShapeDtypeStruct((M, M_WIDTH), BF16),
            jax.ShapeDtypeStruct((M, GATE_PAD), F32),
        ],
        scratch_shapes=[
            pltpu.VMEM((tm, D), BF16),
            pltpu.VMEM((tm, D), BF16),
            pltpu.VMEM((tm, GATE_PAD), F32),
            pltpu.VMEM((tm + 8, tn), F32),
            pltpu.VMEM((tm + 8, tn), F32),
            pltpu.VMEM((2, 8, tn), F32),
        ],
        compiler_params=pltpu.CompilerParams(
            dimension_semantics=("arbitrary",), vmem_limit_bytes=VMEM_LIMIT),
        name="inproj",
    )(x2d, x2d, mod3, mod3, mod3, mod3, norm_w, wa, wb, w_gate, conv_w)


def _mlstm_kernel(q_ref, k_ref, v_ref, og_ref, g_ref, nw_ref, gb_ref, o_ref, ct_scr, m_scr):
    L = M_CHUNK
    NB = q_ref.shape[0]
    c = pl.program_id(1)

    @pl.when(c == 0)
    def _():
        ct_scr[...] = jnp.zeros_like(ct_scr)
        m_scr[...] = jnp.zeros_like(m_scr)

    row = lax.broadcasted_iota(jnp.int32, (L, GATE_PAD), 0)

    def gate_math(bb):
        g = g_ref[bb] + gb_ref[...]
        lf = jnp.minimum(g, 0.0) - jnp.log1p(jnp.exp(-jnp.abs(g)))
        bc = lf
        sh = 1
        while sh < L:
            bc = bc + jnp.where(row >= sh, pltpu.roll(bc, sh, 0), 0.0)
            sh *= 2
        bsh = pltpu.roll(bc, GATE_PAD - M_HEADS, 1)
        a = g - bsh
        cm = a
        sh = 1
        while sh < L:
            cm = jnp.maximum(cm, jnp.where(row >= sh, pltpu.roll(cm, sh, 0), -jnp.inf))
            sh *= 2
        m_prev = m_scr[bb]
        mm = jnp.maximum(cm, m_prev)
        iw = jnp.exp(m_prev - mm)
        eneg = jnp.exp(-(bsh + mm))
        m_scr[bb] = bsh[L - 1:L, :] + mm[L - 1:L, :]
        return dict(a=a, a_t=a.T, mm=mm, iw=iw, eneg=eneg)

    gm = [gate_math(bb) for bb in range(NB)]

    tri = (lax.broadcasted_iota(jnp.int32, (L, L), 1) <= lax.broadcasted_iota(jnp.int32, (L, L), 0))
    ones_col = (lax.broadcasted_iota(jnp.int32, (L, M_AUG - M_HD), 1) == 0).astype(BF16)

    chains = [(bb, h) for bb in range(NB) for h in range(M_HEADS)]
    cols = [slice(h * M_HD, (h + 1) * M_HD) for h in range(M_HEADS)]
    qs = {ch: q_ref[ch[0], :, cols[ch[1]]] for ch in chains}
    ks = {ch: k_ref[ch[0], :, cols[ch[1]]] for ch in chains}
    v_aug = {ch: jnp.concatenate([v_ref[ch[0], :, cols[ch[1]]], ones_col], axis=1) for ch in chains}
    cts = {ch: ct_scr[ch[0], ch[1]] for ch in chains}

    s_raw = {ch: _nt_dot(qs[ch], ks[ch]) for ch in chains}
    r_state = {ch: jnp.dot(qs[ch], cts[ch].astype(BF16), preferred_element_type=F32) for ch in chains}
    for ch in chains:
        bb, h = ch
        m_last = gm[bb]["mm"][L - 1:L, h:h + 1]
        w_row = jnp.exp(gm[bb]["a_t"][h:h + 1, :] - m_last)
        decay = gm[bb]["iw"][L - 1:L, h:h + 1]
        ktw = (ks[ch].astype(F32).T * w_row).astype(BF16)
        ct_scr[bb, h] = decay * cts[ch] + jnp.dot(ktw, v_aug[ch], preferred_element_type=F32)

    s16 = {}
    for ch in chains:
        bb, h = ch
        dw = jnp.where(tri, jnp.exp(gm[bb]["a_t"][h:h + 1, :] - gm[bb]["mm"][:, h:h + 1]), 0.0)
        s16[ch] = (s_raw[ch] * dw).astype(BF16)
    rs = {ch: gm[ch[0]]["iw"][:, ch[1]:ch[1] + 1] * r_state[ch]
          + jnp.dot(s16[ch], v_aug[ch], preferred_element_type=F32) for ch in chains}

    for ch in chains:
        bb, h = ch
        num = rs[ch][:, 0:M_HD]
        d = jnp.maximum(jnp.abs(rs[ch][:, M_HD:M_HD + 1]), gm[bb]["eneg"][:, h:h + 1])
        hc = num - jnp.mean(num, axis=-1, keepdims=True)
        var = jnp.mean(hc * hc, axis=-1, keepdims=True)
        hn = hc * lax.rsqrt(var + EPS * d * d) * nw_ref[:, cols[h]]
        o_ref[bb, :, cols[h]] = (hn * og_ref[bb, :, cols[h]].astype(F32)).astype(BF16)


def _mlstm(p3, og3, gates3, m_norm_w, gbias):
    B, S, _ = p3.shape
    L = M_CHUNK
    W = M_WIDTH

    NB = M_NB

    def col(cb):
        return pl.BlockSpec((NB, L, W), lambda b, c, cb=cb: (b, c, cb))

    return pl.pallas_call(
        _mlstm_kernel,
        grid=(B // NB, S // L),
        in_specs=[
            col(0), col(1), col(2),
            pl.BlockSpec((NB, L, W), lambda b, c: (b, c, 0)),
            pl.BlockSpec((NB, L, GATE_PAD), lambda b, c: (b, c, 0)),
            pl.BlockSpec((1, W), lambda b, c: (0, 0)),
            pl.BlockSpec((1, GATE_PAD), lambda b, c: (0, 0)),
        ],
        out_specs=pl.BlockSpec((NB, L, W), lambda b, c: (b, c, 0)),
        out_shape=jax.ShapeDtypeStruct((B, S, W), BF16),
        scratch_shapes=[
            pltpu.VMEM((NB, M_HEADS, M_HD, M_AUG), F32),
            pltpu.VMEM((NB, 1, GATE_PAD), F32),
        ],
        compiler_params=pltpu.CompilerParams(
            dimension_semantics=("arbitrary", "arbitrary"), vmem_limit_bytes=VMEM_LIMIT),
        name="mlstm",
    )(p3, p3, p3, og3, gates3, m_norm_w, gbias)


def _bucket_tiles():
    T = ATT_BLK
    kk = np.arange(T)[:, None]
    qq = np.arange(T)[None, :]
    out = []
    for base in (0, T):
        n = base + qq - kk
        max_exact = N_BUCKETS // 2
        nf = np.maximum(n, 1).astype(np.float64)
        large = max_exact + (np.log(nf / max_exact) / math.log(MAX_DIST / max_exact)
                             * (N_BUCKETS - max_exact)).astype(np.int64)
        large = np.minimum(large, N_BUCKETS - 1)
        bk = np.where(n < max_exact, n, large)
        bk = np.where(n < 0, -1, bk)
        out.append(bk)
    return np.stack(out).astype(np.int32)


def _attn_kernel(relb_ref, bucket_ref, lam_ref, anw_ref, q_ref, k_ref, v_ref, z_ref, o_ref,
                 bias_scr, vt_scr, qq_scr, s0_scr, s1_scr, acc_scr):
    T = ATT_BLK
    HB = ATT_HB
    S = k_ref.shape[1]
    NQ = S // T
    hg = pl.program_id(0)
    b = pl.program_id(1)
    heads = [slice(hb * A_VHD, (hb + 1) * A_VHD) for hb in range(HB)]

    @pl.when(b == 0)
    def _():
        for hb in range(HB):
            h = hg * HB + hb
            far = relb_ref[N_BUCKETS - 1, h]
            for t in range(2):
                bk = bucket_ref[t]
                tile = jnp.zeros((T, T), F32)
                for bb in range(N_BUCKETS - 1):
                    tile = jnp.where(bk == bb, (relb_ref[bb, h] - far) * LOG2E, tile)
                tile = jnp.where(bk < 0, NEG, tile)
                bias_scr[hb, t] = jnp.concatenate([tile, tile], axis=1)

    lane = lax.broadcasted_iota(jnp.int32, (T, A_VHD), 1)
    ones_rows = (lax.broadcasted_iota(jnp.int32, (VT_ROWS - A_VHD, T), 0) == 0).astype(BF16)
    for hb in range(HB):
        for ci in range(NQ):
            rows = slice(ci * T, (ci + 1) * T)
            vt_scr[hb, 0:A_VHD, rows] = v_ref[0, rows, heads[hb]].astype(F32).T.astype(BF16)
            vt_scr[hb, A_VHD:VT_ROWS, rows] = ones_rows
            qs = q_ref[0, rows, heads[hb]]
            zero = jnp.zeros_like(qs)
            qq_scr[hb, ci, 0:T, :] = jnp.where(lane < A_QKD, qs, zero)
            qq_scr[hb, ci, T:2 * T, :] = jnp.where(lane >= A_QKD, qs, zero)
    acc_scr[...] = jnp.zeros_like(acc_scr)

    s_bufs = (s0_scr, s1_scr)

    def issue(buf, qb, kj, hb):
        s_bufs[buf][hb] = _nt_dot(k_ref[0, kj * T:(kj + 1) * T, heads[hb]], qq_scr[hb, qb])

    def run_stage(cur, nxt, kj, bias_idx, ms):
        start = kj * T
        out = []
        for hb in range(HB):
            s = s_bufs[cur][hb]
            if bias_idx is not None:
                s = s + bias_scr[hb, bias_idx]
            m_old = ms[hb]
            m_new = jnp.maximum(m_old, jnp.max(s, axis=0, keepdims=True))
            alpha = jnp.exp2(m_old - m_new)
            p = jnp.exp2(s - m_new).astype(BF16)
            if nxt is not None:
                issue(1 - cur, nxt[0], nxt[1], hb)
            pv = jnp.dot(vt_scr[hb, :, start:start + T], p, preferred_element_type=F32)
            acc_scr[hb] = alpha * acc_scr[hb] + pv
            out.append(m_new)
        return tuple(out)

    lam_p = lam_ref[...]
    lam = (jnp.exp(jnp.sum(lam_p[0:1] * lam_p[1:2], axis=-1, keepdims=True))
           - jnp.exp(jnp.sum(lam_p[2:3] * lam_p[3:4], axis=-1, keepdims=True)) + LAM_INIT)

    def finalize(qi):
        rows = pl.ds(pl.multiple_of(qi * T, T), T)
        for hb in range(HB):
            acc = acc_scr[hb]
            o = acc[0:A_VHD] / acc[A_VHD:A_VHD + 1]
            d = (o[:, 0:T] - lam * o[:, T:2 * T]).T
            y = d * lax.rsqrt(jnp.mean(d * d, axis=-1, keepdims=True) + EPS) * anw_ref[...] * (1.0 - LAM_INIT)
            o_ref[0, rows, heads[hb]] = (y * z_ref[0, rows, heads[hb]].astype(F32)).astype(BF16)

    m_init = tuple(jnp.full((1, 2 * T), NEG, F32) for _ in range(HB))

    pairs = [(qi, kj) for qi in range(NQ) for kj in range(qi + 1)]
    for hb in range(HB):
        issue(0, 0, 0, hb)
    ms = m_init
    for t, (qi, kj) in enumerate(pairs):
        nxt = pairs[t + 1] if t + 1 < len(pairs) else None
        bias_idx = 0 if kj == qi else (1 if kj == qi - 1 else None)
        ms = run_stage(t % 2, nxt, kj, bias_idx, ms)
        if kj == qi:
            finalize(qi)
            ms = m_init


def _attn(proj3, rel_bias, lam_p, a_norm_w):
    B, S, _ = proj3.shape
    T = ATT_BLK
    HB = ATT_HB
    W = HB * A_VHD
    q0 = 3 * IN_TN // W
    k0 = q0 + A_HEADS // HB
    v0 = k0 + A_HEADS // HB
    z0 = v0 + A_HEADS // HB
    buckets = jnp.asarray(_bucket_tiles())
    return pl.pallas_call(
        _attn_kernel,
        grid=(A_HEADS // HB, B),
        in_specs=[
            pl.BlockSpec(memory_space=pltpu.SMEM),
            pl.BlockSpec((2, T, T), lambda h, b: (0, 0, 0)),
            pl.BlockSpec((4, A_QKD), lambda h, b: (0, 0)),
            pl.BlockSpec((1, A_VHD), lambda h, b: (0, 0)),
            pl.BlockSpec((1, S, W), lambda h, b: (b, 0, q0 + h)),
            pl.BlockSpec((1, S, W), lambda h, b: (b, 0, k0 + h)),
            pl.BlockSpec((1, S, W), lambda h, b: (b, 0, v0 + h)),
            pl.BlockSpec((1, S, W), lambda h, b: (b, 0, z0 + h)),
        ],
        out_specs=pl.BlockSpec((1, S, W), lambda h, b: (b, 0, h)),
        out_shape=jax.ShapeDtypeStruct((B, S, A_WIDTH), BF16),
        scratch_shapes=[
            pltpu.VMEM((HB, 2, T, 2 * T), F32),
            pltpu.VMEM((HB, VT_ROWS, S), BF16),
            pltpu.VMEM((HB, S // T, 2 * T, A_VHD), BF16),
            pltpu.VMEM((HB, T, 2 * T), F32),
            pltpu.VMEM((HB, T, 2 * T), F32),
            pltpu.VMEM((HB, VT_ROWS, 2 * T), F32),
        ],
        compiler_params=pltpu.CompilerParams(
            dimension_semantics=("arbitrary", "arbitrary"), vmem_limit_bytes=VMEM_LIMIT),
        name="attn",
    )(rel_bias, buckets, lam_p, a_norm_w, proj3, proj3, proj3, proj3)


def _outproj_kernel(hm_ref, ha_ref, x_ref, gate_ref, w_ref, fw_ref, o_ref, raw0, raw1):
    sub = raw0.shape[0]
    nsub = x_ref.shape[0] // sub
    raws = (raw0, raw1)

    def matmul(r, buf):
        rows = slice(r * sub, (r + 1) * sub)
        y = jnp.dot(hm_ref[rows, :], w_ref[0:M_WIDTH, :], preferred_element_type=F32)
        raws[buf][...] = y + jnp.dot(ha_ref[rows, :], w_ref[M_WIDTH:, :], preferred_element_type=F32)

    def tail(r, buf):
        rows = slice(r * sub, (r + 1) * sub)
        res = x_ref[rows, :] + gate_ref[0] * raws[buf][...]
        o_ref[rows, :] = res * lax.rsqrt(jnp.mean(res * res, axis=-1, keepdims=True) + EPS) * fw_ref[...]

    matmul(0, 0)
    for r in range(1, nsub):
        tail(r - 1, (r - 1) % 2)
        matmul(r, r % 2)
    tail(nsub - 1, (nsub - 1) % 2)


def _outproj(hm2d, ha2d, x2d, mod3, w_out, final_w, B, S, tm=1024, sub=512):
    M, D = x2d.shape
    tiles_per_seq = S // tm
    return pl.pallas_call(
        _outproj_kernel,
        grid=(M // tm,),
        scratch_shapes=[pltpu.VMEM((sub, D), F32), pltpu.VMEM((sub, D), F32)],
        in_specs=[
            pl.BlockSpec((tm, M_WIDTH), lambda i: (i, 0)),
            pl.BlockSpec((tm, A_WIDTH), lambda i: (i, 0)),
            pl.BlockSpec((tm, D), lambda i: (i, 0)),
            pl.BlockSpec((1, 1, D), lambda i: (2 * B + i // tiles_per_seq, 0, 0)),
            pl.BlockSpec((M_WIDTH + A_WIDTH, D), lambda i: (0, 0)),
            pl.BlockSpec((1, D), lambda i: (0, 0)),
        ],
        out_specs=pl.BlockSpec((tm, D), lambda i: (i, 0)),
        out_shape=jax.ShapeDtypeStruct((M, D), F32),
        compiler_params=pltpu.CompilerParams(
            dimension_semantics=("arbitrary",), vmem_limit_bytes=VMEM_LIMIT),
        name="outproj",
    )(hm2d, ha2d, x2d, mod3, w_out, final_w)


def kernel(x, c, norm_w, w_ada, b_ada, w_in, b_i, b_f, conv_q_w, conv_k_w, m_norm_w,
           lambda_q1, lambda_k1, lambda_q2, lambda_k2, a_norm_w, rel_bias, w_out, final_norm_w):
    B, S, D = x.shape
    assert norm_w.shape[0] == 1, "single layer only"
    x2d = x.reshape(B * S, D)

    mod = _adaln(c, w_ada[0], b_ada[0])
    mod3 = mod.reshape(3 * B, 1, D)

    w_in0 = w_in[0]
    a0 = GATE_COL0 + 2 * M_HEADS
    wa = w_in0[:, :GATE_COL0].astype(BF16)
    wb = w_in0[:, a0:].astype(BF16)
    w_gate = jnp.pad(w_in0[:, GATE_COL0:a0], ((0, 0), (0, GATE_PAD - 2 * M_HEADS))).astype(BF16)
    conv_w = jnp.stack([conv_q_w[0], conv_k_w[0]])
    p, og, gates = _inproj(x2d, mod3, norm_w, wa, wb, w_gate, conv_w, B, S)
    p3 = p.reshape(B, S, P_COLS)
    gates3 = gates.reshape(B, S, GATE_PAD)

    gbias = jnp.pad(jnp.concatenate([b_i[0], b_f[0]]), (0, GATE_PAD - 2 * M_HEADS)).reshape(1, GATE_PAD)
    hm = _mlstm(p3, og.reshape(B, S, M_WIDTH), gates3, m_norm_w, gbias)

    lam_p = jnp.stack([lambda_q1[0], lambda_k1[0], lambda_q2[0], lambda_k2[0]])
    ha = _attn(p3, rel_bias, lam_p, a_norm_w)

    out = _outproj(hm.reshape(B * S, M_WIDTH), ha.reshape(B * S, A_WIDTH), x2d, mod3,
                   w_out[0].astype(BF16), final_norm_w.reshape(1, D), B, S)
    return out.reshape(B, S, D)
```

```python
import functools
import math

import numpy as np
import jax
import jax.numpy as jnp
from jax import lax
from jax.experimental import pallas as pl
from jax.experimental.pallas import tpu as pltpu

F32 = jnp.float32
BF16 = jnp.bfloat16

D_MODEL = 1024
M_WIDTH = 1024
M_HEADS = 4
M_HD = 256
CONV_K = 4
A_WIDTH = 1024
A_HEADS = 8
A_VHD = 128
A_QKD = 64
N_BUCKETS = 32
MAX_DIST = 128
EPS = 1e-6
LAM_INIT = 0.8 - 0.6 * math.exp(-0.3 * 0)

GATE_COL0 = 5 * M_WIDTH
GATE_PAD = 128

M_CHUNK = 256
M_AUG = M_HD + 128
M_NB = 4
ATT_BLK = 256
ATT_HB = 2
VT_ROWS = A_VHD + 16
LOG2E = math.log2(math.e)
NEG = -1e30

VMEM_LIMIT = 56 * 1024 * 1024


def _nt_dot(a, b):
    return lax.dot_general(a, b, (((1,), (1,)), ((), ())), preferred_element_type=F32)


def _sigmoid(x):
    return 1.0 / (1.0 + jnp.exp(-x))


def _adaln_kernel(c_ref, w_ref, b_ref, o_ref):
    c = c_ref[...]
    sc = (c * _sigmoid(c)).astype(BF16)
    o_ref[0] = jnp.dot(sc, w_ref[...].astype(BF16), preferred_element_type=F32) + b_ref[0]


def _adaln(c, w_ada, b_ada):
    B, D = c.shape
    return pl.pallas_call(
        _adaln_kernel,
        grid=(3,),
        in_specs=[
            pl.BlockSpec((B, D), lambda j: (0, 0)),
            pl.BlockSpec((D, D), lambda j: (0, j)),
            pl.BlockSpec((1, 1, D), lambda j: (j, 0, 0)),
        ],
        out_specs=pl.BlockSpec((1, B, D), lambda j: (j, 0, 0)),
        out_shape=jax.ShapeDtypeStruct((3, B, D), F32),
        compiler_params=pltpu.CompilerParams(
            dimension_semantics=("arbitrary",), vmem_limit_bytes=VMEM_LIMIT),
        name="adaln",
    )(c, w_ada, b_ada.reshape(3, 1, D))


IN_TN = 1024
IN_TILES = 9
P_TILES = 7
P_COLS = P_TILES * IN_TN
IN_ORDER = (0, 1, 3, 4, 8, 2, 5, 6, 7)
IN_P_SLOT = {0: 0, 1: 1, 2: 2, 5: 3, 6: 4, 7: 5, 8: 6}


def _inproj_kernel(x_ref, xn_ref, shift_ref, scale_ref, shiftn_ref, scalen_ref, nw_ref,
                   wa_ref, wb_ref, wg_ref, cw_ref,
                   p_ref, g_ref, gates_ref, h_scr, hn_scr, gn_scr, raw0, raw1, tail_scr, *, tiles_per_seq):
    i = pl.program_id(0)
    tm = x_ref.shape[0]
    tn = IN_TN
    raws = (raw0, raw1)
    nbuf = len(raws)
    col_scale = {0: M_HD ** -0.5, 5: A_QKD ** -0.5 * LOG2E}

    def normed(xr, shr, scr):
        x = xr[...]
        y = x * lax.rsqrt(jnp.mean(x * x, axis=-1, keepdims=True) + EPS)
        return (y * (nw_ref[...] * (1.0 + scr[0])) + shr[0]).astype(BF16)

    @pl.when(i == 0)
    def _():
        h = normed(x_ref, shift_ref, scale_ref)
        h_scr[...] = h
        gates_ref[...] = jnp.dot(h, wg_ref[...], preferred_element_type=F32)
        tail_scr[...] = jnp.zeros_like(tail_scr)

    @pl.when(i > 0)
    def _():
        h_scr[...] = hn_scr[...]
        gates_ref[...] = gn_scr[...]

    first = (i % tiles_per_seq) == 0

    def matmul(j, buf):
        raw = raws[buf]

        def dot_cols(w_ref, c0, width):
            return jnp.dot(h_scr[...], w_ref[:, c0:c0 + width], preferred_element_type=F32)

        if j in (3, 4):
            half = tn // 2
            raw[8:tm + 8, 0:half] = dot_cols(wa_ref, 3 * tn + (j - 3) * half, half)
            raw[8:tm + 8, half:tn] = dot_cols(wa_ref, 4 * tn + (j - 3) * half, half)
        elif j < 3:
            raw[8:tm + 8, :] = dot_cols(wa_ref, j * tn, tn)
        else:
            raw[8:tm + 8, :] = dot_cols(wb_ref, (j - 5) * tn, tn)


    def tail(j, buf):
        raw = raws[buf]
        if j in (3, 4):
            o, z = raw[8:tm + 8, 0:tn // 2], raw[8:tm + 8, tn // 2:tn]
            g_ref[:, (j - 3) * (tn // 2):(j - 2) * (tn // 2)] = (_sigmoid(o) * (z * _sigmoid(z))).astype(BF16)
            return
        if j in (0, 1):
            raw[0:8, :] = jnp.where(first, 0.0, tail_scr[j])
            y = raw[8:tm + 8, :] * cw_ref[j, CONV_K - 1:CONV_K, :]
            for jj in range(CONV_K - 1):
                off = 8 - (CONV_K - 1) + jj
                y = y + raw[off:off + tm, :] * cw_ref[j, jj:jj + 1, :]
            tail_scr[j] = raw[tm:tm + 8, :]
            y = y * _sigmoid(y)
        elif j == 8:
            a = raw[8:tm + 8, :]
            y = a * _sigmoid(a)
        else:
            y = raw[8:tm + 8, :]
        if j in col_scale:
            y = y * col_scale[j]
        slot = IN_P_SLOT[j]
        p_ref[:, slot * tn:(slot + 1) * tn] = y.astype(BF16)

    matmul(IN_ORDER[0], 0)
    for n in range(1, IN_TILES):
        tail(IN_ORDER[n - 1], (n - 1) % nbuf)
        matmul(IN_ORDER[n], n % nbuf)
    tail(IN_ORDER[-1], (IN_TILES - 1) % nbuf)
    hn = normed(xn_ref, shiftn_ref, scalen_ref)
    hn_scr[...] = hn
    gn_scr[...] = jnp.dot(hn, wg_ref[...], preferred_element_type=F32)


def _inproj(x2d, mod3, norm_w, wa, wb, w_gate, conv_w, B, S, tm=512):
    M, D = x2d.shape
    tn = IN_TN
    tiles_per_seq = S // tm
    last = M // tm - 1

    def nxt(i):
        return jnp.minimum(i + 1, last)

    return pl.pallas_call(
        functools.partial(_inproj_kernel, tiles_per_seq=tiles_per_seq),
        grid=(M // tm,),
        in_specs=[
            pl.BlockSpec((tm, D), lambda i: (i, 0)),
            pl.BlockSpec((tm, D), lambda i: (nxt(i), 0)),
            pl.BlockSpec((1, 1, D), lambda i: (i // tiles_per_seq, 0, 0)),
            pl.BlockSpec((1, 1, D), lambda i: (B + i // tiles_per_seq, 0, 0)),
            pl.BlockSpec((1, 1, D), lambda i: (nxt(i) // tiles_per_seq, 0, 0)),
            pl.BlockSpec((1, 1, D), lambda i: (B + nxt(i) // tiles_per_seq, 0, 0)),
            pl.BlockSpec((1, D), lambda i: (0, 0)),
            pl.BlockSpec((D, 5 * tn), lambda i: (0, 0), pipeline_mode=pl.Buffered(1)),
            pl.BlockSpec((D, 4 * tn), lambda i: (0, 0), pipeline_mode=pl.Buffered(1)),
            pl.BlockSpec((D, GATE_PAD), lambda i: (0, 0)),
            pl.BlockSpec((2, CONV_K, tn), lambda i: (0, 0, 0)),
        ],
        out_specs=[
            pl.BlockSpec((tm, P_COLS), lambda i: (i, 0)),
            pl.BlockSpec((tm, M_WIDTH), lambda i: (i, 0)),
            pl.BlockSpec((tm, GATE_PAD), lambda i: (i, 0)),
        ],
        out_shape=[
            jax.ShapeDtypeStruct((M, P_COLS), BF16),
            jax.ShapeDtypeStruct((M, M_WIDTH), BF16),
            jax.ShapeDtypeStruct((M, GATE_PAD), F32),
        ],
        scratch_shapes=[
            pltpu.VMEM((tm, D), BF16),
            pltpu.VMEM((tm, D), BF16),
            pltpu.VMEM((tm, GATE_PAD), F32),
            pltpu.VMEM((tm + 8, tn), F32),
            pltpu.VMEM((tm + 8, tn), F32),
            pltpu.VMEM((2, 8, tn), F32),
        ],
        compiler_params=pltpu.CompilerParams(
            dimension_semantics=("arbitrary",), vmem_limit_bytes=VMEM_LIMIT),
        name="inproj",
    )(x2d, x2d, mod3, mod3, mod3, mod3, norm_w, wa, wb, w_gate, conv_w)


def _mlstm_kernel(q_ref, k_ref, v_ref, og_ref, g_ref, nw_ref, gb_ref, o_ref, ct_scr, m_scr):
    L = M_CHUNK
    NB = q_ref.shape[0]
    c = pl.program_id(1)

    @pl.when(c == 0)
    def _():
        ct_scr[...] = jnp.zeros_like(ct_scr)
        m_scr[...] = jnp.zeros_like(m_scr)

    row = lax.broadcasted_iota(jnp.int32, (L, GATE_PAD), 0)

    def gate_math(bb):
        g = g_ref[bb] + gb_ref[...]
        lf = jnp.minimum(g, 0.0) - jnp.log1p(jnp.exp(-jnp.abs(g)))
        bc = lf
        sh = 1
        while sh < L:
            bc = bc + jnp.where(row >= sh, pltpu.roll(bc, sh, 0), 0.0)
            sh *= 2
        bsh = pltpu.roll(bc, GATE_PAD - M_HEADS, 1)
        a = g - bsh
        cm = a
        sh = 1
        while sh < L:
            cm = jnp.maximum(cm, jnp.where(row >= sh, pltpu.roll(cm, sh, 0), -jnp.inf))
            sh *= 2
        m_prev = m_scr[bb]
        mm = jnp.maximum(cm, m_prev)
        iw = jnp.exp(m_prev - mm)
        eneg = jnp.exp(-(bsh + mm))
        m_scr[bb] = bsh[L - 1:L, :] + mm[L - 1:L, :]
        return dict(a=a, a_t=a.T, mm=mm, iw=iw, eneg=eneg)

    gm = [gate_math(bb) for bb in range(NB)]

    tri = (lax.broadcasted_iota(jnp.int32, (L, L), 1) <= lax.broadcasted_iota(jnp.int32, (L, L), 0))
    ones_col = (lax.broadcasted_iota(jnp.int32, (L, M_AUG - M_HD), 1) == 0).astype(BF16)

    chains = [(bb, h) for bb in range(NB) for h in range(M_HEADS)]
    cols = [slice(h * M_HD, (h + 1) * M_HD) for h in range(M_HEADS)]
    qs = {ch: q_ref[ch[0], :, cols[ch[1]]] for ch in chains}
    ks = {ch: k_ref[ch[0], :, cols[ch[1]]] for ch in chains}
    v_aug = {ch: jnp.concatenate([v_ref[ch[0], :, cols[ch[1]]], ones_col], axis=1) for ch in chains}
    cts = {ch: ct_scr[ch[0], ch[1]] for ch in chains}

    s_raw = {ch: _nt_dot(qs[ch], ks[ch]) for ch in chains}
    r_state = {ch: jnp.dot(qs[ch], cts[ch].astype(BF16), preferred_element_type=F32) for ch in chains}
    for ch in chains:
        bb, h = ch
        m_last = gm[bb]["mm"][L - 1:L, h:h + 1]
        w_row = jnp.exp(gm[bb]["a_t"][h:h + 1, :] - m_last)
        decay = gm[bb]["iw"][L - 1:L, h:h + 1]
        ktw = (ks[ch].astype(F32).T * w_row).astype(BF16)
        ct_scr[bb, h] = decay * cts[ch] + jnp.dot(ktw, v_aug[ch], preferred_element_type=F32)

    s16 = {}
    for ch in chains:
        bb, h = ch
        dw = jnp.where(tri, jnp.exp(gm[bb]["a_t"][h:h + 1, :] - gm[bb]["mm"][:, h:h + 1]), 0.0)
        s16[ch] = (s_raw[ch] * dw).astype(BF16)
    rs = {ch: gm[ch[0]]["iw"][:, ch[1]:ch[1] + 1] * r_state[ch]
          + jnp.dot(s16[ch], v_aug[ch], preferred_element_type=F32) for ch in chains}

    for ch in chains:
        bb, h = ch
        num = rs[ch][:, 0:M_HD]
        d = jnp.maximum(jnp.abs(rs[ch][:, M_HD:M_HD + 1]), gm[bb]["eneg"][:, h:h + 1])
        hc = num - jnp.mean(num, axis=-1, keepdims=True)
        var = jnp.mean(hc * hc, axis=-1, keepdims=True)
        hn = hc * lax.rsqrt(var + EPS * d * d) * nw_ref[:, cols[h]]
        o_ref[bb, :, cols[h]] = (hn * og_ref[bb, :, cols[h]].astype(F32)).astype(BF16)


def _mlstm(p3, og3, gates3, m_norm_w, gbias):
    B, S, _ = p3.shape
    L = M_CHUNK
    W = M_WIDTH

    NB = M_NB

    def col(cb):
        return pl.BlockSpec((NB, L, W), lambda b, c, cb=cb: (b, c, cb))

    return pl.pallas_call(
        _mlstm_kernel,
        grid=(B // NB, S // L),
        in_specs=[
            col(0), col(1), col(2),
            pl.BlockSpec((NB, L, W), lambda b, c: (b, c, 0)),
            pl.BlockSpec((NB, L, GATE_PAD), lambda b, c: (b, c, 0)),
            pl.BlockSpec((1, W), lambda b, c: (0, 0)),
            pl.BlockSpec((1, GATE_PAD), lambda b, c: (0, 0)),
        ],
        out_specs=pl.BlockSpec((NB, L, W), lambda b, c: (b, c, 0)),
        out_shape=jax.ShapeDtypeStruct((B, S, W), BF16),
        scratch_shapes=[
            pltpu.VMEM((NB, M_HEADS, M_HD, M_AUG), F32),
            pltpu.VMEM((NB, 1, GATE_PAD), F32),
        ],
        compiler_params=pltpu.CompilerParams(
            dimension_semantics=("arbitrary", "arbitrary"), vmem_limit_bytes=VMEM_LIMIT),
        name="mlstm",
    )(p3, p3, p3, og3, gates3, m_norm_w, gbias)


def _bucket_tiles():
    T = ATT_BLK
    kk = np.arange(T)[:, None]
    qq = np.arange(T)[None, :]
    out = []
    for base in (0, T):
        n = base + qq - kk
        max_exact = N_BUCKETS // 2
        nf = np.maximum(n, 1).astype(np.float64)
        large = max_exact + (np.log(nf / max_exact) / math.log(MAX_DIST / max_exact)
                             * (N_BUCKETS - max_exact)).astype(np.int64)
        large = np.minimum(large, N_BUCKETS - 1)
        bk = np.where(n < max_exact, n, large)
        bk = np.where(n < 0, -1, bk)
        out.append(bk)
    return np.stack(out).astype(np.int32)


def _attn_kernel(relb_ref, bucket_ref, lam_ref, anw_ref, q_ref, k_ref, v_ref, z_ref, o_ref,
                 bias_scr, vt_scr, qq_scr, s0_scr, s1_scr, acc_scr):
    T = ATT_BLK
    HB = ATT_HB
    S = k_ref.shape[1]
    NQ = S // T
    hg = pl.program_id(0)
    b = pl.program_id(1)
    heads = [slice(hb * A_VHD, (hb + 1) * A_VHD) for hb in range(HB)]

    @pl.when(b == 0)
    def _():
        for hb in range(HB):
            h = hg * HB + hb
            far = relb_ref[N_BUCKETS - 1, h]
            for t in range(2):
                bk = bucket_ref[t]
                tile = jnp.zeros((T, T), F32)
                for bb in range(N_BUCKETS - 1):
                    tile = jnp.where(bk == bb, (relb_ref[bb, h] - far) * LOG2E, tile)
                tile = jnp.where(bk < 0, NEG, tile)
                bias_scr[hb, t] = jnp.concatenate([tile, tile], axis=1)

    lane = lax.broadcasted_iota(jnp.int32, (T, A_VHD), 1)
    ones_rows = (lax.broadcasted_iota(jnp.int32, (VT_ROWS - A_VHD, T), 0) == 0).astype(BF16)
    for hb in range(HB):
        for ci in range(NQ):
            rows = slice(ci * T, (ci + 1) * T)
            vt_scr[hb, 0:A_VHD, rows] = v_ref[0, rows, heads[hb]].astype(F32).T.astype(BF16)
            vt_scr[hb, A_VHD:VT_ROWS, rows] = ones_rows
            qs = q_ref[0, rows, heads[hb]]
            zero = jnp.zeros_like(qs)
            qq_scr[hb, ci, 0:T, :] = jnp.where(lane < A_QKD, qs, zero)
            qq_scr[hb, ci, T:2 * T, :] = jnp.where(lane >= A_QKD, qs, zero)
    acc_scr[...] = jnp.zeros_like(acc_scr)

    s_bufs = (s0_scr, s1_scr)

    def issue(buf, qb, kj, hb):
        s_bufs[buf][hb] = _nt_dot(k_ref[0, kj * T:(kj + 1) * T, heads[hb]], qq_scr[hb, qb])

    def run_stage(cur, nxt, kj, bias_idx, ms):
        start = kj * T
        out = []
        for hb in range(HB):
            s = s_bufs[cur][hb]
            if bias_idx is not None:
                s = s + bias_scr[hb, bias_idx]
            m_old = ms[hb]
            m_new = jnp.maximum(m_old, jnp.max(s, axis=0, keepdims=True))
            alpha = jnp.exp2(m_old - m_new)
            p = jnp.exp2(s - m_new).astype(BF16)
            if nxt is not None:
                issue(1 - cur, nxt[0], nxt[1], hb)
            pv = jnp.dot(vt_scr[hb, :, start:start + T], p, preferred_element_type=F32)
            acc_scr[hb] = alpha * acc_scr[hb] + pv
            out.append(m_new)
        return tuple(out)

    lam_p = lam_ref[...]
    lam = (jnp.exp(jnp.sum(lam_p[0:1] * lam_p[1:2], axis=-1, keepdims=True))
           - jnp.exp(jnp.sum(lam_p[2:3] * lam_p[3:4], axis=-1, keepdims=True)) + LAM_INIT)

    def finalize(qi):
        rows = pl.ds(pl.multiple_of(qi * T, T), T)
        for hb in range(HB):
            acc = acc_scr[hb]
            o = acc[0:A_VHD] / acc[A_VHD:A_VHD + 1]
            d = (o[:, 0:T] - lam * o[:, T:2 * T]).T
            y = d * lax.rsqrt(jnp.mean(d * d, axis=-1, keepdims=True) + EPS) * anw_ref[...] * (1.0 - LAM_INIT)
            o_ref[0, rows, heads[hb]] = (y * z_ref[0, rows, heads[hb]].astype(F32)).astype(BF16)

    m_init = tuple(jnp.full((1, 2 * T), NEG, F32) for _ in range(HB))

    pairs = [(qi, kj) for qi in range(NQ) for kj in range(qi + 1)]
    for hb in range(HB):
        issue(0, 0, 0, hb)
    ms = m_init
    for t, (qi, kj) in enumerate(pairs):
        nxt = pairs[t + 1] if t + 1 < len(pairs) else None
        bias_idx = 0 if kj == qi else (1 if kj == qi - 1 else None)
        ms = run_stage(t % 2, nxt, kj, bias_idx, ms)
        if kj == qi:
            finalize(qi)
            ms = m_init


def _attn(proj3, rel_bias, lam_p, a_norm_w):
    B, S, _ = proj3.shape
    T = ATT_BLK
    HB = ATT_HB
    W = HB * A_VHD
    q0 = 3 * IN_TN // W
    k0 = q0 + A_HEADS // HB
    v0 = k0 + A_HEADS // HB
    z0 = v0 + A_HEADS // HB
    buckets = jnp.asarray(_bucket_tiles())
    return pl.pallas_call(
        _attn_kernel,
        grid=(A_HEADS // HB, B),
        in_specs=[
            pl.BlockSpec(memory_space=pltpu.SMEM),
            pl.BlockSpec((2, T, T), lambda h, b: (0, 0, 0)),
            pl.BlockSpec((4, A_QKD), lambda h, b: (0, 0)),
            pl.BlockSpec((1, A_VHD), lambda h, b: (0, 0)),
            pl.BlockSpec((1, S, W), lambda h, b: (b, 0, q0 + h)),
            pl.BlockSpec((1, S, W), lambda h, b: (b, 0, k0 + h)),
            pl.BlockSpec((1, S, W), lambda h, b: (b, 0, v0 + h)),
            pl.BlockSpec((1, S, W), lambda h, b: (b, 0, z0 + h)),
        ],
        out_specs=pl.BlockSpec((1, S, W), lambda h, b: (b, 0, h)),
        out_shape=jax.ShapeDtypeStruct((B, S, A_WIDTH), BF16),
        scratch_shapes=[
            pltpu.VMEM((HB, 2, T, 2 * T), F32),
            pltpu.VMEM((HB, VT_ROWS, S), BF16),
            pltpu.VMEM((HB, S // T, 2 * T, A_VHD), BF16),
            pltpu.VMEM((HB, T, 2 * T), F32),
            pltpu.VMEM((HB, T, 2 * T), F32),
            pltpu.VMEM((HB, VT_ROWS, 2 * T), F32),
        ],
        compiler_params=pltpu.CompilerParams(
            dimension_semantics=("arbitrary", "arbitrary"), vmem_limit_bytes=VMEM_LIMIT),
        name="attn",
    )(rel_bias, buckets, lam_p, a_norm_w, proj3, proj3, proj3, proj3)


def _outproj_kernel(hm_ref, ha_ref, x_ref, gate_ref, w_ref, fw_ref, o_ref, raw0, raw1):
    sub = raw0.shape[0]
    nsub = x_ref.shape[0] // sub
    raws = (raw0, raw1)

    def matmul(r, buf):
        rows = slice(r * sub, (r + 1) * sub)
        y = jnp.dot(hm_ref[rows, :], w_ref[0:M_WIDTH, :], preferred_element_type=F32)
        raws[buf][...] = y + jnp.dot(ha_ref[rows, :], w_ref[M_WIDTH:, :], preferred_element_type=F32)

    def tail(r, buf):
        rows = slice(r * sub, (r + 1) * sub)
        res = x_ref[rows, :] + gate_ref[0] * raws[buf][...]
        o_ref[rows, :] = res * lax.rsqrt(jnp.mean(res * res, axis=-1, keepdims=True) + EPS) * fw_ref[...]

    matmul(0, 0)
    for r in range(1, nsub):
        tail(r - 1, (r - 1) % 2)
        matmul(r, r % 2)
    tail(nsub - 1, (nsub - 1) % 2)


def _outproj(hm2d, ha2d, x2d, mod3, w_out, final_w, B, S, tm=1024, sub=512):
    M, D = x2d.shape
    tiles_per_seq = S // tm
    return pl.pallas_call(
        _outproj_kernel,
        grid=(M // tm,),
        scratch_shapes=[pltpu.VMEM((sub, D), F32), pltpu.VMEM((sub, D), F32)],
        in_specs=[
            pl.BlockSpec((tm, M_WIDTH), lambda i: (i, 0)),
            pl.BlockSpec((tm, A_WIDTH), lambda i: (i, 0)),
            pl.BlockSpec((tm, D), lambda i: (i, 0)),
            pl.BlockSpec((1, 1, D), lambda i: (2 * B + i // tiles_per_seq, 0, 0)),
            pl.BlockSpec((M_WIDTH + A_WIDTH, D), lambda i: (0, 0)),
            pl.BlockSpec((1, D), lambda i: (0, 0)),
        ],
        out_specs=pl.BlockSpec((tm, D), lambda i: (i, 0)),
        out_shape=jax.ShapeDtypeStruct((M, D), F32),
        compiler_params=pltpu.CompilerParams(
            dimension_semantics=("arbitrary",), vmem_limit_bytes=VMEM_LIMIT),
        name="outproj",
    )(hm2d, ha2d, x2d, mod3, w_out, final_w)


def kernel(x, c, norm_w, w_ada, b_ada, w_in, b_i, b_f, conv_q_w, conv_k_w, m_norm_w,
           lambda_q1, lambda_k1, lambda_q2, lambda_k2, a_norm_w, rel_bias, w_out, final_norm_w):
    B, S, D = x.shape
    assert norm_w.shape[0] == 1, "single layer only"
    x2d = x.reshape(B * S, D)

    mod = _adaln(c, w_ada[0], b_ada[0])
    mod3 = mod.reshape(3 * B, 1, D)

    w_in0 = w_in[0]
    a0 = GATE_COL0 + 2 * M_HEADS
    wa = w_in0[:, :GATE_COL0].astype(BF16)
    wb = w_in0[:, a0:].astype(BF16)
    w_gate = jnp.pad(w_in0[:, GATE_COL0:a0], ((0, 0), (0, GATE_PAD - 2 * M_HEADS))).astype(BF16)
    conv_w = jnp.stack([conv_q_w[0], conv_k_w[0]])
    p, og, gates = _inproj(x2d, mod3, norm_w, wa, wb, w_gate, conv_w, B, S)
    p3 = p.reshape(B, S, P_COLS)
    gates3 = gates.reshape(B, S, GATE_PAD)

    gbias = jnp.pad(jnp.concatenate([b_i[0], b_f[0]]), (0, GATE_PAD - 2 * M_HEADS)).reshape(1, GATE_PAD)
    hm = _mlstm(p3, og.reshape(B, S, M_WIDTH), gates3, m_norm_w, gbias)

    lam_p = jnp.stack([lambda_q1[0], lambda_k1[0], lambda_q2[0], lambda_k2[0]])
    ha = _attn(p3, rel_bias, lam_p, a_norm_w)

    out = _outproj(hm.reshape(B * S, M_WIDTH), ha.reshape(B * S, A_WIDTH), x2d, mod3,
                   w_out[0].astype(BF16), final_norm_w.reshape(1, D), B, S)
    return out.reshape(B, S, D)
```

```python
import functools
import math

import numpy as np
import jax
import jax.numpy as jnp
from jax import lax
from jax.experimental import pallas as pl
from jax.experimental.pallas import tpu as pltpu

F32 = jnp.float32
BF16 = jnp.bfloat16

D_MODEL = 1024
M_WIDTH = 1024
M_HEADS = 4
M_HD = 256
CONV_K = 4
A_WIDTH = 1024
A_HEADS = 8
A_VHD = 128
A_QKD = 64
N_BUCKETS = 32
MAX_DIST = 128
EPS = 1e-6
LAM_INIT = 0.8 - 0.6 * math.exp(-0.3 * 0)

V7X_LANES = 128
V7X_SUBLANES = 8
V7X_BF16_SUBLANES = 16
V7X_MXU_DIM = 256
V7X_VMEM_BYTES = 64 * 1024 * 1024
VMEM_LIMIT = V7X_VMEM_BYTES - 8 * 1024 * 1024

GATE_COL0 = 5 * M_WIDTH
GATE_PAD = V7X_LANES

M_CHUNK = V7X_MXU_DIM
M_AUG = M_HD + V7X_LANES
M_NB = 4
ATT_BLK = V7X_MXU_DIM
ATT_HB = 2
VT_ROWS = A_VHD + V7X_BF16_SUBLANES
LOG2E = math.log2(math.e)
NEG = -1e30


def _nt_dot(a, b):
    return lax.dot_general(a, b, (((1,), (1,)), ((), ())), preferred_element_type=F32)


def _sigmoid(x):
    return 1.0 / (1.0 + jnp.exp(-x))


def _adaln_kernel(c_ref, w_ref, b_ref, o_ref):
    c = c_ref[...]
    sc = (c * _sigmoid(c)).astype(BF16)
    o_ref[0] = jnp.dot(sc, w_ref[...].astype(BF16), preferred_element_type=F32) + b_ref[0]


def _adaln(c, w_ada, b_ada):
    B, D = c.shape
    return pl.pallas_call(
        _adaln_kernel,
        grid=(3,),
        in_specs=[
            pl.BlockSpec((B, D), lambda j: (0, 0)),
            pl.BlockSpec((D, D), lambda j: (0, j)),
            pl.BlockSpec((1, 1, D), lambda j: (j, 0, 0)),
        ],
        out_specs=pl.BlockSpec((1, B, D), lambda j: (j, 0, 0)),
        out_shape=jax.ShapeDtypeStruct((3, B, D), F32),
        compiler_params=pltpu.CompilerParams(
            dimension_semantics=("arbitrary",), vmem_limit_bytes=VMEM_LIMIT),
        name="adaln",
    )(c, w_ada, b_ada.reshape(3, 1, D))


IN_TN = 1024
IN_TILES = 9
P_TILES = 7
P_COLS = P_TILES * IN_TN
IN_ORDER = (0, 1, 3, 4, 8, 2, 5, 6, 7)
IN_P_SLOT = {0: 0, 1: 1, 2: 2, 5: 3, 6: 4, 7: 5, 8: 6}
IN_HIST = V7X_SUBLANES


def _inproj_kernel(x_ref, xn_ref, shift_ref, scale_ref, shiftn_ref, scalen_ref, nw_ref,
                   wa_ref, wb_ref, wg_ref, cw_ref,
                   p_ref, g_ref, gates_ref, h_scr, hn_scr, gn_scr, raw0, raw1, tail_scr, *, tiles_per_seq):
    i = pl.program_id(0)
    tm = x_ref.shape[0]
    tn = IN_TN
    H = IN_HIST
    raws = (raw0, raw1)
    nbuf = len(raws)
    col_scale = {0: M_HD ** -0.5, 5: A_QKD ** -0.5 * LOG2E}

    def normed(xr, shr, scr):
        x = xr[...]
        y = x * lax.rsqrt(jnp.mean(x * x, axis=-1, keepdims=True) + EPS)
        return (y * (nw_ref[...] * (1.0 + scr[0])) + shr[0]).astype(BF16)

    @pl.when(i == 0)
    def _():
        h = normed(x_ref, shift_ref, scale_ref)
        h_scr[...] = h
        gates_ref[...] = jnp.dot(h, wg_ref[...], preferred_element_type=F32)
        tail_scr[...] = jnp.zeros_like(tail_scr)

    @pl.when(i > 0)
    def _():
        h_scr[...] = hn_scr[...]
        gates_ref[...] = gn_scr[...]

    first = (i % tiles_per_seq) == 0

    def matmul(j, buf):
        raw = raws[buf]

        def dot_cols(w_ref, c0, width):
            return jnp.dot(h_scr[...], w_ref[:, c0:c0 + width], preferred_element_type=F32)

        if j in (3, 4):
            half = tn // 2
            raw[H:tm + H, 0:half] = dot_cols(wa_ref, 3 * tn + (j - 3) * half, half)
            raw[H:tm + H, half:tn] = dot_cols(wa_ref, 4 * tn + (j - 3) * half, half)
        elif j < 3:
            raw[H:tm + H, :] = dot_cols(wa_ref, j * tn, tn)
        else:
            raw[H:tm + H, :] = dot_cols(wb_ref, (j - 5) * tn, tn)

    def tail(j, buf):
        raw = raws[buf]
        if j in (3, 4):
            o, z = raw[H:tm + H, 0:tn // 2], raw[H:tm + H, tn // 2:tn]
            g_ref[:, (j - 3) * (tn // 2):(j - 2) * (tn // 2)] = (_sigmoid(o) * (z * _sigmoid(z))).astype(BF16)
            return
        if j in (0, 1):
            raw[0:H, :] = jnp.where(first, 0.0, tail_scr[j])
            y = raw[H:tm + H, :] * cw_ref[j, CONV_K - 1:CONV_K, :]
            for jj in range(CONV_K - 1):
                off = H - (CONV_K - 1) + jj
                y = y + raw[off:off + tm, :] * cw_ref[j, jj:jj + 1, :]
            tail_scr[j] = raw[tm:tm + H, :]
            y = y * _sigmoid(y)
        elif j == 8:
            a = raw[H:tm + H, :]
            y = a * _sigmoid(a)
        else:
            y = raw[H:tm + H, :]
        if j in col_scale:
            y = y * col_scale[j]
        slot = IN_P_SLOT[j]
        p_ref[:, slot * tn:(slot + 1) * tn] = y.astype(BF16)

    matmul(IN_ORDER[0], 0)
    for n in range(1, IN_TILES):
        tail(IN_ORDER[n - 1], (n - 1) % nbuf)
        matmul(IN_ORDER[n], n % nbuf)
    tail(IN_ORDER[-1], (IN_TILES - 1) % nbuf)
    hn = normed(xn_ref, shiftn_ref, scalen_ref)
    hn_scr[...] = hn
    gn_scr[...] = jnp.dot(hn, wg_ref[...], preferred_element_type=F32)


def _inproj(x2d, mod3, norm_w, wa, wb, w_gate, conv_w, B, S, tm=512):
    M, D = x2d.shape
    assert S % tm == 0
    tn = IN_TN
    tiles_per_seq = S // tm
    last = M // tm - 1

    def nxt(i):
        return jnp.minimum(i + 1, last)

    return pl.pallas_call(
        functools.partial(_inproj_kernel, tiles_per_seq=tiles_per_seq),
        grid=(M // tm,),
        in_specs=[
            pl.BlockSpec((tm, D), lambda i: (i, 0)),
            pl.BlockSpec((tm, D), lambda i: (nxt(i), 0)),
            pl.BlockSpec((1, 1, D), lambda i: (i // tiles_per_seq, 0, 0)),
            pl.BlockSpec((1, 1, D), lambda i: (B + i // tiles_per_seq, 0, 0)),
            pl.BlockSpec((1, 1, D), lambda i: (nxt(i) // tiles_per_seq, 0, 0)),
            pl.BlockSpec((1, 1, D), lambda i: (B + nxt(i) // tiles_per_seq, 0, 0)),
            pl.BlockSpec((1, D), lambda i: (0, 0)),
            pl.BlockSpec((D, 5 * tn), lambda i: (0, 0), pipeline_mode=pl.Buffered(1)),
            pl.BlockSpec((D, 4 * tn), lambda i: (0, 0), pipeline_mode=pl.Buffered(1)),
            pl.BlockSpec((D, GATE_PAD), lambda i: (0, 0)),
            pl.BlockSpec((2, CONV_K, tn), lambda i: (0, 0, 0)),
        ],
        out_specs=[
            pl.BlockSpec((tm, P_COLS), lambda i: (i, 0)),
            pl.BlockSpec((tm, M_WIDTH), lambda i: (i, 0)),
            pl.BlockSpec((tm, GATE_PAD), lambda i: (i, 0)),
        ],
        out_shape=[
            jax.ShapeDtypeStruct((M, P_COLS), BF16),
            jax.ShapeDtypeStruct((M, M_WIDTH), BF16),
            jax.ShapeDtypeStruct((M, GATE_PAD), F32),
        ],
        scratch_shapes=[
            pltpu.VMEM((tm, D), BF16),
            pltpu.VMEM((tm, D), BF16),
            pltpu.VMEM((tm, GATE_PAD), F32),
            pltpu.VMEM((tm + IN_HIST, tn), F32),
            pltpu.VMEM((tm + IN_HIST, tn), F32),
            pltpu.VMEM((2, IN_HIST, tn), F32),
        ],
        compiler_params=pltpu.CompilerParams(
            dimension_semantics=("arbitrary",), vmem_limit_bytes=VMEM_LIMIT),
        name="inproj",
    )(x2d, x2d, mod3, mod3, mod3, mod3, norm_w, wa, wb, w_gate, conv_w)


def _mlstm_kernel(q_ref, k_ref, v_ref, og_ref, g_ref, nw_ref, gb_ref, o_ref, ct_scr, m_scr):
    L = M_CHUNK
    NB = q_ref.shape[0]
    c = pl.program_id(1)

    @pl.when(c == 0)
    def _():
        ct_scr[...] = jnp.zeros_like(ct_scr)
        m_scr[...] = jnp.zeros_like(m_scr)

    row = lax.broadcasted_iota(jnp.int32, (L, GATE_PAD), 0)

    def gate_math(bb):
        g = g_ref[bb] + gb_ref[...]
        lf = jnp.minimum(g, 0.0) - jnp.log1p(jnp.exp(-jnp.abs(g)))
        bc = lf
        sh = 1
        while sh < L:
            bc = bc + jnp.where(row >= sh, pltpu.roll(bc, sh, 0), 0.0)
            sh *= 2
        bsh = pltpu.roll(bc, GATE_PAD - M_HEADS, 1)
        a = g - bsh
        cm = a
        sh = 1
        while sh < L:
            cm = jnp.maximum(cm, jnp.where(row >= sh, pltpu.roll(cm, sh, 0), -jnp.inf))
            sh *= 2
        m_prev = m_scr[bb]
        mm = jnp.maximum(cm, m_prev)
        iw = jnp.exp(m_prev - mm)
        eneg = jnp.exp(-(bsh + mm))
        m_scr[bb] = bsh[L - 1:L, :] + mm[L - 1:L, :]
        return dict(a=a, a_t=a.T, mm=mm, iw=iw, eneg=eneg)

    gm = [gate_math(bb) for bb in range(NB)]

    tri = (lax.broadcasted_iota(jnp.int32, (L, L), 1) <= lax.broadcasted_iota(jnp.int32, (L, L), 0))
    ones_col = (lax.broadcasted_iota(jnp.int32, (L, M_AUG - M_HD), 1) == 0).astype(BF16)

    chains = [(bb, h) for bb in range(NB) for h in range(M_HEADS)]
    cols = [slice(h * M_HD, (h + 1) * M_HD) for h in range(M_HEADS)]
    qs = {ch: q_ref[ch[0], :, cols[ch[1]]] for ch in chains}
    ks = {ch: k_ref[ch[0], :, cols[ch[1]]] for ch in chains}
    v_aug = {ch: jnp.concatenate([v_ref[ch[0], :, cols[ch[1]]], ones_col], axis=1) for ch in chains}
    cts = {ch: ct_scr[ch[0], ch[1]] for ch in chains}

    s_raw = {ch: _nt_dot(qs[ch], ks[ch]) for ch in chains}
    r_state = {ch: jnp.dot(qs[ch], cts[ch].astype(BF16), preferred_element_type=F32) for ch in chains}
    for ch in chains:
        bb, h = ch
        m_last = gm[bb]["mm"][L - 1:L, h:h + 1]
        w_row = jnp.exp(gm[bb]["a_t"][h:h + 1, :] - m_last)
        decay = gm[bb]["iw"][L - 1:L, h:h + 1]
        ktw = (ks[ch].astype(F32).T * w_row).astype(BF16)
        ct_scr[bb, h] = decay * cts[ch] + jnp.dot(ktw, v_aug[ch], preferred_element_type=F32)

    s16 = {}
    for ch in chains:
        bb, h = ch
        dw = jnp.where(tri, jnp.exp(gm[bb]["a_t"][h:h + 1, :] - gm[bb]["mm"][:, h:h + 1]), 0.0)
        s16[ch] = (s_raw[ch] * dw).astype(BF16)
    rs = {ch: gm[ch[0]]["iw"][:, ch[1]:ch[1] + 1] * r_state[ch]
          + jnp.dot(s16[ch], v_aug[ch], preferred_element_type=F32) for ch in chains}

    for ch in chains:
        bb, h = ch
        num = rs[ch][:, 0:M_HD]
        d = jnp.maximum(jnp.abs(rs[ch][:, M_HD:M_HD + 1]), gm[bb]["eneg"][:, h:h + 1])
        hc = num - jnp.mean(num, axis=-1, keepdims=True)
        var = jnp.mean(hc * hc, axis=-1, keepdims=True)
        hn = hc * lax.rsqrt(var + EPS * d * d) * nw_ref[:, cols[h]]
        o_ref[bb, :, cols[h]] = (hn * og_ref[bb, :, cols[h]].astype(F32)).astype(BF16)


def _mlstm(p3, og3, gates3, m_norm_w, gbias):
    B, S, _ = p3.shape
    L = M_CHUNK
    W = M_WIDTH
    NB = M_NB

    def col(cb):
        return pl.BlockSpec((NB, L, W), lambda b, c, cb=cb: (b, c, cb))

    return pl.pallas_call(
        _mlstm_kernel,
        grid=(B // NB, S // L),
        in_specs=[
            col(0), col(1), col(2),
            pl.BlockSpec((NB, L, W), lambda b, c: (b, c, 0)),
            pl.BlockSpec((NB, L, GATE_PAD), lambda b, c: (b, c, 0)),
            pl.BlockSpec((1, W), lambda b, c: (0, 0)),
            pl.BlockSpec((1, GATE_PAD), lambda b, c: (0, 0)),
        ],
        out_specs=pl.BlockSpec((NB, L, W), lambda b, c: (b, c, 0)),
        out_shape=jax.ShapeDtypeStruct((B, S, W), BF16),
        scratch_shapes=[
            pltpu.VMEM((NB, M_HEADS, M_HD, M_AUG), F32),
            pltpu.VMEM((NB, 1, GATE_PAD), F32),
        ],
        compiler_params=pltpu.CompilerParams(
            dimension_semantics=("arbitrary", "arbitrary"), vmem_limit_bytes=VMEM_LIMIT),
        name="mlstm",
    )(p3, p3, p3, og3, gates3, m_norm_w, gbias)


def _bucket_tiles():
    T = ATT_BLK
    kk = np.arange(T)[:, None]
    qq = np.arange(T)[None, :]
    out = []
    for base in (0, T):
        n = base + qq - kk
        max_exact = N_BUCKETS // 2
        nf = np.maximum(n, 1).astype(np.float64)
        large = max_exact + (np.log(nf / max_exact) / math.log(MAX_DIST / max_exact)
                             * (N_BUCKETS - max_exact)).astype(np.int64)
        large = np.minimum(large, N_BUCKETS - 1)
        bk = np.where(n < max_exact, n, large)
        bk = np.where(n < 0, -1, bk)
        out.append(bk)
    return np.stack(out).astype(np.int32)


def _attn_kernel(relb_ref, bucket_ref, lam_ref, anw_ref, q_ref, k_ref, v_ref, z_ref, o_ref,
                 bias_scr, vt_scr, qq_scr, s0_scr, s1_scr, acc_scr):
    T = ATT_BLK
    HB = ATT_HB
    S = k_ref.shape[1]
    NQ = S // T
    hg = pl.program_id(0)
    b = pl.program_id(1)
    heads = [slice(hb * A_VHD, (hb + 1) * A_VHD) for hb in range(HB)]

    @pl.when(b == 0)
    def _():
        for hb in range(HB):
            h = hg * HB + hb
            far = relb_ref[N_BUCKETS - 1, h]
            for t in range(2):
                bk = bucket_ref[t]
                tile = jnp.zeros((T, T), F32)
                for bb in range(N_BUCKETS - 1):
                    tile = jnp.where(bk == bb, (relb_ref[bb, h] - far) * LOG2E, tile)
                tile = jnp.where(bk < 0, NEG, tile)
                bias_scr[hb, t] = jnp.concatenate([tile, tile], axis=1)

    lane = lax.broadcasted_iota(jnp.int32, (T, A_VHD), 1)
    ones_rows = (lax.broadcasted_iota(jnp.int32, (VT_ROWS - A_VHD, T), 0) == 0).astype(BF16)
    for hb in range(HB):
        for ci in range(NQ):
            rows = slice(ci * T, (ci + 1) * T)
            vt_scr[hb, 0:A_VHD, rows] = v_ref[0, rows, heads[hb]].astype(F32).T.astype(BF16)
            vt_scr[hb, A_VHD:VT_ROWS, rows] = ones_rows
            qs = q_ref[0, rows, heads[hb]]
            zero = jnp.zeros_like(qs)
            qq_scr[hb, ci, 0:T, :] = jnp.where(lane < A_QKD, qs, zero)
            qq_scr[hb, ci, T:2 * T, :] = jnp.where(lane >= A_QKD, qs, zero)
    acc_scr[...] = jnp.zeros_like(acc_scr)

    s_bufs = (s0_scr, s1_scr)

    def issue(buf, qb, kj, hb):
        s_bufs[buf][hb] = _nt_dot(k_ref[0, kj * T:(kj + 1) * T, heads[hb]], qq_scr[hb, qb])

    def run_stage(cur, nxt, kj, bias_idx, ms):
        start = kj * T
        out = []
        for hb in range(HB):
            s = s_bufs[cur][hb]
            if bias_idx is not None:
                s = s + bias_scr[hb, bias_idx]
            m_old = ms[hb]
            m_new = jnp.maximum(m_old, jnp.max(s, axis=0, keepdims=True))
            alpha = jnp.exp2(m_old - m_new)
            p = jnp.exp2(s - m_new).astype(BF16)
            if nxt is not None:
                issue(1 - cur, nxt[0], nxt[1], hb)
            pv = jnp.dot(vt_scr[hb, :, start:start + T], p, preferred_element_type=F32)
            acc_scr[hb] = alpha * acc_scr[hb] + pv
            out.append(m_new)
        return tuple(out)

    lam_p = lam_ref[...]
    lam = (jnp.exp(jnp.sum(lam_p[0:1] * lam_p[1:2], axis=-1, keepdims=True))
           - jnp.exp(jnp.sum(lam_p[2:3] * lam_p[3:4], axis=-1, keepdims=True)) + LAM_INIT)

    def finalize(qi):
        rows = slice(qi * T, (qi + 1) * T)
        for hb in range(HB):
            acc = acc_scr[hb]
            o = acc[0:A_VHD] / acc[A_VHD:A_VHD + 1]
            d = (o[:, 0:T] - lam * o[:, T:2 * T]).T
            y = d * lax.rsqrt(jnp.mean(d * d, axis=-1, keepdims=True) + EPS) * anw_ref[...] * (1.0 - LAM_INIT)
            o_ref[0, rows, heads[hb]] = (y * z_ref[0, rows, heads[hb]].astype(F32)).astype(BF16)

    m_init = tuple(jnp.full((1, 2 * T), NEG, F32) for _ in range(HB))

    pairs = [(qi, kj) for qi in range(NQ) for kj in range(qi + 1)]
    for hb in range(HB):
        issue(0, 0, 0, hb)
    ms = m_init
    for t, (qi, kj) in enumerate(pairs):
        nxt = pairs[t + 1] if t + 1 < len(pairs) else None
        bias_idx = 0 if kj == qi else (1 if kj == qi - 1 else None)
        ms = run_stage(t % 2, nxt, kj, bias_idx, ms)
        if kj == qi:
            finalize(qi)
            ms = m_init


def _attn(proj3, rel_bias, lam_p, a_norm_w):
    B, S, _ = proj3.shape
    T = ATT_BLK
    HB = ATT_HB
    W = HB * A_VHD
    q0 = 3 * IN_TN // W
    k0 = q0 + A_HEADS // HB
    v0 = k0 + A_HEADS // HB
    z0 = v0 + A_HEADS // HB
    buckets = jnp.asarray(_bucket_tiles())
    return pl.pallas_call(
        _attn_kernel,
        grid=(A_HEADS // HB, B),
        in_specs=[
            pl.BlockSpec(memory_space=pltpu.SMEM),
            pl.BlockSpec((2, T, T), lambda h, b: (0, 0, 0)),
            pl.BlockSpec((4, A_QKD), lambda h, b: (0, 0)),
            pl.BlockSpec((1, A_VHD), lambda h, b: (0, 0)),
            pl.BlockSpec((1, S, W), lambda h, b: (b, 0, q0 + h)),
            pl.BlockSpec((1, S, W), lambda h, b: (b, 0, k0 + h)),
            pl.BlockSpec((1, S, W), lambda h, b: (b, 0, v0 + h)),
            pl.BlockSpec((1, S, W), lambda h, b: (b, 0, z0 + h)),
        ],
        out_specs=pl.BlockSpec((1, S, W), lambda h, b: (b, 0, h)),
        out_shape=jax.ShapeDtypeStruct((B, S, A_WIDTH), BF16),
        scratch_shapes=[
            pltpu.VMEM((HB, 2, T, 2 * T), F32),
            pltpu.VMEM((HB, VT_ROWS, S), BF16),
            pltpu.VMEM((HB, S // T, 2 * T, A_VHD), BF16),
            pltpu.VMEM((HB, T, 2 * T), F32),
            pltpu.VMEM((HB, T, 2 * T), F32),
            pltpu.VMEM((HB, VT_ROWS, 2 * T), F32),
        ],
        compiler_params=pltpu.CompilerParams(
            dimension_semantics=("arbitrary", "arbitrary"), vmem_limit_bytes=VMEM_LIMIT),
        name="attn",
    )(rel_bias, buckets, lam_p, a_norm_w, proj3, proj3, proj3, proj3)


def _outproj_kernel(hm_ref, ha_ref, x_ref, gate_ref, w_ref, fw_ref, o_ref, raw0, raw1):
    sub = raw0.shape[0]
    nsub = x_ref.shape[0] // sub
    raws = (raw0, raw1)

    def matmul(r, buf):
        rows = slice(r * sub, (r + 1) * sub)
        y = jnp.dot(hm_ref[rows, :], w_ref[0:M_WIDTH, :], preferred_element_type=F32)
        raws[buf][...] = y + jnp.dot(ha_ref[rows, :], w_ref[M_WIDTH:, :], preferred_element_type=F32)

    def tail(r, buf):
        rows = slice(r * sub, (r + 1) * sub)
        res = x_ref[rows, :] + gate_ref[0] * raws[buf][...]
        o_ref[rows, :] = res * lax.rsqrt(jnp.mean(res * res, axis=-1, keepdims=True) + EPS) * fw_ref[...]

    matmul(0, 0)
    for r in range(1, nsub):
        tail(r - 1, (r - 1) % 2)
        matmul(r, r % 2)
    tail(nsub - 1, (nsub - 1) % 2)


def _outproj(hm2d, ha2d, x2d, mod3, w_out, final_w, B, S, tm=1024, sub=512):
    M, D = x2d.shape
    assert S % tm == 0 and tm % sub == 0
    tiles_per_seq = S // tm
    return pl.pallas_call(
        _outproj_kernel,
        grid=(M // tm,),
        scratch_shapes=[pltpu.VMEM((sub, D), F32), pltpu.VMEM((sub, D), F32)],
        in_specs=[
            pl.BlockSpec((tm, M_WIDTH), lambda i: (i, 0)),
            pl.BlockSpec((tm, A_WIDTH), lambda i: (i, 0)),
            pl.BlockSpec((tm, D), lambda i: (i, 0)),
            pl.BlockSpec((1, 1, D), lambda i: (2 * B + i // tiles_per_seq, 0, 0)),
            pl.BlockSpec((M_WIDTH + A_WIDTH, D), lambda i: (0, 0)),
            pl.BlockSpec((1, D), lambda i: (0, 0)),
        ],
        out_specs=pl.BlockSpec((tm, D), lambda i: (i, 0)),
        out_shape=jax.ShapeDtypeStruct((M, D), F32),
        compiler_params=pltpu.CompilerParams(
            dimension_semantics=("arbitrary",), vmem_limit_bytes=VMEM_LIMIT),
        name="outproj",
    )(hm2d, ha2d, x2d, mod3, w_out, final_w)


def kernel(x, c, norm_w, w_ada, b_ada, w_in, b_i, b_f, conv_q_w, conv_k_w, m_norm_w,
           lambda_q1, lambda_k1, lambda_q2, lambda_k2, a_norm_w, rel_bias, w_out, final_norm_w):
    B, S, D = x.shape
    assert norm_w.shape[0] == 1, "single layer only"
    assert D == D_MODEL and w_in.shape[2] == GATE_COL0 + 2 * M_HEADS + 4 * A_WIDTH
    assert B % M_NB == 0 and S % M_CHUNK == 0 and S % ATT_BLK == 0
    x2d = x.reshape(B * S, D)

    mod = _adaln(c, w_ada[0], b_ada[0])
    mod3 = mod.reshape(3 * B, 1, D)

    w_in0 = w_in[0]
    a0 = GATE_COL0 + 2 * M_HEADS
    wa = w_in0[:, :GATE_COL0].astype(BF16)
    wb = w_in0[:, a0:].astype(BF16)
    w_gate = jnp.pad(w_in0[:, GATE_COL0:a0], ((0, 0), (0, GATE_PAD - 2 * M_HEADS))).astype(BF16)
    conv_w = jnp.stack([conv_q_w[0], conv_k_w[0]])
    p, og, gates = _inproj(x2d, mod3, norm_w, wa, wb, w_gate, conv_w, B, S)
    p3 = p.reshape(B, S, P_COLS)
    gates3 = gates.reshape(B, S, GATE_PAD)

    gbias = jnp.pad(jnp.concatenate([b_i[0], b_f[0]]), (0, GATE_PAD - 2 * M_HEADS)).reshape(1, GATE_PAD)
    hm = _mlstm(p3, og.reshape(B, S, M_WIDTH), gates3, m_norm_w, gbias)

    lam_p = jnp.stack([lambda_q1[0], lambda_k1[0], lambda_q2[0], lambda_k2[0]])
    ha = _attn(p3, rel_bias, lam_p, a_norm_w)

    out = _outproj(hm.reshape(B * S, M_WIDTH), ha.reshape(B * S, A_WIDTH), x2d, mod3,
                   w_out[0].astype(BF16), final_norm_w.reshape(1, D), B, S)
    return out.reshape(B, S, D)
```

```python
import functools
import math

import numpy as np
import jax
import jax.numpy as jnp
from jax import lax
from jax.experimental import pallas as pl
from jax.experimental.pallas import tpu as pltpu

F32 = jnp.float32
BF16 = jnp.bfloat16

D_MODEL = 1024
M_WIDTH = 1024
M_HEADS = 4
M_HD = 256
CONV_K = 4
A_WIDTH = 1024
A_HEADS = 8
A_VHD = 128
A_QKD = 64
N_BUCKETS = 32
MAX_DIST = 128
EPS = 1e-6
LAM_INIT = 0.8 - 0.6 * math.exp(-0.3 * 0)

V7X_LANES = 128
V7X_SUBLANES = 8
V7X_BF16_SUBLANES = 16
V7X_MXU_DIM = 256
V7X_VMEM_BYTES = 64 * 1024 * 1024
VMEM_LIMIT = V7X_VMEM_BYTES - 8 * 1024 * 1024

GATE_COL0 = 5 * M_WIDTH
GATE_PAD = V7X_LANES

M_CHUNK = V7X_MXU_DIM
M_AUG = M_HD + V7X_LANES
M_NB = 4
ATT_BLK = V7X_MXU_DIM
ATT_HB = 2
VT_ROWS = A_VHD + V7X_BF16_SUBLANES
LOG2E = math.log2(math.e)
NEG = -1e30


def _nt_dot(a, b):
    return lax.dot_general(a, b, (((1,), (1,)), ((), ())), preferred_element_type=F32)


def _sigmoid(x):
    return 1.0 / (1.0 + jnp.exp(-x))


def _adaln_kernel(c_ref, w_ref, b_ref, o_ref):
    c = c_ref[...]
    sc = (c * _sigmoid(c)).astype(BF16)
    o_ref[0] = jnp.dot(sc, w_ref[...].astype(BF16), preferred_element_type=F32) + b_ref[0]


def _adaln(c, w_ada, b_ada):
    B, D = c.shape
    return pl.pallas_call(
        _adaln_kernel,
        grid=(3,),
        in_specs=[
            pl.BlockSpec((B, D), lambda j: (0, 0)),
            pl.BlockSpec((D, D), lambda j: (0, j)),
            pl.BlockSpec((1, 1, D), lambda j: (j, 0, 0)),
        ],
        out_specs=pl.BlockSpec((1, B, D), lambda j: (j, 0, 0)),
        out_shape=jax.ShapeDtypeStruct((3, B, D), F32),
        compiler_params=pltpu.CompilerParams(
            dimension_semantics=("arbitrary",), vmem_limit_bytes=VMEM_LIMIT),
        name="adaln",
    )(c, w_ada, b_ada.reshape(3, 1, D))


def _wprep_kernel(a_ref, b_ref, wa_ref, wb_ref, wg_ref):
    t = pl.program_id(0)
    n_gate = 2 * M_HEADS

    @pl.when(t < 5)
    def _():
        wa_ref[...] = a_ref[0].astype(BF16)

    @pl.when(t >= 5)
    def _():
        x = jnp.concatenate([a_ref[0], b_ref[0]], axis=1)
        wb_ref[...] = x[:, n_gate:n_gate + IN_TN].astype(BF16)

    @pl.when(t == 5)
    def _():
        head = a_ref[0, :, 0:GATE_PAD]
        lane = lax.broadcasted_iota(jnp.int32, head.shape, 1)
        wg_ref[...] = jnp.where(lane < n_gate, head, 0.0).astype(BF16)


def _wprep(w_in):
    _, D, N = w_in.shape
    tn = IN_TN
    return pl.pallas_call(
        _wprep_kernel,
        grid=(IN_TILES,),
        in_specs=[
            pl.BlockSpec((1, D, tn), lambda t: (0, 0, t)),
            pl.BlockSpec((1, D, V7X_LANES), lambda t: (0, 0, (t + 1) * (tn // V7X_LANES))),
        ],
        out_specs=[
            pl.BlockSpec((D, tn), lambda t: (0, jnp.minimum(t, 4))),
            pl.BlockSpec((D, tn), lambda t: (0, jnp.maximum(t - 5, 0))),
            pl.BlockSpec((D, GATE_PAD), lambda t: (0, 0)),
        ],
        out_shape=[
            jax.ShapeDtypeStruct((D, 5 * tn), BF16),
            jax.ShapeDtypeStruct((D, 4 * tn), BF16),
            jax.ShapeDtypeStruct((D, GATE_PAD), BF16),
        ],
        compiler_params=pltpu.CompilerParams(
            dimension_semantics=("arbitrary",), vmem_limit_bytes=VMEM_LIMIT),
        name="wprep",
    )(w_in, w_in)


IN_TN = 1024
IN_TILES = 9
P_TILES = 7
P_COLS = P_TILES * IN_TN
IN_ORDER = (0, 1, 3, 4, 8, 2, 5, 6, 7)
IN_P_SLOT = {0: 0, 1: 1, 2: 2, 5: 3, 6: 4, 7: 5, 8: 6}
IN_HIST = V7X_SUBLANES


def _inproj_kernel(x_ref, xn_ref, shift_ref, scale_ref, shiftn_ref, scalen_ref, nw_ref,
                   wa_ref, wb_ref, wg_ref, cw_ref,
                   p_ref, g_ref, gates_ref, h_scr, hn_scr, gn_scr, raw0, raw1, tail_scr, *, tiles_per_seq):
    i = pl.program_id(0)
    tm = x_ref.shape[0]
    tn = IN_TN
    H = IN_HIST
    raws = (raw0, raw1)
    nbuf = len(raws)
    col_scale = {0: M_HD ** -0.5, 5: A_QKD ** -0.5 * LOG2E}

    def normed(xr, shr, scr):
        x = xr[...]
        y = x * lax.rsqrt(jnp.mean(x * x, axis=-1, keepdims=True) + EPS)
        return (y * (nw_ref[...] * (1.0 + scr[0])) + shr[0]).astype(BF16)

    @pl.when(i == 0)
    def _():
        h = normed(x_ref, shift_ref, scale_ref)
        h_scr[...] = h
        gates_ref[...] = jnp.dot(h, wg_ref[...], preferred_element_type=F32)
        tail_scr[...] = jnp.zeros_like(tail_scr)

    @pl.when(i > 0)
    def _():
        h_scr[...] = hn_scr[...]
        gates_ref[...] = gn_scr[...]

    first = (i % tiles_per_seq) == 0

    def matmul(j, buf):
        raw = raws[buf]

        def dot_cols(w_ref, c0, width):
            return jnp.dot(h_scr[...], w_ref[:, c0:c0 + width], preferred_element_type=F32)

        if j in (3, 4):
            half = tn // 2
            raw[H:tm + H, 0:half] = dot_cols(wa_ref, 3 * tn + (j - 3) * half, half)
            raw[H:tm + H, half:tn] = dot_cols(wa_ref, 4 * tn + (j - 3) * half, half)
        elif j < 3:
            raw[H:tm + H, :] = dot_cols(wa_ref, j * tn, tn)
        else:
            raw[H:tm + H, :] = dot_cols(wb_ref, (j - 5) * tn, tn)

    def tail(j, buf):
        raw = raws[buf]
        if j in (3, 4):
            o, z = raw[H:tm + H, 0:tn // 2], raw[H:tm + H, tn // 2:tn]
            g_ref[:, (j - 3) * (tn // 2):(j - 2) * (tn // 2)] = (_sigmoid(o) * (z * _sigmoid(z))).astype(BF16)
            return
        if j in (0, 1):
            raw[0:H, :] = jnp.where(first, 0.0, tail_scr[j])
            y = raw[H:tm + H, :] * cw_ref[j, CONV_K - 1:CONV_K, :]
            for jj in range(CONV_K - 1):
                off = H - (CONV_K - 1) + jj
                y = y + raw[off:off + tm, :] * cw_ref[j, jj:jj + 1, :]
            tail_scr[j] = raw[tm:tm + H, :]
            y = y * _sigmoid(y)
        elif j == 8:
            a = raw[H:tm + H, :]
            y = a * _sigmoid(a)
        else:
            y = raw[H:tm + H, :]
        if j in col_scale:
            y = y * col_scale[j]
        slot = IN_P_SLOT[j]
        p_ref[:, slot * tn:(slot + 1) * tn] = y.astype(BF16)

    matmul(IN_ORDER[0], 0)
    for n in range(1, IN_TILES):
        tail(IN_ORDER[n - 1], (n - 1) % nbuf)
        matmul(IN_ORDER[n], n % nbuf)
    tail(IN_ORDER[-1], (IN_TILES - 1) % nbuf)
    hn = normed(xn_ref, shiftn_ref, scalen_ref)
    hn_scr[...] = hn
    gn_scr[...] = jnp.dot(hn, wg_ref[...], preferred_element_type=F32)


def _inproj(x2d, mod3, norm_w, wa, wb, w_gate, conv_w, B, S, tm=512):
    M, D = x2d.shape
    assert S % tm == 0
    tn = IN_TN
    tiles_per_seq = S // tm
    last = M // tm - 1

    def nxt(i):
        return jnp.minimum(i + 1, last)

    return pl.pallas_call(
        functools.partial(_inproj_kernel, tiles_per_seq=tiles_per_seq),
        grid=(M // tm,),
        in_specs=[
            pl.BlockSpec((tm, D), lambda i: (i, 0)),
            pl.BlockSpec((tm, D), lambda i: (nxt(i), 0)),
            pl.BlockSpec((1, 1, D), lambda i: (i // tiles_per_seq, 0, 0)),
            pl.BlockSpec((1, 1, D), lambda i: (B + i // tiles_per_seq, 0, 0)),
            pl.BlockSpec((1, 1, D), lambda i: (nxt(i) // tiles_per_seq, 0, 0)),
            pl.BlockSpec((1, 1, D), lambda i: (B + nxt(i) // tiles_per_seq, 0, 0)),
            pl.BlockSpec((1, D), lambda i: (0, 0)),
            pl.BlockSpec((D, 5 * tn), lambda i: (0, 0), pipeline_mode=pl.Buffered(1)),
            pl.BlockSpec((D, 4 * tn), lambda i: (0, 0), pipeline_mode=pl.Buffered(1)),
            pl.BlockSpec((D, GATE_PAD), lambda i: (0, 0)),
            pl.BlockSpec((2, CONV_K, tn), lambda i: (0, 0, 0)),
        ],
        out_specs=[
            pl.BlockSpec((tm, P_COLS), lambda i: (i, 0)),
            pl.BlockSpec((tm, M_WIDTH), lambda i: (i, 0)),
            pl.BlockSpec((tm, GATE_PAD), lambda i: (i, 0)),
        ],
        out_shape=[
            jax.ShapeDtypeStruct((M, P_COLS), BF16),
            jax.ShapeDtypeStruct((M, M_WIDTH), BF16),
            jax.ShapeDtypeStruct((M, GATE_PAD), F32),
        ],
        scratch_shapes=[
            pltpu.VMEM((tm, D), BF16),
            pltpu.VMEM((tm, D), BF16),
            pltpu.VMEM((tm, GATE_PAD), F32),
            pltpu.VMEM((tm + IN_HIST, tn), F32),
            pltpu.VMEM((tm + IN_HIST, tn), F32),
            pltpu.VMEM((2, IN_HIST, tn), F32),
        ],
        compiler_params=pltpu.CompilerParams(
            dimension_semantics=("arbitrary",), vmem_limit_bytes=VMEM_LIMIT),
        name="inproj",
    )(x2d, x2d, mod3, mod3, mod3, mod3, norm_w, wa, wb, w_gate, conv_w)


def _mlstm_kernel(q_ref, k_ref, v_ref, og_ref, g_ref, nw_ref, gb_ref, o_ref, ct_scr, m_scr):
    L = M_CHUNK
    NB = q_ref.shape[0]
    c = pl.program_id(1)

    @pl.when(c == 0)
    def _():
        ct_scr[...] = jnp.zeros_like(ct_scr)
        m_scr[...] = jnp.zeros_like(m_scr)

    row = lax.broadcasted_iota(jnp.int32, (L, GATE_PAD), 0)

    def gate_math(bb):
        g = g_ref[bb] + gb_ref[...]
        lf = jnp.minimum(g, 0.0) - jnp.log1p(jnp.exp(-jnp.abs(g)))
        bc = lf
        sh = 1
        while sh < L:
            bc = bc + jnp.where(row >= sh, pltpu.roll(bc, sh, 0), 0.0)
            sh *= 2
        bsh = pltpu.roll(bc, GATE_PAD - M_HEADS, 1)
        a = g - bsh
        cm = a
        sh = 1
        while sh < L:
            cm = jnp.maximum(cm, jnp.where(row >= sh, pltpu.roll(cm, sh, 0), -jnp.inf))
            sh *= 2
        m_prev = m_scr[bb]
        mm = jnp.maximum(cm, m_prev)
        iw = jnp.exp(m_prev - mm)
        eneg = jnp.exp(-(bsh + mm))
        m_scr[bb] = bsh[L - 1:L, :] + mm[L - 1:L, :]
        return dict(a=a, a_t=a.T, mm=mm, iw=iw, eneg=eneg)

    gm = [gate_math(bb) for bb in range(NB)]

    tri = (lax.broadcasted_iota(jnp.int32, (L, L), 1) <= lax.broadcasted_iota(jnp.int32, (L, L), 0))
    ones_col = (lax.broadcasted_iota(jnp.int32, (L, M_AUG - M_HD), 1) == 0).astype(BF16)

    chains = [(bb, h) for bb in range(NB) for h in range(M_HEADS)]
    cols = [slice(h * M_HD, (h + 1) * M_HD) for h in range(M_HEADS)]
    qs = {ch: q_ref[ch[0], :, cols[ch[1]]] for ch in chains}
    ks = {ch: k_ref[ch[0], :, cols[ch[1]]] for ch in chains}
    v_aug = {ch: jnp.concatenate([v_ref[ch[0], :, cols[ch[1]]], ones_col], axis=1) for ch in chains}
    cts = {ch: ct_scr[ch[0], ch[1]] for ch in chains}

    s_raw = {ch: _nt_dot(qs[ch], ks[ch]) for ch in chains}
    r_state = {ch: jnp.dot(qs[ch], cts[ch].astype(BF16), preferred_element_type=F32) for ch in chains}
    for ch in chains:
        bb, h = ch
        m_last = gm[bb]["mm"][L - 1:L, h:h + 1]
        w_row = jnp.exp(gm[bb]["a_t"][h:h + 1, :] - m_last)
        decay = gm[bb]["iw"][L - 1:L, h:h + 1]
        ktw = (ks[ch].astype(F32).T * w_row).astype(BF16)
        ct_scr[bb, h] = decay * cts[ch] + jnp.dot(ktw, v_aug[ch], preferred_element_type=F32)

    s16 = {}
    for ch in chains:
        bb, h = ch
        dw = jnp.where(tri, jnp.exp(gm[bb]["a_t"][h:h + 1, :] - gm[bb]["mm"][:, h:h + 1]), 0.0)
        s16[ch] = (s_raw[ch] * dw).astype(BF16)
    rs = {ch: gm[ch[0]]["iw"][:, ch[1]:ch[1] + 1] * r_state[ch]
          + jnp.dot(s16[ch], v_aug[ch], preferred_element_type=F32) for ch in chains}

    for ch in chains:
        bb, h = ch
        num = rs[ch][:, 0:M_HD]
        d = jnp.maximum(jnp.abs(rs[ch][:, M_HD:M_HD + 1]), gm[bb]["eneg"][:, h:h + 1])
        hc = num - jnp.mean(num, axis=-1, keepdims=True)
        var = jnp.mean(hc * hc, axis=-1, keepdims=True)
        hn = hc * lax.rsqrt(var + EPS * d * d) * nw_ref[:, cols[h]]
        o_ref[bb, :, cols[h]] = (hn * og_ref[bb, :, cols[h]].astype(F32)).astype(BF16)


def _mlstm(p3, og3, gates3, m_norm_w, gbias):
    B, S, _ = p3.shape
    L = M_CHUNK
    W = M_WIDTH
    NB = M_NB

    def col(cb):
        return pl.BlockSpec((NB, L, W), lambda b, c, cb=cb: (b, c, cb))

    return pl.pallas_call(
        _mlstm_kernel,
        grid=(B // NB, S // L),
        in_specs=[
            col(0), col(1), col(2),
            pl.BlockSpec((NB, L, W), lambda b, c: (b, c, 0)),
            pl.BlockSpec((NB, L, GATE_PAD), lambda b, c: (b, c, 0)),
            pl.BlockSpec((1, W), lambda b, c: (0, 0)),
            pl.BlockSpec((1, GATE_PAD), lambda b, c: (0, 0)),
        ],
        out_specs=pl.BlockSpec((NB, L, W), lambda b, c: (b, c, 0)),
        out_shape=jax.ShapeDtypeStruct((B, S, W), BF16),
        scratch_shapes=[
            pltpu.VMEM((NB, M_HEADS, M_HD, M_AUG), F32),
            pltpu.VMEM((NB, 1, GATE_PAD), F32),
        ],
        compiler_params=pltpu.CompilerParams(
            dimension_semantics=("arbitrary", "arbitrary"), vmem_limit_bytes=VMEM_LIMIT),
        name="mlstm",
    )(p3, p3, p3, og3, gates3, m_norm_w, gbias)


def _bucket_tiles():
    T = ATT_BLK
    kk = np.arange(T)[:, None]
    qq = np.arange(T)[None, :]
    out = []
    for base in (0, T):
        n = base + qq - kk
        max_exact = N_BUCKETS // 2
        nf = np.maximum(n, 1).astype(np.float64)
        large = max_exact + (np.log(nf / max_exact) / math.log(MAX_DIST / max_exact)
                             * (N_BUCKETS - max_exact)).astype(np.int64)
        large = np.minimum(large, N_BUCKETS - 1)
        bk = np.where(n < max_exact, n, large)
        bk = np.where(n < 0, -1, bk)
        out.append(bk)
    return np.stack(out).astype(np.int32)


def _attn_kernel(relb_ref, bucket_ref, lam_ref, anw_ref, q_ref, k_ref, v_ref, z_ref, o_ref,
                 bias_scr, vt_scr, qq_scr, s0_scr, s1_scr, acc_scr):
    T = ATT_BLK
    HB = ATT_HB
    S = k_ref.shape[1]
    NQ = S // T
    hg = pl.program_id(0)
    b = pl.program_id(1)
    heads = [slice(hb * A_VHD, (hb + 1) * A_VHD) for hb in range(HB)]

    @pl.when(b == 0)
    def _():
        for hb in range(HB):
            h = hg * HB + hb
            far = relb_ref[N_BUCKETS - 1, h]
            for t in range(2):
                bk = bucket_ref[t]
                tile = jnp.zeros((T, T), F32)
                for bb in range(N_BUCKETS - 1):
                    tile = jnp.where(bk == bb, (relb_ref[bb, h] - far) * LOG2E, tile)
                tile = jnp.where(bk < 0, NEG, tile)
                bias_scr[hb, t] = jnp.concatenate([tile, tile], axis=1)

    lane = lax.broadcasted_iota(jnp.int32, (T, A_VHD), 1)
    ones_rows = (lax.broadcasted_iota(jnp.int32, (VT_ROWS - A_VHD, T), 0) == 0).astype(BF16)
    for hb in range(HB):
        for ci in range(NQ):
            rows = slice(ci * T, (ci + 1) * T)
            vt_scr[hb, 0:A_VHD, rows] = v_ref[0, rows, heads[hb]].astype(F32).T.astype(BF16)
            vt_scr[hb, A_VHD:VT_ROWS, rows] = ones_rows
            qs = q_ref[0, rows, heads[hb]]
            zero = jnp.zeros_like(qs)
            qq_scr[hb, ci, 0:T, :] = jnp.where(lane < A_QKD, qs, zero)
            qq_scr[hb, ci, T:2 * T, :] = jnp.where(lane >= A_QKD, qs, zero)
    acc_scr[...] = jnp.zeros_like(acc_scr)

    s_bufs = (s0_scr, s1_scr)

    def issue(buf, qb, kj, hb):
        s_bufs[buf][hb] = _nt_dot(k_ref[0, kj * T:(kj + 1) * T, heads[hb]], qq_scr[hb, qb])

    def run_stage(cur, nxt, kj, bias_idx, ms):
        start = kj * T
        out = []
        for hb in range(HB):
            s = s_bufs[cur][hb]
            if bias_idx is not None:
                s = s + bias_scr[hb, bias_idx]
            m_old = ms[hb]
            m_new = jnp.maximum(m_old, jnp.max(s, axis=0, keepdims=True))
            alpha = jnp.exp2(m_old - m_new)
            p = jnp.exp2(s - m_new).astype(BF16)
            if nxt is not None:
                issue(1 - cur, nxt[0], nxt[1], hb)
            pv = jnp.dot(vt_scr[hb, :, start:start + T], p, preferred_element_type=F32)
            acc_scr[hb] = alpha * acc_scr[hb] + pv
            out.append(m_new)
        return tuple(out)

    lam_p = lam_ref[...]
    lam = (jnp.exp(jnp.sum(lam_p[0:1] * lam_p[1:2], axis=-1, keepdims=True))
           - jnp.exp(jnp.sum(lam_p[2:3] * lam_p[3:4], axis=-1, keepdims=True)) + LAM_INIT)

    def finalize(qi):
        rows = slice(qi * T, (qi + 1) * T)
        for hb in range(HB):
            acc = acc_scr[hb]
            o = acc[0:A_VHD] / acc[A_VHD:A_VHD + 1]
            d = (o[:, 0:T] - lam * o[:, T:2 * T]).T
            y = d * lax.rsqrt(jnp.mean(d * d, axis=-1, keepdims=True) + EPS) * anw_ref[...] * (1.0 - LAM_INIT)
            o_ref[0, rows, heads[hb]] = (y * z_ref[0, rows, heads[hb]].astype(F32)).astype(BF16)

    m_init = tuple(jnp.full((1, 2 * T), NEG, F32) for _ in range(HB))

    pairs = [(qi, kj) for qi in range(NQ) for kj in range(qi + 1)]
    for hb in range(HB):
        issue(0, 0, 0, hb)
    ms = m_init
    for t, (qi, kj) in enumerate(pairs):
        nxt = pairs[t + 1] if t + 1 < len(pairs) else None
        bias_idx = 0 if kj == qi else (1 if kj == qi - 1 else None)
        ms = run_stage(t % 2, nxt, kj, bias_idx, ms)
        if kj == qi:
            finalize(qi)
            ms = m_init


def _attn(proj3, rel_bias, lam_p, a_norm_w):
    B, S, _ = proj3.shape
    T = ATT_BLK
    HB = ATT_HB
    W = HB * A_VHD
    q0 = 3 * IN_TN // W
    k0 = q0 + A_HEADS // HB
    v0 = k0 + A_HEADS // HB
    z0 = v0 + A_HEADS // HB
    buckets = jnp.asarray(_bucket_tiles())
    return pl.pallas_call(
        _attn_kernel,
        grid=(A_HEADS // HB, B),
        in_specs=[
            pl.BlockSpec(memory_space=pltpu.SMEM),
            pl.BlockSpec((2, T, T), lambda h, b: (0, 0, 0)),
            pl.BlockSpec((4, A_QKD), lambda h, b: (0, 0)),
            pl.BlockSpec((1, A_VHD), lambda h, b: (0, 0)),
            pl.BlockSpec((1, S, W), lambda h, b: (b, 0, q0 + h)),
            pl.BlockSpec((1, S, W), lambda h, b: (b, 0, k0 + h)),
            pl.BlockSpec((1, S, W), lambda h, b: (b, 0, v0 + h)),
            pl.BlockSpec((1, S, W), lambda h, b: (b, 0, z0 + h)),
        ],
        out_specs=pl.BlockSpec((1, S, W), lambda h, b: (b, 0, h)),
        out_shape=jax.ShapeDtypeStruct((B, S, A_WIDTH), BF16),
        scratch_shapes=[
            pltpu.VMEM((HB, 2, T, 2 * T), F32),
            pltpu.VMEM((HB, VT_ROWS, S), BF16),
            pltpu.VMEM((HB, S // T, 2 * T, A_VHD), BF16),
            pltpu.VMEM((HB, T, 2 * T), F32),
            pltpu.VMEM((HB, T, 2 * T), F32),
            pltpu.VMEM((HB, VT_ROWS, 2 * T), F32),
        ],
        compiler_params=pltpu.CompilerParams(
            dimension_semantics=("arbitrary", "arbitrary"), vmem_limit_bytes=VMEM_LIMIT),
        name="attn",
    )(rel_bias, buckets, lam_p, a_norm_w, proj3, proj3, proj3, proj3)


def _outproj_kernel(hm_ref, ha_ref, x_ref, gate_ref, w_ref, fw_ref, o_ref, raw0, raw1):
    sub = raw0.shape[0]
    nsub = x_ref.shape[0] // sub
    raws = (raw0, raw1)

    def matmul(r, buf):
        rows = slice(r * sub, (r + 1) * sub)
        y = jnp.dot(hm_ref[rows, :], w_ref[0:M_WIDTH, :], preferred_element_type=F32)
        raws[buf][...] = y + jnp.dot(ha_ref[rows, :], w_ref[M_WIDTH:, :], preferred_element_type=F32)

    def tail(r, buf):
        rows = slice(r * sub, (r + 1) * sub)
        res = x_ref[rows, :] + gate_ref[0] * raws[buf][...]
        o_ref[rows, :] = res * lax.rsqrt(jnp.mean(res * res, axis=-1, keepdims=True) + EPS) * fw_ref[...]

    matmul(0, 0)
    for r in range(1, nsub):
        tail(r - 1, (r - 1) % 2)
        matmul(r, r % 2)
    tail(nsub - 1, (nsub - 1) % 2)


def _outproj(hm2d, ha2d, x2d, mod3, w_out, final_w, B, S, tm=1024, sub=512):
    M, D = x2d.shape
    assert S % tm == 0 and tm % sub == 0
    tiles_per_seq = S // tm
    return pl.pallas_call(
        _outproj_kernel,
        grid=(M // tm,),
        scratch_shapes=[pltpu.VMEM((sub, D), F32), pltpu.VMEM((sub, D), F32)],
        in_specs=[
            pl.BlockSpec((tm, M_WIDTH), lambda i: (i, 0)),
            pl.BlockSpec((tm, A_WIDTH), lambda i: (i, 0)),
            pl.BlockSpec((tm, D), lambda i: (i, 0)),
            pl.BlockSpec((1, 1, D), lambda i: (2 * B + i // tiles_per_seq, 0, 0)),
            pl.BlockSpec((M_WIDTH + A_WIDTH, D), lambda i: (0, 0)),
            pl.BlockSpec((1, D), lambda i: (0, 0)),
        ],
        out_specs=pl.BlockSpec((tm, D), lambda i: (i, 0)),
        out_shape=jax.ShapeDtypeStruct((M, D), F32),
        compiler_params=pltpu.CompilerParams(
            dimension_semantics=("arbitrary",), vmem_limit_bytes=VMEM_LIMIT),
        name="outproj",
    )(hm2d, ha2d, x2d, mod3, w_out, final_w)


def kernel(x, c, norm_w, w_ada, b_ada, w_in, b_i, b_f, conv_q_w, conv_k_w, m_norm_w,
           lambda_q1, lambda_k1, lambda_q2, lambda_k2, a_norm_w, rel_bias, w_out, final_norm_w):
    B, S, D = x.shape
    assert norm_w.shape[0] == 1, "single layer only"
    assert D == D_MODEL and w_in.shape[2] == GATE_COL0 + 2 * M_HEADS + 4 * A_WIDTH
    assert B % M_NB == 0 and S % M_CHUNK == 0 and S % ATT_BLK == 0
    x2d = x.reshape(B * S, D)

    mod = _adaln(c, w_ada[0], b_ada[0])
    mod3 = mod.reshape(3 * B, 1, D)

    wa, wb, w_gate = _wprep(w_in)
    conv_w = jnp.stack([conv_q_w[0], conv_k_w[0]])
    p, og, gates = _inproj(x2d, mod3, norm_w, wa, wb, w_gate, conv_w, B, S)
    p3 = p.reshape(B, S, P_COLS)
    gates3 = gates.reshape(B, S, GATE_PAD)

    gbias = jnp.pad(jnp.concatenate([b_i[0], b_f[0]]), (0, GATE_PAD - 2 * M_HEADS)).reshape(1, GATE_PAD)
    hm = _mlstm(p3, og.reshape(B, S, M_WIDTH), gates3, m_norm_w, gbias)

    lam_p = jnp.stack([lambda_q1[0], lambda_k1[0], lambda_q2[0], lambda_k2[0]])
    ha = _attn(p3, rel_bias, lam_p, a_norm_w)

    out = _outproj(hm.reshape(B * S, M_WIDTH), ha.reshape(B * S, A_WIDTH), x2d, mod3,
                   w_out[0].astype(BF16), final_norm_w.reshape(1, D), B, S)
    return out.reshape(B, S, D)
```

```python
import functools
import math

import numpy as np
import jax
import jax.numpy as jnp
from jax import lax
from jax.experimental import pallas as pl
from jax.experimental.pallas import tpu as pltpu

F32 = jnp.float32
BF16 = jnp.bfloat16

D_MODEL = 1024
M_WIDTH = 1024
M_HEADS = 4
M_HD = 256
CONV_K = 4
A_WIDTH = 1024
A_HEADS = 8
A_VHD = 128
A_QKD = 64
N_BUCKETS = 32
MAX_DIST = 128
EPS = 1e-6
LAM_INIT = 0.8 - 0.6 * math.exp(-0.3 * 0)

V7X_LANES = 128
V7X_SUBLANES = 8
V7X_BF16_SUBLANES = 16
V7X_MXU_DIM = 256
V7X_VMEM_BYTES = 64 * 1024 * 1024
VMEM_LIMIT = V7X_VMEM_BYTES - 8 * 1024 * 1024

GATE_COL0 = 5 * M_WIDTH
GATE_PAD = V7X_LANES

M_CHUNK = V7X_MXU_DIM
M_AUG = M_HD + V7X_LANES
M_NB = 4
ATT_BLK = V7X_MXU_DIM
ATT_HB = 2
VT_ROWS = A_VHD + V7X_BF16_SUBLANES
LOG2E = math.log2(math.e)
NEG = -1e30


def _nt_dot(a, b):
    return lax.dot_general(a, b, (((1,), (1,)), ((), ())), preferred_element_type=F32)


def _sigmoid(x):
    return 1.0 / (1.0 + jnp.exp(-x))


def _adaln_kernel(c_ref, w_ref, b_ref, o_ref):
    c = c_ref[...]
    sc = (c * _sigmoid(c)).astype(BF16)
    o_ref[0] = jnp.dot(sc, w_ref[...].astype(BF16), preferred_element_type=F32) + b_ref[0]


def _adaln(c, w_ada, b_ada):
    B, D = c.shape
    return pl.pallas_call(
        _adaln_kernel,
        grid=(3,),
        in_specs=[
            pl.BlockSpec((B, D), lambda j: (0, 0)),
            pl.BlockSpec((D, D), lambda j: (0, j)),
            pl.BlockSpec((1, 1, D), lambda j: (j, 0, 0)),
        ],
        out_specs=pl.BlockSpec((1, B, D), lambda j: (j, 0, 0)),
        out_shape=jax.ShapeDtypeStruct((3, B, D), F32),
        compiler_params=pltpu.CompilerParams(
            dimension_semantics=("arbitrary",), vmem_limit_bytes=VMEM_LIMIT),
        name="adaln",
    )(c, w_ada, b_ada.reshape(3, 1, D))


IN_TN = 1024
IN_TILES = 9
P_TILES = 7
P_COLS = P_TILES * IN_TN
IN_ORDER = (0, 1, 3, 4, 8, 2, 5, 6, 7)
IN_P_SLOT = {0: 0, 1: 1, 2: 2, 5: 3, 6: 4, 7: 5, 8: 6}
IN_HIST = V7X_SUBLANES


def _inproj_kernel(x_ref, xn_ref, shift_ref, scale_ref, shiftn_ref, scalen_ref, nw_ref,
                   wa_ref, wb_ref, wg_ref, cw_ref,
                   p_ref, g_ref, gates_ref, h_scr, hn_scr, gn_scr, raw0, raw1, tail_scr, *, tiles_per_seq):
    i = pl.program_id(0)
    tm = x_ref.shape[0]
    tn = IN_TN
    H = IN_HIST
    raws = (raw0, raw1)
    nbuf = len(raws)
    col_scale = {0: M_HD ** -0.5, 5: A_QKD ** -0.5 * LOG2E}

    def normed(xr, shr, scr):
        x = xr[...]
        y = x * lax.rsqrt(jnp.mean(x * x, axis=-1, keepdims=True) + EPS)
        return (y * (nw_ref[...] * (1.0 + scr[0])) + shr[0]).astype(BF16)

    @pl.when(i == 0)
    def _():
        h = normed(x_ref, shift_ref, scale_ref)
        h_scr[...] = h
        gates_ref[...] = _nt_dot(h, wg_ref[...])
        tail_scr[...] = jnp.zeros_like(tail_scr)

    @pl.when(i > 0)
    def _():
        h_scr[...] = hn_scr[...]
        gates_ref[...] = gn_scr[...]

    first = (i % tiles_per_seq) == 0

    def matmul(j, buf):
        raw = raws[buf]

        def dot_cols(w_ref, c0, width):
            return _nt_dot(h_scr[...], w_ref[c0:c0 + width, :])

        if j in (3, 4):
            half = tn // 2
            raw[H:tm + H, 0:half] = dot_cols(wa_ref, 3 * tn + (j - 3) * half, half)
            raw[H:tm + H, half:tn] = dot_cols(wa_ref, 4 * tn + (j - 3) * half, half)
        elif j < 3:
            raw[H:tm + H, :] = dot_cols(wa_ref, j * tn, tn)
        else:
            raw[H:tm + H, :] = dot_cols(wb_ref, (j - 5) * tn, tn)

    def tail(j, buf):
        raw = raws[buf]
        if j in (3, 4):
            o, z = raw[H:tm + H, 0:tn // 2], raw[H:tm + H, tn // 2:tn]
            g_ref[:, (j - 3) * (tn // 2):(j - 2) * (tn // 2)] = (_sigmoid(o) * (z * _sigmoid(z))).astype(BF16)
            return
        if j in (0, 1):
            raw[0:H, :] = jnp.where(first, 0.0, tail_scr[j])
            y = raw[H:tm + H, :] * cw_ref[j, CONV_K - 1:CONV_K, :]
            for jj in range(CONV_K - 1):
                off = H - (CONV_K - 1) + jj
                y = y + raw[off:off + tm, :] * cw_ref[j, jj:jj + 1, :]
            tail_scr[j] = raw[tm:tm + H, :]
            y = y * _sigmoid(y)
        elif j == 8:
            a = raw[H:tm + H, :]
            y = a * _sigmoid(a)
        else:
            y = raw[H:tm + H, :]
        if j in col_scale:
            y = y * col_scale[j]
        slot = IN_P_SLOT[j]
        p_ref[:, slot * tn:(slot + 1) * tn] = y.astype(BF16)

    matmul(IN_ORDER[0], 0)
    for n in range(1, IN_TILES):
        tail(IN_ORDER[n - 1], (n - 1) % nbuf)
        matmul(IN_ORDER[n], n % nbuf)
    tail(IN_ORDER[-1], (IN_TILES - 1) % nbuf)
    hn = normed(xn_ref, shiftn_ref, scalen_ref)
    hn_scr[...] = hn
    gn_scr[...] = _nt_dot(hn, wg_ref[...])


def _inproj(x2d, mod3, norm_w, wa, wb, w_gate, conv_w, B, S, tm=512):
    M, D = x2d.shape
    assert S % tm == 0
    tn = IN_TN
    tiles_per_seq = S // tm
    last = M // tm - 1

    def nxt(i):
        return jnp.minimum(i + 1, last)

    return pl.pallas_call(
        functools.partial(_inproj_kernel, tiles_per_seq=tiles_per_seq),
        grid=(M // tm,),
        in_specs=[
            pl.BlockSpec((tm, D), lambda i: (i, 0)),
            pl.BlockSpec((tm, D), lambda i: (nxt(i), 0)),
            pl.BlockSpec((1, 1, D), lambda i: (i // tiles_per_seq, 0, 0)),
            pl.BlockSpec((1, 1, D), lambda i: (B + i // tiles_per_seq, 0, 0)),
            pl.BlockSpec((1, 1, D), lambda i: (nxt(i) // tiles_per_seq, 0, 0)),
            pl.BlockSpec((1, 1, D), lambda i: (B + nxt(i) // tiles_per_seq, 0, 0)),
            pl.BlockSpec((1, D), lambda i: (0, 0)),
            pl.BlockSpec((5 * tn, D), lambda i: (0, 0), pipeline_mode=pl.Buffered(1)),
            pl.BlockSpec((4 * tn, D), lambda i: (0, 0), pipeline_mode=pl.Buffered(1)),
            pl.BlockSpec((GATE_PAD, D), lambda i: (0, 0)),
            pl.BlockSpec((2, CONV_K, tn), lambda i: (0, 0, 0)),
        ],
        out_specs=[
            pl.BlockSpec((tm, P_COLS), lambda i: (i, 0)),
            pl.BlockSpec((tm, M_WIDTH), lambda i: (i, 0)),
            pl.BlockSpec((tm, GATE_PAD), lambda i: (i, 0)),
        ],
        out_shape=[
            jax.ShapeDtypeStruct((M, P_COLS), BF16),
            jax.ShapeDtypeStruct((M, M_WIDTH), BF16),
            jax.ShapeDtypeStruct((M, GATE_PAD), F32),
        ],
        scratch_shapes=[
            pltpu.VMEM((tm, D), BF16),
            pltpu.VMEM((tm, D), BF16),
            pltpu.VMEM((tm, GATE_PAD), F32),
            pltpu.VMEM((tm + IN_HIST, tn), F32),
            pltpu.VMEM((tm + IN_HIST, tn), F32),
            pltpu.VMEM((2, IN_HIST, tn), F32),
        ],
        compiler_params=pltpu.CompilerParams(
            dimension_semantics=("arbitrary",), vmem_limit_bytes=VMEM_LIMIT),
        name="inproj",
    )(x2d, x2d, mod3, mod3, mod3, mod3, norm_w, wa, wb, w_gate, conv_w)


def _mlstm_kernel(q_ref, k_ref, v_ref, og_ref, g_ref, nw_ref, gb_ref, o_ref, ct_scr, m_scr):
    L = M_CHUNK
    NB = q_ref.shape[0]
    c = pl.program_id(1)

    @pl.when(c == 0)
    def _():
        ct_scr[...] = jnp.zeros_like(ct_scr)
        m_scr[...] = jnp.zeros_like(m_scr)

    row = lax.broadcasted_iota(jnp.int32, (L, GATE_PAD), 0)

    def gate_math(bb):
        g = g_ref[bb] + gb_ref[...]
        lf = jnp.minimum(g, 0.0) - jnp.log1p(jnp.exp(-jnp.abs(g)))
        bc = lf
        sh = 1
        while sh < L:
            bc = bc + jnp.where(row >= sh, pltpu.roll(bc, sh, 0), 0.0)
            sh *= 2
        bsh = pltpu.roll(bc, GATE_PAD - M_HEADS, 1)
        a = g - bsh
        cm = a
        sh = 1
        while sh < L:
            cm = jnp.maximum(cm, jnp.where(row >= sh, pltpu.roll(cm, sh, 0), -jnp.inf))
            sh *= 2
        m_prev = m_scr[bb]
        mm = jnp.maximum(cm, m_prev)
        iw = jnp.exp(m_prev - mm)
        eneg = jnp.exp(-(bsh + mm))
        m_scr[bb] = bsh[L - 1:L, :] + mm[L - 1:L, :]
        return dict(a=a, a_t=a.T, mm=mm, iw=iw, eneg=eneg)

    gm = [gate_math(bb) for bb in range(NB)]

    tri = (lax.broadcasted_iota(jnp.int32, (L, L), 1) <= lax.broadcasted_iota(jnp.int32, (L, L), 0))
    ones_col = (lax.broadcasted_iota(jnp.int32, (L, M_AUG - M_HD), 1) == 0).astype(BF16)

    chains = [(bb, h) for bb in range(NB) for h in range(M_HEADS)]
    cols = [slice(h * M_HD, (h + 1) * M_HD) for h in range(M_HEADS)]
    qs = {ch: q_ref[ch[0], :, cols[ch[1]]] for ch in chains}
    ks = {ch: k_ref[ch[0], :, cols[ch[1]]] for ch in chains}
    v_aug = {ch: jnp.concatenate([v_ref[ch[0], :, cols[ch[1]]], ones_col], axis=1) for ch in chains}
    cts = {ch: ct_scr[ch[0], ch[1]] for ch in chains}

    s_raw = {ch: _nt_dot(qs[ch], ks[ch]) for ch in chains}
    r_state = {ch: jnp.dot(qs[ch], cts[ch].astype(BF16), preferred_element_type=F32) for ch in chains}
    for ch in chains:
        bb, h = ch
        m_last = gm[bb]["mm"][L - 1:L, h:h + 1]
        w_row = jnp.exp(gm[bb]["a_t"][h:h + 1, :] - m_last)
        decay = gm[bb]["iw"][L - 1:L, h:h + 1]
        ktw = (ks[ch].astype(F32).T * w_row).astype(BF16)
        ct_scr[bb, h] = decay * cts[ch] + jnp.dot(ktw, v_aug[ch], preferred_element_type=F32)

    s16 = {}
    for ch in chains:
        bb, h = ch
        dw = jnp.where(tri, jnp.exp(gm[bb]["a_t"][h:h + 1, :] - gm[bb]["mm"][:, h:h + 1]), 0.0)
        s16[ch] = (s_raw[ch] * dw).astype(BF16)
    rs = {ch: gm[ch[0]]["iw"][:, ch[1]:ch[1] + 1] * r_state[ch]
          + jnp.dot(s16[ch], v_aug[ch], preferred_element_type=F32) for ch in chains}

    for ch in chains:
        bb, h = ch
        num = rs[ch][:, 0:M_HD]
        d = jnp.maximum(jnp.abs(rs[ch][:, M_HD:M_HD + 1]), gm[bb]["eneg"][:, h:h + 1])
        hc = num - jnp.mean(num, axis=-1, keepdims=True)
        var = jnp.mean(hc * hc, axis=-1, keepdims=True)
        hn = hc * lax.rsqrt(var + EPS * d * d) * nw_ref[:, cols[h]]
        o_ref[bb, :, cols[h]] = (hn * og_ref[bb, :, cols[h]].astype(F32)).astype(BF16)


def _mlstm(p3, og3, gates3, m_norm_w, gbias):
    B, S, _ = p3.shape
    L = M_CHUNK
    W = M_WIDTH
    NB = M_NB

    def col(cb):
        return pl.BlockSpec((NB, L, W), lambda b, c, cb=cb: (b, c, cb))

    return pl.pallas_call(
        _mlstm_kernel,
        grid=(B // NB, S // L),
        in_specs=[
            col(0), col(1), col(2),
            pl.BlockSpec((NB, L, W), lambda b, c: (b, c, 0)),
            pl.BlockSpec((NB, L, GATE_PAD), lambda b, c: (b, c, 0)),
            pl.BlockSpec((1, W), lambda b, c: (0, 0)),
            pl.BlockSpec((1, GATE_PAD), lambda b, c: (0, 0)),
        ],
        out_specs=pl.BlockSpec((NB, L, W), lambda b, c: (b, c, 0)),
        out_shape=jax.ShapeDtypeStruct((B, S, W), BF16),
        scratch_shapes=[
            pltpu.VMEM((NB, M_HEADS, M_HD, M_AUG), F32),
            pltpu.VMEM((NB, 1, GATE_PAD), F32),
        ],
        compiler_params=pltpu.CompilerParams(
            dimension_semantics=("arbitrary", "arbitrary"), vmem_limit_bytes=VMEM_LIMIT),
        name="mlstm",
    )(p3, p3, p3, og3, gates3, m_norm_w, gbias)


def _bucket_tiles():
    T = ATT_BLK
    kk = np.arange(T)[:, None]
    qq = np.arange(T)[None, :]
    out = []
    for base in (0, T):
        n = base + qq - kk
        max_exact = N_BUCKETS // 2
        nf = np.maximum(n, 1).astype(np.float64)
        large = max_exact + (np.log(nf / max_exact) / math.log(MAX_DIST / max_exact)
                             * (N_BUCKETS - max_exact)).astype(np.int64)
        large = np.minimum(large, N_BUCKETS - 1)
        bk = np.where(n < max_exact, n, large)
        bk = np.where(n < 0, -1, bk)
        out.append(bk)
    return np.stack(out).astype(np.int32)


def _attn_kernel(relb_ref, bucket_ref, lam_ref, anw_ref, q_ref, k_ref, v_ref, z_ref, o_ref,
                 bias_scr, vt_scr, qq_scr, s0_scr, s1_scr, acc_scr):
    T = ATT_BLK
    HB = ATT_HB
    S = k_ref.shape[1]
    NQ = S // T
    hg = pl.program_id(0)
    b = pl.program_id(1)
    heads = [slice(hb * A_VHD, (hb + 1) * A_VHD) for hb in range(HB)]

    @pl.when(b == 0)
    def _():
        for hb in range(HB):
            h = hg * HB + hb
            far = relb_ref[N_BUCKETS - 1, h]
            for t in range(2):
                bk = bucket_ref[t]
                tile = jnp.zeros((T, T), F32)
                for bb in range(N_BUCKETS - 1):
                    tile = jnp.where(bk == bb, (relb_ref[bb, h] - far) * LOG2E, tile)
                tile = jnp.where(bk < 0, NEG, tile)
                bias_scr[hb, t] = jnp.concatenate([tile, tile], axis=1)

    lane = lax.broadcasted_iota(jnp.int32, (T, A_VHD), 1)
    ones_rows = (lax.broadcasted_iota(jnp.int32, (VT_ROWS - A_VHD, T), 0) == 0).astype(BF16)
    for hb in range(HB):
        for ci in range(NQ):
            rows = slice(ci * T, (ci + 1) * T)
            vt_scr[hb, 0:A_VHD, rows] = v_ref[0, rows, heads[hb]].astype(F32).T.astype(BF16)
            vt_scr[hb, A_VHD:VT_ROWS, rows] = ones_rows
            qs = q_ref[0, rows, heads[hb]]
            zero = jnp.zeros_like(qs)
            qq_scr[hb, ci, 0:T, :] = jnp.where(lane < A_QKD, qs, zero)
            qq_scr[hb, ci, T:2 * T, :] = jnp.where(lane >= A_QKD, qs, zero)
    acc_scr[...] = jnp.zeros_like(acc_scr)

    s_bufs = (s0_scr, s1_scr)

    def issue(buf, qb, kj, hb):
        s_bufs[buf][hb] = _nt_dot(k_ref[0, kj * T:(kj + 1) * T, heads[hb]], qq_scr[hb, qb])

    def run_stage(cur, nxt, kj, bias_idx, ms):
        start = kj * T
        out = []
        for hb in range(HB):
            s = s_bufs[cur][hb]
            if bias_idx is not None:
                s = s + bias_scr[hb, bias_idx]
            m_old = ms[hb]
            m_new = jnp.maximum(m_old, jnp.max(s, axis=0, keepdims=True))
            alpha = jnp.exp2(m_old - m_new)
            p = jnp.exp2(s - m_new).astype(BF16)
            if nxt is not None:
                issue(1 - cur, nxt[0], nxt[1], hb)
            pv = jnp.dot(vt_scr[hb, :, start:start + T], p, preferred_element_type=F32)
            acc_scr[hb] = alpha * acc_scr[hb] + pv
            out.append(m_new)
        return tuple(out)

    lam_p = lam_ref[...]
    lam = (jnp.exp(jnp.sum(lam_p[0:1] * lam_p[1:2], axis=-1, keepdims=True))
           - jnp.exp(jnp.sum(lam_p[2:3] * lam_p[3:4], axis=-1, keepdims=True)) + LAM_INIT)

    def finalize(qi):
        rows = slice(qi * T, (qi + 1) * T)
        for hb in range(HB):
            acc = acc_scr[hb]
            o = acc[0:A_VHD] / acc[A_VHD:A_VHD + 1]
            d = (o[:, 0:T] - lam * o[:, T:2 * T]).T
            y = d * lax.rsqrt(jnp.mean(d * d, axis=-1, keepdims=True) + EPS) * anw_ref[...] * (1.0 - LAM_INIT)
            o_ref[0, rows, heads[hb]] = (y * z_ref[0, rows, heads[hb]].astype(F32)).astype(BF16)

    m_init = tuple(jnp.full((1, 2 * T), NEG, F32) for _ in range(HB))

    pairs = [(qi, kj) for qi in range(NQ) for kj in range(qi + 1)]
    for hb in range(HB):
        issue(0, 0, 0, hb)
    ms = m_init
    for t, (qi, kj) in enumerate(pairs):
        nxt = pairs[t + 1] if t + 1 < len(pairs) else None
        bias_idx = 0 if kj == qi else (1 if kj == qi - 1 else None)
        ms = run_stage(t % 2, nxt, kj, bias_idx, ms)
        if kj == qi:
            finalize(qi)
            ms = m_init


def _attn(proj3, rel_bias, lam_p, a_norm_w):
    B, S, _ = proj3.shape
    T = ATT_BLK
    HB = ATT_HB
    W = HB * A_VHD
    q0 = 3 * IN_TN // W
    k0 = q0 + A_HEADS // HB
    v0 = k0 + A_HEADS // HB
    z0 = v0 + A_HEADS // HB
    buckets = jnp.asarray(_bucket_tiles())
    return pl.pallas_call(
        _attn_kernel,
        grid=(A_HEADS // HB, B),
        in_specs=[
            pl.BlockSpec(memory_space=pltpu.SMEM),
            pl.BlockSpec((2, T, T), lambda h, b: (0, 0, 0)),
            pl.BlockSpec((4, A_QKD), lambda h, b: (0, 0)),
            pl.BlockSpec((1, A_VHD), lambda h, b: (0, 0)),
            pl.BlockSpec((1, S, W), lambda h, b: (b, 0, q0 + h)),
            pl.BlockSpec((1, S, W), lambda h, b: (b, 0, k0 + h)),
            pl.BlockSpec((1, S, W), lambda h, b: (b, 0, v0 + h)),
            pl.BlockSpec((1, S, W), lambda h, b: (b, 0, z0 + h)),
        ],
        out_specs=pl.BlockSpec((1, S, W), lambda h, b: (b, 0, h)),
        out_shape=jax.ShapeDtypeStruct((B, S, A_WIDTH), BF16),
        scratch_shapes=[
            pltpu.VMEM((HB, 2, T, 2 * T), F32),
            pltpu.VMEM((HB, VT_ROWS, S), BF16),
            pltpu.VMEM((HB, S // T, 2 * T, A_VHD), BF16),
            pltpu.VMEM((HB, T, 2 * T), F32),
            pltpu.VMEM((HB, T, 2 * T), F32),
            pltpu.VMEM((HB, VT_ROWS, 2 * T), F32),
        ],
        compiler_params=pltpu.CompilerParams(
            dimension_semantics=("arbitrary", "arbitrary"), vmem_limit_bytes=VMEM_LIMIT),
        name="attn",
    )(rel_bias, buckets, lam_p, a_norm_w, proj3, proj3, proj3, proj3)


def _outproj_kernel(hm_ref, ha_ref, x_ref, gate_ref, w_ref, fw_ref, o_ref, raw0, raw1):
    sub = raw0.shape[0]
    nsub = x_ref.shape[0] // sub
    raws = (raw0, raw1)

    def matmul(r, buf):
        rows = slice(r * sub, (r + 1) * sub)
        y = jnp.dot(hm_ref[rows, :], w_ref[0:M_WIDTH, :], preferred_element_type=F32)
        raws[buf][...] = y + jnp.dot(ha_ref[rows, :], w_ref[M_WIDTH:, :], preferred_element_type=F32)

    def tail(r, buf):
        rows = slice(r * sub, (r + 1) * sub)
        res = x_ref[rows, :] + gate_ref[0] * raws[buf][...]
        o_ref[rows, :] = res * lax.rsqrt(jnp.mean(res * res, axis=-1, keepdims=True) + EPS) * fw_ref[...]

    matmul(0, 0)
    for r in range(1, nsub):
        tail(r - 1, (r - 1) % 2)
        matmul(r, r % 2)
    tail(nsub - 1, (nsub - 1) % 2)


def _outproj(hm2d, ha2d, x2d, mod3, w_out, final_w, B, S, tm=1024, sub=512):
    M, D = x2d.shape
    assert S % tm == 0 and tm % sub == 0
    tiles_per_seq = S // tm
    return pl.pallas_call(
        _outproj_kernel,
        grid=(M // tm,),
        scratch_shapes=[pltpu.VMEM((sub, D), F32), pltpu.VMEM((sub, D), F32)],
        in_specs=[
            pl.BlockSpec((tm, M_WIDTH), lambda i: (i, 0)),
            pl.BlockSpec((tm, A_WIDTH), lambda i: (i, 0)),
            pl.BlockSpec((tm, D), lambda i: (i, 0)),
            pl.BlockSpec((1, 1, D), lambda i: (2 * B + i // tiles_per_seq, 0, 0)),
            pl.BlockSpec((M_WIDTH + A_WIDTH, D), lambda i: (0, 0)),
            pl.BlockSpec((1, D), lambda i: (0, 0)),
        ],
        out_specs=pl.BlockSpec((tm, D), lambda i: (i, 0)),
        out_shape=jax.ShapeDtypeStruct((M, D), F32),
        compiler_params=pltpu.CompilerParams(
            dimension_semantics=("arbitrary",), vmem_limit_bytes=VMEM_LIMIT),
        name="outproj",
    )(hm2d, ha2d, x2d, mod3, w_out, final_w)


def kernel(x, c, norm_w, w_ada, b_ada, w_in, b_i, b_f, conv_q_w, conv_k_w, m_norm_w,
           lambda_q1, lambda_k1, lambda_q2, lambda_k2, a_norm_w, rel_bias, w_out, final_norm_w):
    B, S, D = x.shape
    assert norm_w.shape[0] == 1, "single layer only"
    assert D == D_MODEL and w_in.shape[2] == GATE_COL0 + 2 * M_HEADS + 4 * A_WIDTH
    assert B % M_NB == 0 and S % M_CHUNK == 0 and S % ATT_BLK == 0
    x2d = x.reshape(B * S, D)

    mod = _adaln(c, w_ada[0], b_ada[0])
    mod3 = mod.reshape(3 * B, 1, D)

    w_t = jnp.swapaxes(w_in, 1, 2)[0]
    a0 = GATE_COL0 + 2 * M_HEADS
    wa = w_t[:GATE_COL0].astype(BF16)
    wb = w_t[a0:].astype(BF16)
    w_gate = jnp.pad(w_t[GATE_COL0:a0], ((0, GATE_PAD - 2 * M_HEADS), (0, 0))).astype(BF16)
    conv_w = jnp.stack([conv_q_w[0], conv_k_w[0]])
    p, og, gates = _inproj(x2d, mod3, norm_w, wa, wb, w_gate, conv_w, B, S)
    p3 = p.reshape(B, S, P_COLS)
    gates3 = gates.reshape(B, S, GATE_PAD)

    gbias = jnp.pad(jnp.concatenate([b_i[0], b_f[0]]), (0, GATE_PAD - 2 * M_HEADS)).reshape(1, GATE_PAD)
    hm = _mlstm(p3, og.reshape(B, S, M_WIDTH), gates3, m_norm_w, gbias)

    lam_p = jnp.stack([lambda_q1[0], lambda_k1[0], lambda_q2[0], lambda_k2[0]])
    ha = _attn(p3, rel_bias, lam_p, a_norm_w)

    out = _outproj(hm.reshape(B * S, M_WIDTH), ha.reshape(B * S, A_WIDTH), x2d, mod3,
                   w_out[0].astype(BF16), final_norm_w.reshape(1, D), B, S)
    return out.reshape(B, S, D)
```

```python
import functools
import math

import numpy as np
import jax
import jax.numpy as jnp
from jax import lax
from jax.experimental import pallas as pl
from jax.experimental.pallas import tpu as pltpu

F32 = jnp.float32
BF16 = jnp.bfloat16

D_MODEL = 1024
M_WIDTH = 1024
M_HEADS = 4
M_HD = 256
CONV_K = 4
A_WIDTH = 1024
A_HEADS = 8
A_VHD = 128
A_QKD = 64
N_BUCKETS = 32
MAX_DIST = 128
EPS = 1e-6
LAM_INIT = 0.8 - 0.6 * math.exp(-0.3 * 0)

V7X_LANES = 128
V7X_SUBLANES = 8
V7X_BF16_SUBLANES = 16
V7X_MXU_DIM = 256
V7X_VMEM_BYTES = 64 * 1024 * 1024
VMEM_LIMIT = V7X_VMEM_BYTES - 8 * 1024 * 1024

GATE_COL0 = 5 * M_WIDTH
GATE_PAD = V7X_LANES

M_CHUNK = V7X_MXU_DIM
M_AUG = M_HD + V7X_LANES
M_NB = 4
ATT_BLK = V7X_MXU_DIM
ATT_HB = 1
VT_ROWS = A_VHD + V7X_BF16_SUBLANES
LOG2E = math.log2(math.e)
NEG = -1e30


def _nt_dot(a, b):
    return lax.dot_general(a, b, (((1,), (1,)), ((), ())), preferred_element_type=F32)


def _sigmoid(x):
    return 1.0 / (1.0 + jnp.exp(-x))


def _adaln_kernel(c_ref, w_ref, b_ref, o_ref):
    c = c_ref[...]
    sc = (c * _sigmoid(c)).astype(BF16)
    o_ref[0] = jnp.dot(sc, w_ref[...].astype(BF16), preferred_element_type=F32) + b_ref[0]


def _adaln(c, w_ada, b_ada):
    B, D = c.shape
    return pl.pallas_call(
        _adaln_kernel,
        grid=(3,),
        in_specs=[
            pl.BlockSpec((B, D), lambda j: (0, 0)),
            pl.BlockSpec((D, D), lambda j: (0, j)),
            pl.BlockSpec((1, 1, D), lambda j: (j, 0, 0)),
        ],
        out_specs=pl.BlockSpec((1, B, D), lambda j: (j, 0, 0)),
        out_shape=jax.ShapeDtypeStruct((3, B, D), F32),
        compiler_params=pltpu.CompilerParams(
            dimension_semantics=("arbitrary",), vmem_limit_bytes=VMEM_LIMIT),
        name="adaln",
    )(c, w_ada, b_ada.reshape(3, 1, D))


IN_TN = 1024
IN_TILES = 9
P_TILES = 7
P_COLS = P_TILES * IN_TN
IN_ORDER = (0, 1, 3, 4, 8, 2, 5, 6, 7)
IN_P_SLOT = {0: 0, 1: 1, 2: 2, 5: 3, 6: 4, 7: 5, 8: 6}
IN_HIST = V7X_SUBLANES


def _inproj_kernel(x_ref, xn_ref, shift_ref, scale_ref, shiftn_ref, scalen_ref, nw_ref,
                   wa_ref, wb_ref, wg_ref, cw_ref,
                   p_ref, g_ref, gates_ref, h_scr, hn_scr, gn_scr, raw0, raw1, tail_scr, *, tiles_per_seq):
    i = pl.program_id(0)
    tm = x_ref.shape[0]
    tn = IN_TN
    H = IN_HIST
    raws = (raw0, raw1)
    nbuf = len(raws)
    col_scale = {0: M_HD ** -0.5, 5: A_QKD ** -0.5 * LOG2E}

    def normed(xr, shr, scr):
        x = xr[...]
        y = x * lax.rsqrt(jnp.mean(x * x, axis=-1, keepdims=True) + EPS)
        return (y * (nw_ref[...] * (1.0 + scr[0])) + shr[0]).astype(BF16)

    @pl.when(i == 0)
    def _():
        h = normed(x_ref, shift_ref, scale_ref)
        h_scr[...] = h
        gates_ref[...] = _nt_dot(h, wg_ref[...])
        tail_scr[...] = jnp.zeros_like(tail_scr)

    @pl.when(i > 0)
    def _():
        h_scr[...] = hn_scr[...]
        gates_ref[...] = gn_scr[...]

    first = (i % tiles_per_seq) == 0

    def matmul(j, buf):
        raw = raws[buf]

        def dot_cols(w_ref, c0, width):
            return _nt_dot(h_scr[...], w_ref[c0:c0 + width, :])

        if j in (3, 4):
            half = tn // 2
            raw[H:tm + H, 0:half] = dot_cols(wa_ref, 3 * tn + (j - 3) * half, half)
            raw[H:tm + H, half:tn] = dot_cols(wa_ref, 4 * tn + (j - 3) * half, half)
        elif j < 3:
            raw[H:tm + H, :] = dot_cols(wa_ref, j * tn, tn)
        else:
            raw[H:tm + H, :] = dot_cols(wb_ref, (j - 5) * tn, tn)

    def tail(j, buf):
        raw = raws[buf]
        if j in (3, 4):
            o, z = raw[H:tm + H, 0:tn // 2], raw[H:tm + H, tn // 2:tn]
            g_ref[:, (j - 3) * (tn // 2):(j - 2) * (tn // 2)] = (_sigmoid(o) * (z * _sigmoid(z))).astype(BF16)
            return
        if j in (0, 1):
            raw[0:H, :] = jnp.where(first, 0.0, tail_scr[j])
            y = raw[H:tm + H, :] * cw_ref[j, CONV_K - 1:CONV_K, :]
            for jj in range(CONV_K - 1):
                off = H - (CONV_K - 1) + jj
                y = y + raw[off:off + tm, :] * cw_ref[j, jj:jj + 1, :]
            tail_scr[j] = raw[tm:tm + H, :]
            y = y * _sigmoid(y)
        elif j == 8:
            a = raw[H:tm + H, :]
            y = a * _sigmoid(a)
        else:
            y = raw[H:tm + H, :]
        if j in col_scale:
            y = y * col_scale[j]
        slot = IN_P_SLOT[j]
        p_ref[:, slot * tn:(slot + 1) * tn] = y.astype(BF16)

    matmul(IN_ORDER[0], 0)
    for n in range(1, IN_TILES):
        tail(IN_ORDER[n - 1], (n - 1) % nbuf)
        matmul(IN_ORDER[n], n % nbuf)
    tail(IN_ORDER[-1], (IN_TILES - 1) % nbuf)
    hn = normed(xn_ref, shiftn_ref, scalen_ref)
    hn_scr[...] = hn
    gn_scr[...] = _nt_dot(hn, wg_ref[...])


def _inproj(x2d, mod3, norm_w, wa, wb, w_gate, conv_w, B, S, tm=512):
    M, D = x2d.shape
    assert S % tm == 0
    tn = IN_TN
    tiles_per_seq = S // tm
    last = M // tm - 1

    def nxt(i):
        return jnp.minimum(i + 1, last)

    return pl.pallas_call(
        functools.partial(_inproj_kernel, tiles_per_seq=tiles_per_seq),
        grid=(M // tm,),
        in_specs=[
            pl.BlockSpec((tm, D), lambda i: (i, 0)),
            pl.BlockSpec((tm, D), lambda i: (nxt(i), 0)),
            pl.BlockSpec((1, 1, D), lambda i: (i // tiles_per_seq, 0, 0)),
            pl.BlockSpec((1, 1, D), lambda i: (B + i // tiles_per_seq, 0, 0)),
            pl.BlockSpec((1, 1, D), lambda i: (nxt(i) // tiles_per_seq, 0, 0)),
            pl.BlockSpec((1, 1, D), lambda i: (B + nxt(i) // tiles_per_seq, 0, 0)),
            pl.BlockSpec((1, D), lambda i: (0, 0)),
            pl.BlockSpec((5 * tn, D), lambda i: (0, 0), pipeline_mode=pl.Buffered(1)),
            pl.BlockSpec((4 * tn, D), lambda i: (0, 0), pipeline_mode=pl.Buffered(1)),
            pl.BlockSpec((GATE_PAD, D), lambda i: (0, 0)),
            pl.BlockSpec((2, CONV_K, tn), lambda i: (0, 0, 0)),
        ],
        out_specs=[
            pl.BlockSpec((tm, P_COLS), lambda i: (i, 0)),
            pl.BlockSpec((tm, M_WIDTH), lambda i: (i, 0)),
            pl.BlockSpec((tm, GATE_PAD), lambda i: (i, 0)),
        ],
        out_shape=[
            jax.ShapeDtypeStruct((M, P_COLS), BF16),
            jax.ShapeDtypeStruct((M, M_WIDTH), BF16),
            jax.ShapeDtypeStruct((M, GATE_PAD), F32),
        ],
        scratch_shapes=[
            pltpu.VMEM((tm, D), BF16),
            pltpu.VMEM((tm, D), BF16),
            pltpu.VMEM((tm, GATE_PAD), F32),
            pltpu.VMEM((tm + IN_HIST, tn), F32),
            pltpu.VMEM((tm + IN_HIST, tn), F32),
            pltpu.VMEM((2, IN_HIST, tn), F32),
        ],
        compiler_params=pltpu.CompilerParams(
            dimension_semantics=("arbitrary",), vmem_limit_bytes=VMEM_LIMIT),
        name="inproj",
    )(x2d, x2d, mod3, mod3, mod3, mod3, norm_w, wa, wb, w_gate, conv_w)


def _mlstm_kernel(q_ref, k_ref, v_ref, og_ref, g_ref, nw_ref, gb_ref, o_ref, ct_scr, m_scr):
    L = M_CHUNK
    NB = q_ref.shape[0]
    c = pl.program_id(1)

    @pl.when(c == 0)
    def _():
        ct_scr[...] = jnp.zeros_like(ct_scr)
        m_scr[...] = jnp.zeros_like(m_scr)

    row = lax.broadcasted_iota(jnp.int32, (L, GATE_PAD), 0)

    def gate_math(bb):
        g = g_ref[bb] + gb_ref[...]
        lf = jnp.minimum(g, 0.0) - jnp.log1p(jnp.exp(-jnp.abs(g)))
        bc = lf
        sh = 1
        while sh < L:
            bc = bc + jnp.where(row >= sh, pltpu.roll(bc, sh, 0), 0.0)
            sh *= 2
        bsh = pltpu.roll(bc, GATE_PAD - M_HEADS, 1)
        a = g - bsh
        cm = a
        sh = 1
        while sh < L:
            cm = jnp.maximum(cm, jnp.where(row >= sh, pltpu.roll(cm, sh, 0), -jnp.inf))
            sh *= 2
        m_prev = m_scr[bb]
        mm = jnp.maximum(cm, m_prev)
        iw = jnp.exp(m_prev - mm)
        eneg = jnp.exp(-(bsh + mm))
        m_scr[bb] = bsh[L - 1:L, :] + mm[L - 1:L, :]
        return dict(a=a, a_t=a.T, mm=mm, iw=iw, eneg=eneg)

    gm = [gate_math(bb) for bb in range(NB)]

    tri = (lax.broadcasted_iota(jnp.int32, (L, L), 1) <= lax.broadcasted_iota(jnp.int32, (L, L), 0))
    ones_col = (lax.broadcasted_iota(jnp.int32, (L, M_AUG - M_HD), 1) == 0).astype(BF16)

    chains = [(bb, h) for bb in range(NB) for h in range(M_HEADS)]
    cols = [slice(h * M_HD, (h + 1) * M_HD) for h in range(M_HEADS)]
    qs = {ch: q_ref[ch[0], :, cols[ch[1]]] for ch in chains}
    ks = {ch: k_ref[ch[0], :, cols[ch[1]]] for ch in chains}
    v_aug = {ch: jnp.concatenate([v_ref[ch[0], :, cols[ch[1]]], ones_col], axis=1) for ch in chains}
    cts = {ch: ct_scr[ch[0], ch[1]] for ch in chains}

    s_raw = {ch: _nt_dot(qs[ch], ks[ch]) for ch in chains}
    r_state = {ch: jnp.dot(qs[ch], cts[ch].astype(BF16), preferred_element_type=F32) for ch in chains}
    for ch in chains:
        bb, h = ch
        m_last = gm[bb]["mm"][L - 1:L, h:h + 1]
        w_row = jnp.exp(gm[bb]["a_t"][h:h + 1, :] - m_last)
        decay = gm[bb]["iw"][L - 1:L, h:h + 1]
        ktw = (ks[ch].astype(F32).T * w_row).astype(BF16)
        ct_scr[bb, h] = decay * cts[ch] + jnp.dot(ktw, v_aug[ch], preferred_element_type=F32)

    s16 = {}
    for ch in chains:
        bb, h = ch
        dw = jnp.where(tri, jnp.exp(gm[bb]["a_t"][h:h + 1, :] - gm[bb]["mm"][:, h:h + 1]), 0.0)
        s16[ch] = (s_raw[ch] * dw).astype(BF16)
    rs = {ch: gm[ch[0]]["iw"][:, ch[1]:ch[1] + 1] * r_state[ch]
          + jnp.dot(s16[ch], v_aug[ch], preferred_element_type=F32) for ch in chains}

    for ch in chains:
        bb, h = ch
        num = rs[ch][:, 0:M_HD]
        d = jnp.maximum(jnp.abs(rs[ch][:, M_HD:M_HD + 1]), gm[bb]["eneg"][:, h:h + 1])
        hc = num - jnp.mean(num, axis=-1, keepdims=True)
        var = jnp.mean(hc * hc, axis=-1, keepdims=True)
        hn = hc * lax.rsqrt(var + EPS * d * d) * nw_ref[:, cols[h]]
        o_ref[bb, :, cols[h]] = (hn * og_ref[bb, :, cols[h]].astype(F32)).astype(BF16)


def _mlstm(p3, og3, gates3, m_norm_w, gbias):
    B, S, _ = p3.shape
    L = M_CHUNK
    W = M_WIDTH
    NB = M_NB

    def col(cb):
        return pl.BlockSpec((NB, L, W), lambda b, c, cb=cb: (b, c, cb))

    return pl.pallas_call(
        _mlstm_kernel,
        grid=(B // NB, S // L),
        in_specs=[
            col(0), col(1), col(2),
            pl.BlockSpec((NB, L, W), lambda b, c: (b, c, 0)),
            pl.BlockSpec((NB, L, GATE_PAD), lambda b, c: (b, c, 0)),
            pl.BlockSpec((1, W), lambda b, c: (0, 0)),
            pl.BlockSpec((1, GATE_PAD), lambda b, c: (0, 0)),
        ],
        out_specs=pl.BlockSpec((NB, L, W), lambda b, c: (b, c, 0)),
        out_shape=jax.ShapeDtypeStruct((B, S, W), BF16),
        scratch_shapes=[
            pltpu.VMEM((NB, M_HEADS, M_HD, M_AUG), F32),
            pltpu.VMEM((NB, 1, GATE_PAD), F32),
        ],
        compiler_params=pltpu.CompilerParams(
            dimension_semantics=("arbitrary", "arbitrary"), vmem_limit_bytes=VMEM_LIMIT),
        name="mlstm",
    )(p3, p3, p3, og3, gates3, m_norm_w, gbias)


def _bucket_tiles():
    T = ATT_BLK
    kk = np.arange(T)[:, None]
    qq = np.arange(T)[None, :]
    out = []
    for base in (0, T):
        n = base + qq - kk
        max_exact = N_BUCKETS // 2
        nf = np.maximum(n, 1).astype(np.float64)
        large = max_exact + (np.log(nf / max_exact) / math.log(MAX_DIST / max_exact)
                             * (N_BUCKETS - max_exact)).astype(np.int64)
        large = np.minimum(large, N_BUCKETS - 1)
        bk = np.where(n < max_exact, n, large)
        bk = np.where(n < 0, -1, bk)
        out.append(bk)
    return np.stack(out).astype(np.int32)


def _attn_kernel(relb_ref, bucket_ref, lam_ref, anw_ref, q_ref, k_ref, v_ref, z_ref, o_ref,
                 bias_scr, vt_scr, qq_scr, s0_scr, s1_scr, acc_scr):
    T = ATT_BLK
    HB = ATT_HB
    S = k_ref.shape[1]
    NQ = S // T
    hg = pl.program_id(0)
    b = pl.program_id(1)
    heads = [slice(hb * A_VHD, (hb + 1) * A_VHD) for hb in range(HB)]

    @pl.when(b == 0)
    def _():
        for hb in range(HB):
            h = hg * HB + hb
            far = relb_ref[N_BUCKETS - 1, h]
            for t in range(2):
                bk = bucket_ref[t]
                tile = jnp.zeros((T, T), F32)
                for bb in range(N_BUCKETS - 1):
                    tile = jnp.where(bk == bb, (relb_ref[bb, h] - far) * LOG2E, tile)
                tile = jnp.where(bk < 0, NEG, tile)
                bias_scr[hb, t] = jnp.concatenate([tile, tile], axis=1)

    lane = lax.broadcasted_iota(jnp.int32, (T, A_VHD), 1)
    ones_rows = (lax.broadcasted_iota(jnp.int32, (VT_ROWS - A_VHD, T), 0) == 0).astype(BF16)
    for hb in range(HB):
        for ci in range(NQ):
            rows = slice(ci * T, (ci + 1) * T)
            vt_scr[hb, 0:A_VHD, rows] = v_ref[0, rows, heads[hb]].astype(F32).T.astype(BF16)
            vt_scr[hb, A_VHD:VT_ROWS, rows] = ones_rows
            qs = q_ref[0, rows, heads[hb]]
            zero = jnp.zeros_like(qs)
            qq_scr[hb, ci, 0:T, :] = jnp.where(lane < A_QKD, qs, zero)
            qq_scr[hb, ci, T:2 * T, :] = jnp.where(lane >= A_QKD, qs, zero)
    acc_scr[...] = jnp.zeros_like(acc_scr)

    s_bufs = (s0_scr, s1_scr)

    def issue(buf, qb, kj, hb):
        s_bufs[buf][hb] = _nt_dot(k_ref[0, kj * T:(kj + 1) * T, heads[hb]], qq_scr[hb, qb])

    def run_stage(cur, nxt, kj, bias_idx, ms):
        start = kj * T
        out = []
        for hb in range(HB):
            s = s_bufs[cur][hb]
            if bias_idx is not None:
                s = s + bias_scr[hb, bias_idx]
            m_old = ms[hb]
            m_new = jnp.maximum(m_old, jnp.max(s, axis=0, keepdims=True))
            alpha = jnp.exp2(m_old - m_new)
            p = jnp.exp2(s - m_new).astype(BF16)
            if nxt is not None:
                issue(1 - cur, nxt[0], nxt[1], hb)
            pv = jnp.dot(vt_scr[hb, :, start:start + T], p, preferred_element_type=F32)
            acc_scr[hb] = alpha * acc_scr[hb] + pv
            out.append(m_new)
        return tuple(out)

    lam_p = lam_ref[...]
    lam = (jnp.exp(jnp.sum(lam_p[0:1] * lam_p[1:2], axis=-1, keepdims=True))
           - jnp.exp(jnp.sum(lam_p[2:3] * lam_p[3:4], axis=-1, keepdims=True)) + LAM_INIT)

    def finalize(qi):
        rows = slice(qi * T, (qi + 1) * T)
        for hb in range(HB):
            acc = acc_scr[hb]
            o = acc[0:A_VHD] / acc[A_VHD:A_VHD + 1]
            d = (o[:, 0:T] - lam * o[:, T:2 * T]).T
            y = d * lax.rsqrt(jnp.mean(d * d, axis=-1, keepdims=True) + EPS) * anw_ref[...] * (1.0 - LAM_INIT)
            o_ref[0, rows, heads[hb]] = (y * z_ref[0, rows, heads[hb]].astype(F32)).astype(BF16)

    m_init = tuple(jnp.full((1, 2 * T), NEG, F32) for _ in range(HB))

    pairs = [(qi, kj) for qi in range(NQ) for kj in range(qi + 1)]
    for hb in range(HB):
        issue(0, 0, 0, hb)
    ms = m_init
    for t, (qi, kj) in enumerate(pairs):
        nxt = pairs[t + 1] if t + 1 < len(pairs) else None
        bias_idx = 0 if kj == qi else (1 if kj == qi - 1 else None)
        ms = run_stage(t % 2, nxt, kj, bias_idx, ms)
        if kj == qi:
            finalize(qi)
            ms = m_init


def _attn(proj3, rel_bias, lam_p, a_norm_w):
    B, S, _ = proj3.shape
    T = ATT_BLK
    HB = ATT_HB
    W = HB * A_VHD
    q0 = 3 * IN_TN // W
    k0 = q0 + A_HEADS // HB
    v0 = k0 + A_HEADS // HB
    z0 = v0 + A_HEADS // HB
    buckets = jnp.asarray(_bucket_tiles())
    return pl.pallas_call(
        _attn_kernel,
        grid=(A_HEADS // HB, B),
        in_specs=[
            pl.BlockSpec(memory_space=pltpu.SMEM),
            pl.BlockSpec((2, T, T), lambda h, b: (0, 0, 0)),
            pl.BlockSpec((4, A_QKD), lambda h, b: (0, 0)),
            pl.BlockSpec((1, A_VHD), lambda h, b: (0, 0)),
            pl.BlockSpec((1, S, W), lambda h, b: (b, 0, q0 + h)),
            pl.BlockSpec((1, S, W), lambda h, b: (b, 0, k0 + h)),
            pl.BlockSpec((1, S, W), lambda h, b: (b, 0, v0 + h)),
            pl.BlockSpec((1, S, W), lambda h, b: (b, 0, z0 + h)),
        ],
        out_specs=pl.BlockSpec((1, S, W), lambda h, b: (b, 0, h)),
        out_shape=jax.ShapeDtypeStruct((B, S, A_WIDTH), BF16),
        scratch_shapes=[
            pltpu.VMEM((HB, 2, T, 2 * T), F32),
            pltpu.VMEM((HB, VT_ROWS, S), BF16),
            pltpu.VMEM((HB, S // T, 2 * T, A_VHD), BF16),
            pltpu.VMEM((HB, T, 2 * T), F32),
            pltpu.VMEM((HB, T, 2 * T), F32),
            pltpu.VMEM((HB, VT_ROWS, 2 * T), F32),
        ],
        compiler_params=pltpu.CompilerParams(
            dimension_semantics=("arbitrary", "arbitrary"), vmem_limit_bytes=VMEM_LIMIT),
        name="attn",
    )(rel_bias, buckets, lam_p, a_norm_w, proj3, proj3, proj3, proj3)


OUT_SUBTILES = (512, 512)
def _outproj_kernel(hm_ref, ha_ref, x_ref, gate_ref, w_ref, fw_ref, o_ref, raw0, raw1):
    raws = (raw0, raw1)
    bounds = [0]
    for n in OUT_SUBTILES:
        bounds.append(bounds[-1] + n)
    assert bounds[-1] == x_ref.shape[0] and max(OUT_SUBTILES) <= raw0.shape[0]

    def matmul(r, buf):
        rows = slice(bounds[r], bounds[r + 1])
        y = jnp.dot(hm_ref[rows, :], w_ref[0:M_WIDTH, :], preferred_element_type=F32)
        raws[buf][0:OUT_SUBTILES[r], :] = y + jnp.dot(ha_ref[rows, :], w_ref[M_WIDTH:, :],
                                                      preferred_element_type=F32)

    def tail(r, buf):
        rows = slice(bounds[r], bounds[r + 1])
        res = x_ref[rows, :] + gate_ref[0] * raws[buf][0:OUT_SUBTILES[r], :]
        o_ref[rows, :] = res * lax.rsqrt(jnp.mean(res * res, axis=-1, keepdims=True) + EPS) * fw_ref[...]

    nsub = len(OUT_SUBTILES)
    matmul(0, 0)
    for r in range(1, nsub):
        tail(r - 1, (r - 1) % 2)
        matmul(r, r % 2)
    tail(nsub - 1, (nsub - 1) % 2)


def _outproj(hm2d, ha2d, x2d, mod3, w_out, final_w, B, S):
    M, D = x2d.shape
    tm = sum(OUT_SUBTILES)
    sub = max(OUT_SUBTILES)
    assert S % tm == 0
    tiles_per_seq = S // tm
    return pl.pallas_call(
        _outproj_kernel,
        grid=(M // tm,),
        scratch_shapes=[pltpu.VMEM((sub, D), F32), pltpu.VMEM((sub, D), F32)],
        in_specs=[
            pl.BlockSpec((tm, M_WIDTH), lambda i: (i, 0)),
            pl.BlockSpec((tm, A_WIDTH), lambda i: (i, 0)),
            pl.BlockSpec((tm, D), lambda i: (i, 0)),
            pl.BlockSpec((1, 1, D), lambda i: (2 * B + i // tiles_per_seq, 0, 0)),
            pl.BlockSpec((M_WIDTH + A_WIDTH, D), lambda i: (0, 0)),
            pl.BlockSpec((1, D), lambda i: (0, 0)),
        ],
        out_specs=pl.BlockSpec((tm, D), lambda i: (i, 0)),
        out_shape=jax.ShapeDtypeStruct((M, D), F32),
        compiler_params=pltpu.CompilerParams(
            dimension_semantics=("arbitrary",), vmem_limit_bytes=VMEM_LIMIT),
        name="outproj",
    )(hm2d, ha2d, x2d, mod3, w_out, final_w)


def kernel(x, c, norm_w, w_ada, b_ada, w_in, b_i, b_f, conv_q_w, conv_k_w, m_norm_w,
           lambda_q1, lambda_k1, lambda_q2, lambda_k2, a_norm_w, rel_bias, w_out, final_norm_w):
    B, S, D = x.shape
    assert norm_w.shape[0] == 1, "single layer only"
    assert D == D_MODEL and w_in.shape[2] == GATE_COL0 + 2 * M_HEADS + 4 * A_WIDTH
    assert B % M_NB == 0 and S % M_CHUNK == 0 and S % ATT_BLK == 0
    x2d = x.reshape(B * S, D)

    mod = _adaln(c, w_ada[0], b_ada[0])
    mod3 = mod.reshape(3 * B, 1, D)

    w_t = jnp.swapaxes(w_in, 1, 2)[0]
    a0 = GATE_COL0 + 2 * M_HEADS
    wa = w_t[:GATE_COL0].astype(BF16)
    wb = w_t[a0:].astype(BF16)
    w_gate = jnp.pad(w_t[GATE_COL0:a0], ((0, GATE_PAD - 2 * M_HEADS), (0, 0))).astype(BF16)
    conv_w = jnp.stack([conv_q_w[0], conv_k_w[0]])
    p, og, gates = _inproj(x2d, mod3, norm_w, wa, wb, w_gate, conv_w, B, S)
    p3 = p.reshape(B, S, P_COLS)
    gates3 = gates.reshape(B, S, GATE_PAD)

    gbias = jnp.pad(jnp.concatenate([b_i[0], b_f[0]]), (0, GATE_PAD - 2 * M_HEADS)).reshape(1, GATE_PAD)
    hm = _mlstm(p3, og.reshape(B, S, M_WIDTH), gates3, m_norm_w, gbias)

    lam_p = jnp.stack([lambda_q1[0], lambda_k1[0], lambda_q2[0], lambda_k2[0]])
    ha = _attn(p3, rel_bias, lam_p, a_norm_w)

    out = _outproj(hm.reshape(B * S, M_WIDTH), ha.reshape(B * S, A_WIDTH), x2d, mod3,
                   w_out[0].astype(BF16), final_norm_w.reshape(1, D), B, S)
    return out.reshape(B, S, D)
```

```python
import functools
import math

import numpy as np
import jax
import jax.numpy as jnp
from jax import lax
from jax.experimental import pallas as pl
from jax.experimental.pallas import tpu as pltpu

F32 = jnp.float32
BF16 = jnp.bfloat16

D_MODEL = 1024
M_WIDTH = 1024
M_HEADS = 4
M_HD = 256
CONV_K = 4
A_WIDTH = 1024
A_HEADS = 8
A_VHD = 128
A_QKD = 64
N_BUCKETS = 32
MAX_DIST = 128
EPS = 1e-6
LAM_INIT = 0.8 - 0.6 * math.exp(-0.3 * 0)

V7X_LANES = 128
V7X_SUBLANES = 8
V7X_BF16_SUBLANES = 16
V7X_MXU_DIM = 256
V7X_VMEM_BYTES = 64 * 1024 * 1024
VMEM_LIMIT = V7X_VMEM_BYTES - 8 * 1024 * 1024

GATE_COL0 = 5 * M_WIDTH
GATE_PAD = V7X_LANES

M_CHUNK = V7X_MXU_DIM
M_AUG = M_HD + V7X_LANES
M_NB = 4
ATT_BLK = V7X_MXU_DIM
ATT_HB = 2
VT_ROWS = A_VHD + V7X_BF16_SUBLANES
LOG2E = math.log2(math.e)
NEG = -1e30


def _nt_dot(a, b):
    return lax.dot_general(a, b, (((1,), (1,)), ((), ())), preferred_element_type=F32)


def _sigmoid(x):
    return 1.0 / (1.0 + jnp.exp(-x))


def _adaln_kernel(c_ref, w_ref, b_ref, o_ref):
    c = c_ref[...]
    sc = (c * _sigmoid(c)).astype(BF16)
    o_ref[0] = jnp.dot(sc, w_ref[...].astype(BF16), preferred_element_type=F32) + b_ref[0]


def _adaln(c, w_ada, b_ada):
    B, D = c.shape
    return pl.pallas_call(
        _adaln_kernel,
        grid=(3,),
        in_specs=[
            pl.BlockSpec((B, D), lambda j: (0, 0)),
            pl.BlockSpec((D, D), lambda j: (0, j)),
            pl.BlockSpec((1, 1, D), lambda j: (j, 0, 0)),
        ],
        out_specs=pl.BlockSpec((1, B, D), lambda j: (j, 0, 0)),
        out_shape=jax.ShapeDtypeStruct((3, B, D), F32),
        compiler_params=pltpu.CompilerParams(
            dimension_semantics=("arbitrary",), vmem_limit_bytes=VMEM_LIMIT),
        name="adaln",
    )(c, w_ada, b_ada.reshape(3, 1, D))


IN_TN = 1024
IN_TILES = 9
P_TILES = 7
P_COLS = P_TILES * IN_TN
IN_ORDER = (0, 1, 3, 4, 8, 2, 5, 6, 7)
IN_P_SLOT = {0: 0, 1: 1, 2: 2, 5: 3, 6: 4, 7: 5, 8: 6}
IN_HIST = V7X_SUBLANES


def _inproj_kernel(x_ref, xn_ref, shift_ref, scale_ref, shiftn_ref, scalen_ref, nw_ref,
                   wa_ref, wb_ref, wg_ref, cw_ref,
                   p_ref, g_ref, gates_ref, h_scr, hn_scr, gn_scr, raw0, raw1, tail_scr, *, tiles_per_seq):
    i = pl.program_id(0)
    tm = x_ref.shape[0]
    tn = IN_TN
    H = IN_HIST
    raws = (raw0, raw1)
    nbuf = len(raws)
    col_scale = {0: M_HD ** -0.5, 5: A_QKD ** -0.5 * LOG2E}

    def normed(xr, shr, scr):
        x = xr[...]
        y = x * lax.rsqrt(jnp.mean(x * x, axis=-1, keepdims=True) + EPS)
        return (y * (nw_ref[...] * (1.0 + scr[0])) + shr[0]).astype(BF16)

    @pl.when(i == 0)
    def _():
        h = normed(x_ref, shift_ref, scale_ref)
        h_scr[...] = h
        gates_ref[...] = _nt_dot(h, wg_ref[...])
        tail_scr[...] = jnp.zeros_like(tail_scr)

    @pl.when(i > 0)
    def _():
        h_scr[...] = hn_scr[...]
        gates_ref[...] = gn_scr[...]

    first = (i % tiles_per_seq) == 0

    def matmul(j, buf):
        raw = raws[buf]

        def dot_cols(w_ref, c0, width):
            return _nt_dot(h_scr[...], w_ref[c0:c0 + width, :])

        if j in (3, 4):
            half = tn // 2
            raw[H:tm + H, 0:half] = dot_cols(wa_ref, 3 * tn + (j - 3) * half, half)
            raw[H:tm + H, half:tn] = dot_cols(wa_ref, 4 * tn + (j - 3) * half, half)
        elif j < 3:
            raw[H:tm + H, :] = dot_cols(wa_ref, j * tn, tn)
        else:
            raw[H:tm + H, :] = dot_cols(wb_ref, (j - 5) * tn, tn)

    def tail(j, buf):
        raw = raws[buf]
        if j in (3, 4):
            o, z = raw[H:tm + H, 0:tn // 2], raw[H:tm + H, tn // 2:tn]
            g_ref[:, (j - 3) * (tn // 2):(j - 2) * (tn // 2)] = (_sigmoid(o) * (z * _sigmoid(z))).astype(BF16)
            return
        if j in (0, 1):
            raw[0:H, :] = jnp.where(first, 0.0, tail_scr[j])
            y = raw[H:tm + H, :] * cw_ref[j, CONV_K - 1:CONV_K, :]
            for jj in range(CONV_K - 1):
                off = H - (CONV_K - 1) + jj
                y = y + raw[off:off + tm, :] * cw_ref[j, jj:jj + 1, :]
            tail_scr[j] = raw[tm:tm + H, :]
            y = y * _sigmoid(y)
        elif j == 8:
            a = raw[H:tm + H, :]
            y = a * _sigmoid(a)
        else:
            y = raw[H:tm + H, :]
        if j in col_scale:
            y = y * col_scale[j]
        slot = IN_P_SLOT[j]
        p_ref[:, slot * tn:(slot + 1) * tn] = y.astype(BF16)

    matmul(IN_ORDER[0], 0)
    for n in range(1, IN_TILES):
        tail(IN_ORDER[n - 1], (n - 1) % nbuf)
        matmul(IN_ORDER[n], n % nbuf)
    tail(IN_ORDER[-1], (IN_TILES - 1) % nbuf)
    hn = normed(xn_ref, shiftn_ref, scalen_ref)
    hn_scr[...] = hn
    gn_scr[...] = _nt_dot(hn, wg_ref[...])


def _inproj(x2d, mod3, norm_w, wa, wb, w_gate, conv_w, B, S, tm=512):
    M, D = x2d.shape
    assert S % tm == 0
    tn = IN_TN
    tiles_per_seq = S // tm
    last = M // tm - 1

    def nxt(i):
        return jnp.minimum(i + 1, last)

    return pl.pallas_call(
        functools.partial(_inproj_kernel, tiles_per_seq=tiles_per_seq),
        grid=(M // tm,),
        in_specs=[
            pl.BlockSpec((tm, D), lambda i: (i, 0)),
            pl.BlockSpec((tm, D), lambda i: (nxt(i), 0)),
            pl.BlockSpec((1, 1, D), lambda i: (i // tiles_per_seq, 0, 0)),
            pl.BlockSpec((1, 1, D), lambda i: (B + i // tiles_per_seq, 0, 0)),
            pl.BlockSpec((1, 1, D), lambda i: (nxt(i) // tiles_per_seq, 0, 0)),
            pl.BlockSpec((1, 1, D), lambda i: (B + nxt(i) // tiles_per_seq, 0, 0)),
            pl.BlockSpec((1, D), lambda i: (0, 0)),
            pl.BlockSpec((5 * tn, D), lambda i: (0, 0), pipeline_mode=pl.Buffered(1)),
            pl.BlockSpec((4 * tn, D), lambda i: (0, 0), pipeline_mode=pl.Buffered(1)),
            pl.BlockSpec((GATE_PAD, D), lambda i: (0, 0)),
            pl.BlockSpec((2, CONV_K, tn), lambda i: (0, 0, 0)),
        ],
        out_specs=[
            pl.BlockSpec((tm, P_COLS), lambda i: (i, 0)),
            pl.BlockSpec((tm, M_WIDTH), lambda i: (i, 0)),
            pl.BlockSpec((tm, GATE_PAD), lambda i: (i, 0)),
        ],
        out_shape=[
            jax.ShapeDtypeStruct((M, P_COLS), BF16),
            jax.ShapeDtypeStruct((M, M_WIDTH), BF16),
            jax.ShapeDtypeStruct((M, GATE_PAD), F32),
        ],
        scratch_shapes=[
            pltpu.VMEM((tm, D), BF16),
            pltpu.VMEM((tm, D), BF16),
            pltpu.VMEM((tm, GATE_PAD), F32),
            pltpu.VMEM((tm + IN_HIST, tn), F32),
            pltpu.VMEM((tm + IN_HIST, tn), F32),
            pltpu.VMEM((2, IN_HIST, tn), F32),
        ],
        compiler_params=pltpu.CompilerParams(
            dimension_semantics=("arbitrary",), vmem_limit_bytes=VMEM_LIMIT),
        name="inproj",
    )(x2d, x2d, mod3, mod3, mod3, mod3, norm_w, wa, wb, w_gate, conv_w)


def _mlstm_kernel(q_ref, k_ref, v_ref, og_ref, g_ref, nw_ref, gb_ref, o_ref, ct_scr, m_scr):
    L = M_CHUNK
    NB = q_ref.shape[0]
    c = pl.program_id(1)

    @pl.when(c == 0)
    def _():
        ct_scr[...] = jnp.zeros_like(ct_scr)
        m_scr[...] = jnp.zeros_like(m_scr)

    row = lax.broadcasted_iota(jnp.int32, (L, GATE_PAD), 0)

    def gate_math(bb):
        g = g_ref[bb] + gb_ref[...]
        lf = jnp.minimum(g, 0.0) - jnp.log1p(jnp.exp(-jnp.abs(g)))
        bc = lf
        sh = 1
        while sh < L:
            bc = bc + jnp.where(row >= sh, pltpu.roll(bc, sh, 0), 0.0)
            sh *= 2
        bsh = pltpu.roll(bc, GATE_PAD - M_HEADS, 1)
        a = g - bsh
        cm = a
        sh = 1
        while sh < L:
            cm = jnp.maximum(cm, jnp.where(row >= sh, pltpu.roll(cm, sh, 0), -jnp.inf))
            sh *= 2
        m_prev = m_scr[bb]
        mm = jnp.maximum(cm, m_prev)
        iw = jnp.exp(m_prev - mm)
        eneg = jnp.exp(-(bsh + mm))
        m_scr[bb] = bsh[L - 1:L, :] + mm[L - 1:L, :]
        return dict(a=a, a_t=a.T, mm=mm, iw=iw, eneg=eneg)

    gm = [gate_math(bb) for bb in range(NB)]

    tri = (lax.broadcasted_iota(jnp.int32, (L, L), 1) <= lax.broadcasted_iota(jnp.int32, (L, L), 0))
    ones_col = (lax.broadcasted_iota(jnp.int32, (L, M_AUG - M_HD), 1) == 0).astype(BF16)

    chains = [(bb, h) for bb in range(NB) for h in range(M_HEADS)]
    cols = [slice(h * M_HD, (h + 1) * M_HD) for h in range(M_HEADS)]
    qs = {ch: q_ref[ch[0], :, cols[ch[1]]] for ch in chains}
    ks = {ch: k_ref[ch[0], :, cols[ch[1]]] for ch in chains}
    v_aug = {ch: jnp.concatenate([v_ref[ch[0], :, cols[ch[1]]], ones_col], axis=1) for ch in chains}
    cts = {ch: ct_scr[ch[0], ch[1]] for ch in chains}

    s_raw = {ch: _nt_dot(qs[ch], ks[ch]) for ch in chains}
    r_state = {ch: jnp.dot(qs[ch], cts[ch].astype(BF16), preferred_element_type=F32) for ch in chains}
    for ch in chains:
        bb, h = ch
        m_last = gm[bb]["mm"][L - 1:L, h:h + 1]
        w_row = jnp.exp(gm[bb]["a_t"][h:h + 1, :] - m_last)
        decay = gm[bb]["iw"][L - 1:L, h:h + 1]
        ktw = (ks[ch].astype(F32).T * w_row).astype(BF16)
        ct_scr[bb, h] = decay * cts[ch] + jnp.dot(ktw, v_aug[ch], preferred_element_type=F32)

    s16 = {}
    for ch in chains:
        bb, h = ch
        dw = jnp.where(tri, jnp.exp(gm[bb]["a_t"][h:h + 1, :] - gm[bb]["mm"][:, h:h + 1]), 0.0)
        s16[ch] = (s_raw[ch] * dw).astype(BF16)
    rs = {ch: gm[ch[0]]["iw"][:, ch[1]:ch[1] + 1] * r_state[ch]
          + jnp.dot(s16[ch], v_aug[ch], preferred_element_type=F32) for ch in chains}

    for ch in chains:
        bb, h = ch
        num = rs[ch][:, 0:M_HD]
        d = jnp.maximum(jnp.abs(rs[ch][:, M_HD:M_HD + 1]), gm[bb]["eneg"][:, h:h + 1])
        hc = num - jnp.mean(num, axis=-1, keepdims=True)
        var = jnp.mean(hc * hc, axis=-1, keepdims=True)
        hn = hc * lax.rsqrt(var + EPS * d * d) * nw_ref[:, cols[h]]
        o_ref[bb, :, cols[h]] = (hn * og_ref[bb, :, cols[h]].astype(F32)).astype(BF16)


def _mlstm(p3, og3, gates3, m_norm_w, gbias):
    B, S, _ = p3.shape
    L = M_CHUNK
    W = M_WIDTH
    NB = M_NB

    def col(cb):
        return pl.BlockSpec((NB, L, W), lambda b, c, cb=cb: (b, c, cb))

    return pl.pallas_call(
        _mlstm_kernel,
        grid=(B // NB, S // L),
        in_specs=[
            col(0), col(1), col(2),
            pl.BlockSpec((NB, L, W), lambda b, c: (b, c, 0)),
            pl.BlockSpec((NB, L, GATE_PAD), lambda b, c: (b, c, 0)),
            pl.BlockSpec((1, W), lambda b, c: (0, 0)),
            pl.BlockSpec((1, GATE_PAD), lambda b, c: (0, 0)),
        ],
        out_specs=pl.BlockSpec((NB, L, W), lambda b, c: (b, c, 0)),
        out_shape=jax.ShapeDtypeStruct((B, S, W), BF16),
        scratch_shapes=[
            pltpu.VMEM((NB, M_HEADS, M_HD, M_AUG), F32),
            pltpu.VMEM((NB, 1, GATE_PAD), F32),
        ],
        compiler_params=pltpu.CompilerParams(
            dimension_semantics=("arbitrary", "arbitrary"), vmem_limit_bytes=VMEM_LIMIT),
        name="mlstm",
    )(p3, p3, p3, og3, gates3, m_norm_w, gbias)


def _bucket_tiles():
    T = ATT_BLK
    kk = np.arange(T)[:, None]
    qq = np.arange(T)[None, :]
    out = []
    for base in (0, T):
        n = base + qq - kk
        max_exact = N_BUCKETS // 2
        nf = np.maximum(n, 1).astype(np.float64)
        large = max_exact + (np.log(nf / max_exact) / math.log(MAX_DIST / max_exact)
                             * (N_BUCKETS - max_exact)).astype(np.int64)
        large = np.minimum(large, N_BUCKETS - 1)
        bk = np.where(n < max_exact, n, large)
        bk = np.where(n < 0, -1, bk)
        out.append(bk)
    return np.stack(out).astype(np.int32)


def _attn_kernel(relb_ref, bucket_ref, lam_ref, anw_ref, q_ref, k_ref, v_ref, z_ref, o_ref,
                 bias_scr, vt_scr, qq_scr, s0_scr, s1_scr, s2_scr, acc_scr):
    T = ATT_BLK
    HB = ATT_HB
    S = k_ref.shape[1]
    NQ = S // T
    hg = pl.program_id(0)
    b = pl.program_id(1)
    heads = [slice(hb * A_VHD, (hb + 1) * A_VHD) for hb in range(HB)]

    @pl.when(b == 0)
    def _():
        for hb in range(HB):
            h = hg * HB + hb
            far = relb_ref[N_BUCKETS - 1, h]
            for t in range(2):
                bk = bucket_ref[t]
                tile = jnp.zeros((T, T), F32)
                for bb in range(N_BUCKETS - 1):
                    tile = jnp.where(bk == bb, (relb_ref[bb, h] - far) * LOG2E, tile)
                tile = jnp.where(bk < 0, NEG, tile)
                bias_scr[hb, t] = jnp.concatenate([tile, tile], axis=1)

    lane = lax.broadcasted_iota(jnp.int32, (T, A_VHD), 1)
    ones_rows = (lax.broadcasted_iota(jnp.int32, (VT_ROWS - A_VHD, T), 0) == 0).astype(BF16)
    for hb in range(HB):
        for ci in range(NQ):
            rows = slice(ci * T, (ci + 1) * T)
            vt_scr[hb, 0:A_VHD, rows] = v_ref[0, rows, heads[hb]].astype(F32).T.astype(BF16)
            vt_scr[hb, A_VHD:VT_ROWS, rows] = ones_rows
            qs = q_ref[0, rows, heads[hb]]
            zero = jnp.zeros_like(qs)
            qq_scr[hb, ci, 0:T, :] = jnp.where(lane < A_QKD, qs, zero)
            qq_scr[hb, ci, T:2 * T, :] = jnp.where(lane >= A_QKD, qs, zero)
    acc_scr[...] = jnp.zeros_like(acc_scr)

    s_bufs = (s0_scr, s1_scr, s2_scr)
    NBUF = len(s_bufs)

    def issue(ts, hb):
        qb, kj = pairs[ts[0]]
        n = len(ts)
        s = _nt_dot(k_ref[0, kj * T:(kj + n) * T, heads[hb]], qq_scr[hb, qb])
        for i, t in enumerate(ts):
            s_bufs[t % NBUF][hb] = s[i * T:(i + 1) * T]

    def run_stage(t, to_issue, kj, bias_idx, ms):
        start = kj * T
        out = []
        for hb in range(HB):
            s = s_bufs[t % NBUF][hb]
            if bias_idx is not None:
                s = s + bias_scr[hb, bias_idx]
            m_old = ms[hb]
            m_new = jnp.maximum(m_old, jnp.max(s, axis=0, keepdims=True))
            alpha = jnp.exp2(m_old - m_new)
            p = jnp.exp2(s - m_new).astype(BF16)
            if to_issue:
                issue(to_issue, hb)
            pv = jnp.dot(vt_scr[hb, :, start:start + T], p, preferred_element_type=F32)
            acc_scr[hb] = alpha * acc_scr[hb] + pv
            out.append(m_new)
        return tuple(out)

    lam_p = lam_ref[...]
    lam = (jnp.exp(jnp.sum(lam_p[0:1] * lam_p[1:2], axis=-1, keepdims=True))
           - jnp.exp(jnp.sum(lam_p[2:3] * lam_p[3:4], axis=-1, keepdims=True)) + LAM_INIT)

    def finalize(qi):
        rows = slice(qi * T, (qi + 1) * T)
        for hb in range(HB):
            acc = acc_scr[hb]
            o = acc[0:A_VHD] / acc[A_VHD:A_VHD + 1]
            d = (o[:, 0:T] - lam * o[:, T:2 * T]).T
            y = d * lax.rsqrt(jnp.mean(d * d, axis=-1, keepdims=True) + EPS) * anw_ref[...] * (1.0 - LAM_INIT)
            o_ref[0, rows, heads[hb]] = (y * z_ref[0, rows, heads[hb]].astype(F32)).astype(BF16)

    m_init = tuple(jnp.full((1, 2 * T), NEG, F32) for _ in range(HB))

    pairs = [(qi, kj) for qi in range(NQ) for kj in range(qi + 1)]
    for hb in range(HB):
        issue((0,), hb)
    issued = 1
    ms = m_init
    for t, (qi, kj) in enumerate(pairs):
        to_issue = ()
        if issued == t + 1 and issued < len(pairs):
            to_issue = (issued,)
            if issued + 1 < len(pairs) and pairs[issued + 1][0] == pairs[issued][0]:
                to_issue = (issued, issued + 1)
            issued += len(to_issue)
        bias_idx = 0 if kj == qi else (1 if kj == qi - 1 else None)
        ms = run_stage(t, to_issue, kj, bias_idx, ms)
        if kj == qi:
            finalize(qi)
            ms = m_init


def _attn(proj3, rel_bias, lam_p, a_norm_w):
    B, S, _ = proj3.shape
    T = ATT_BLK
    HB = ATT_HB
    W = HB * A_VHD
    q0 = 3 * IN_TN // W
    k0 = q0 + A_HEADS // HB
    v0 = k0 + A_HEADS // HB
    z0 = v0 + A_HEADS // HB
    buckets = jnp.asarray(_bucket_tiles())
    return pl.pallas_call(
        _attn_kernel,
        grid=(A_HEADS // HB, B),
        in_specs=[
            pl.BlockSpec(memory_space=pltpu.SMEM),
            pl.BlockSpec((2, T, T), lambda h, b: (0, 0, 0)),
            pl.BlockSpec((4, A_QKD), lambda h, b: (0, 0)),
            pl.BlockSpec((1, A_VHD), lambda h, b: (0, 0)),
            pl.BlockSpec((1, S, W), lambda h, b: (b, 0, q0 + h)),
            pl.BlockSpec((1, S, W), lambda h, b: (b, 0, k0 + h)),
            pl.BlockSpec((1, S, W), lambda h, b: (b, 0, v0 + h)),
            pl.BlockSpec((1, S, W), lambda h, b: (b, 0, z0 + h)),
        ],
        out_specs=pl.BlockSpec((1, S, W), lambda h, b: (b, 0, h)),
        out_shape=jax.ShapeDtypeStruct((B, S, A_WIDTH), BF16),
        scratch_shapes=[
            pltpu.VMEM((HB, 2, T, 2 * T), F32),
            pltpu.VMEM((HB, VT_ROWS, S), BF16),
            pltpu.VMEM((HB, S // T, 2 * T, A_VHD), BF16),
            pltpu.VMEM((HB, T, 2 * T), F32),
            pltpu.VMEM((HB, T, 2 * T), F32),
            pltpu.VMEM((HB, T, 2 * T), F32),
            pltpu.VMEM((HB, VT_ROWS, 2 * T), F32),
        ],
        compiler_params=pltpu.CompilerParams(
            dimension_semantics=("arbitrary", "arbitrary"), vmem_limit_bytes=VMEM_LIMIT),
        name="attn",
    )(rel_bias, buckets, lam_p, a_norm_w, proj3, proj3, proj3, proj3)


def _outproj_kernel(hm_ref, ha_ref, x_ref, gate_ref, w_ref, fw_ref, o_ref, raw0, raw1):
    sub = raw0.shape[0]
    nsub = x_ref.shape[0] // sub
    raws = (raw0, raw1)

    def matmul(r, buf):
        rows = slice(r * sub, (r + 1) * sub)
        y = jnp.dot(hm_ref[rows, :], w_ref[0:M_WIDTH, :], preferred_element_type=F32)
        raws[buf][...] = y + jnp.dot(ha_ref[rows, :], w_ref[M_WIDTH:, :], preferred_element_type=F32)

    def tail(r, buf):
        rows = slice(r * sub, (r + 1) * sub)
        res = x_ref[rows, :] + gate_ref[0] * raws[buf][...]
        o_ref[rows, :] = res * lax.rsqrt(jnp.mean(res * res, axis=-1, keepdims=True) + EPS) * fw_ref[...]

    matmul(0, 0)
    for r in range(1, nsub):
        tail(r - 1, (r - 1) % 2)
        matmul(r, r % 2)
    tail(nsub - 1, (nsub - 1) % 2)


def _outproj(hm2d, ha2d, x2d, mod3, w_out, final_w, B, S, tm=1024, sub=512):
    M, D = x2d.shape
    assert S % tm == 0 and tm % sub == 0
    tiles_per_seq = S // tm
    return pl.pallas_call(
        _outproj_kernel,
        grid=(M // tm,),
        scratch_shapes=[pltpu.VMEM((sub, D), F32), pltpu.VMEM((sub, D), F32)],
        in_specs=[
            pl.BlockSpec((tm, M_WIDTH), lambda i: (i, 0)),
            pl.BlockSpec((tm, A_WIDTH), lambda i: (i, 0)),
            pl.BlockSpec((tm, D), lambda i: (i, 0)),
            pl.BlockSpec((1, 1, D), lambda i: (2 * B + i // tiles_per_seq, 0, 0)),
            pl.BlockSpec((M_WIDTH + A_WIDTH, D), lambda i: (0, 0)),
            pl.BlockSpec((1, D), lambda i: (0, 0)),
        ],
        out_specs=pl.BlockSpec((tm, D), lambda i: (i, 0)),
        out_shape=jax.ShapeDtypeStruct((M, D), F32),
        compiler_params=pltpu.CompilerParams(
            dimension_semantics=("arbitrary",), vmem_limit_bytes=VMEM_LIMIT),
        name="outproj",
    )(hm2d, ha2d, x2d, mod3, w_out, final_w)


def kernel(x, c, norm_w, w_ada, b_ada, w_in, b_i, b_f, conv_q_w, conv_k_w, m_norm_w,
           lambda_q1, lambda_k1, lambda_q2, lambda_k2, a_norm_w, rel_bias, w_out, final_norm_w):
    B, S, D = x.shape
    assert norm_w.shape[0] == 1, "single layer only"
    assert D == D_MODEL and w_in.shape[2] == GATE_COL0 + 2 * M_HEADS + 4 * A_WIDTH
    assert B % M_NB == 0 and S % M_CHUNK == 0 and S % ATT_BLK == 0
    x2d = x.reshape(B * S, D)

    mod = _adaln(c, w_ada[0], b_ada[0])
    mod3 = mod.reshape(3 * B, 1, D)

    w_t = jnp.swapaxes(w_in, 1, 2)[0]
    a0 = GATE_COL0 + 2 * M_HEADS
    wa = w_t[:GATE_COL0].astype(BF16)
    wb = w_t[a0:].astype(BF16)
    w_gate = jnp.pad(w_t[GATE_COL0:a0], ((0, GATE_PAD - 2 * M_HEADS), (0, 0))).astype(BF16)
    conv_w = jnp.stack([conv_q_w[0], conv_k_w[0]])
    p, og, gates = _inproj(x2d, mod3, norm_w, wa, wb, w_gate, conv_w, B, S)
    p3 = p.reshape(B, S, P_COLS)
    gates3 = gates.reshape(B, S, GATE_PAD)

    gbias = jnp.pad(jnp.concatenate([b_i[0], b_f[0]]), (0, GATE_PAD - 2 * M_HEADS)).reshape(1, GATE_PAD)
    hm = _mlstm(p3, og.reshape(B, S, M_WIDTH), gates3, m_norm_w, gbias)

    lam_p = jnp.stack([lambda_q1[0], lambda_k1[0], lambda_q2[0], lambda_k2[0]])
    ha = _attn(p3, rel_bias, lam_p, a_norm_w)

    out = _outproj(hm.reshape(B * S, M_WIDTH), ha.reshape(B * S, A_WIDTH), x2d, mod3,
                   w_out[0].astype(BF16), final_norm_w.reshape(1, D), B, S)
    return out.reshape(B, S, D)
```

```python
import functools
import math

import numpy as np
import jax
import jax.numpy as jnp
from jax import lax
from jax.experimental import pallas as pl
from jax.experimental.pallas import tpu as pltpu

F32 = jnp.float32
BF16 = jnp.bfloat16

D_MODEL = 1024
M_WIDTH = 1024
M_HEADS = 4
M_HD = 256
CONV_K = 4
A_WIDTH = 1024
A_HEADS = 8
A_VHD = 128
A_QKD = 64
N_BUCKETS = 32
MAX_DIST = 128
EPS = 1e-6
LAM_INIT = 0.8 - 0.6 * math.exp(-0.3 * 0)

V7X_LANES = 128
V7X_SUBLANES = 8
V7X_BF16_SUBLANES = 16
V7X_MXU_DIM = 256
V7X_VMEM_BYTES = 64 * 1024 * 1024
VMEM_LIMIT = V7X_VMEM_BYTES - 8 * 1024 * 1024

GATE_COL0 = 5 * M_WIDTH
GATE_PAD = V7X_LANES

M_CHUNK = V7X_MXU_DIM
M_AUG = M_HD + V7X_LANES
M_NB = 4
ATT_BLK = V7X_MXU_DIM
ATT_HB = 2
VT_ROWS = A_VHD + V7X_BF16_SUBLANES
LOG2E = math.log2(math.e)
NEG = -1e30


def _nt_dot(a, b):
    return lax.dot_general(a, b, (((1,), (1,)), ((), ())), preferred_element_type=F32)


def _sigmoid(x):
    return 1.0 / (1.0 + jnp.exp(-x))


def _adaln_kernel(c_ref, w_ref, b_ref, o_ref):
    c = c_ref[...]
    sc = (c * _sigmoid(c)).astype(BF16)
    o_ref[0] = jnp.dot(sc, w_ref[...].astype(BF16), preferred_element_type=F32) + b_ref[0]


def _adaln(c, w_ada, b_ada):
    B, D = c.shape
    return pl.pallas_call(
        _adaln_kernel,
        grid=(3,),
        in_specs=[
            pl.BlockSpec((B, D), lambda j: (0, 0)),
            pl.BlockSpec((D, D), lambda j: (0, j)),
            pl.BlockSpec((1, 1, D), lambda j: (j, 0, 0)),
        ],
        out_specs=pl.BlockSpec((1, B, D), lambda j: (j, 0, 0)),
        out_shape=jax.ShapeDtypeStruct((3, B, D), F32),
        compiler_params=pltpu.CompilerParams(
            dimension_semantics=("arbitrary",), vmem_limit_bytes=VMEM_LIMIT),
        name="adaln",
    )(c, w_ada, b_ada.reshape(3, 1, D))


IN_TN = 1024
IN_TILES = 9
P_TILES = 7
P_COLS = P_TILES * IN_TN
IN_ORDER = (0, 1, 3, 4, 8, 2, 5, 6, 7)
IN_P_SLOT = {0: 0, 1: 1, 2: 2, 5: 3, 6: 4, 7: 5, 8: 6}
IN_HIST = V7X_SUBLANES


def _inproj_kernel(x_ref, xn_ref, shift_ref, scale_ref, shiftn_ref, scalen_ref, nw_ref,
                   wa_ref, wb_ref, wg_ref, cw_ref,
                   p_ref, g_ref, gates_ref, h_scr, hn_scr, gn_scr, raw0, raw1, tail_scr, *, tiles_per_seq):
    i = pl.program_id(0)
    tm = x_ref.shape[0]
    tn = IN_TN
    H = IN_HIST
    raws = (raw0, raw1)
    nbuf = len(raws)
    col_scale = {0: M_HD ** -0.5, 5: A_QKD ** -0.5 * LOG2E}

    def normed(xr, shr, scr):
        x = xr[...]
        y = x * lax.rsqrt(jnp.mean(x * x, axis=-1, keepdims=True) + EPS)
        return (y * (nw_ref[...] * (1.0 + scr[0])) + shr[0]).astype(BF16)

    @pl.when(i == 0)
    def _():
        h = normed(x_ref, shift_ref, scale_ref)
        h_scr[...] = h
        gates_ref[...] = _nt_dot(h, wg_ref[...])
        tail_scr[...] = jnp.zeros_like(tail_scr)

    @pl.when(i > 0)
    def _():
        h_scr[...] = hn_scr[...]
        gates_ref[...] = gn_scr[...]

    first = (i % tiles_per_seq) == 0

    def matmul(j, buf):
        raw = raws[buf]

        def dot_cols(w_ref, c0, width):
            return _nt_dot(h_scr[...], w_ref[c0:c0 + width, :])

        if j in (3, 4):
            half = tn // 2
            raw[H:tm + H, 0:half] = dot_cols(wa_ref, 3 * tn + (j - 3) * half, half)
            raw[H:tm + H, half:tn] = dot_cols(wa_ref, 4 * tn + (j - 3) * half, half)
        elif j < 3:
            raw[H:tm + H, :] = dot_cols(wa_ref, j * tn, tn)
        else:
            raw[H:tm + H, :] = dot_cols(wb_ref, (j - 5) * tn, tn)

    def tail(j, buf):
        raw = raws[buf]
        if j in (3, 4):
            o, z = raw[H:tm + H, 0:tn // 2], raw[H:tm + H, tn // 2:tn]
            g_ref[:, (j - 3) * (tn // 2):(j - 2) * (tn // 2)] = (_sigmoid(o) * (z * _sigmoid(z))).astype(BF16)
            return
        if j in (0, 1):
            raw[0:H, :] = jnp.where(first, 0.0, tail_scr[j])
            y = raw[H:tm + H, :] * cw_ref[j, CONV_K - 1:CONV_K, :]
            for jj in range(CONV_K - 1):
                off = H - (CONV_K - 1) + jj
                y = y + raw[off:off + tm, :] * cw_ref[j, jj:jj + 1, :]
            tail_scr[j] = raw[tm:tm + H, :]
            y = y * _sigmoid(y)
        elif j == 8:
            a = raw[H:tm + H, :]
            y = a * _sigmoid(a)
        else:
            y = raw[H:tm + H, :]
        if j in col_scale:
            y = y * col_scale[j]
        slot = IN_P_SLOT[j]
        p_ref[:, slot * tn:(slot + 1) * tn] = y.astype(BF16)

    matmul(IN_ORDER[0], 0)
    for n in range(1, IN_TILES):
        tail(IN_ORDER[n - 1], (n - 1) % nbuf)
        matmul(IN_ORDER[n], n % nbuf)
    tail(IN_ORDER[-1], (IN_TILES - 1) % nbuf)
    hn = normed(xn_ref, shiftn_ref, scalen_ref)
    hn_scr[...] = hn
    gn_scr[...] = _nt_dot(hn, wg_ref[...])


def _inproj(x2d, mod3, norm_w, wa, wb, w_gate, conv_w, B, S, tm=512):
    M, D = x2d.shape
    assert S % tm == 0
    tn = IN_TN
    tiles_per_seq = S // tm
    last = M // tm - 1

    def nxt(i):
        return jnp.minimum(i + 1, last)

    return pl.pallas_call(
        functools.partial(_inproj_kernel, tiles_per_seq=tiles_per_seq),
        grid=(M // tm,),
        in_specs=[
            pl.BlockSpec((tm, D), lambda i: (i, 0)),
            pl.BlockSpec((tm, D), lambda i: (nxt(i), 0)),
            pl.BlockSpec((1, 1, D), lambda i: (i // tiles_per_seq, 0, 0)),
            pl.BlockSpec((1, 1, D), lambda i: (B + i // tiles_per_seq, 0, 0)),
            pl.BlockSpec((1, 1, D), lambda i: (nxt(i) // tiles_per_seq, 0, 0)),
            pl.BlockSpec((1, 1, D), lambda i: (B + nxt(i) // tiles_per_seq, 0, 0)),
            pl.BlockSpec((1, D), lambda i: (0, 0)),
            pl.BlockSpec((5 * tn, D), lambda i: (0, 0), pipeline_mode=pl.Buffered(1)),
            pl.BlockSpec((4 * tn, D), lambda i: (0, 0), pipeline_mode=pl.Buffered(1)),
            pl.BlockSpec((GATE_PAD, D), lambda i: (0, 0)),
            pl.BlockSpec((2, CONV_K, tn), lambda i: (0, 0, 0)),
        ],
        out_specs=[
            pl.BlockSpec((tm, P_COLS), lambda i: (i, 0)),
            pl.BlockSpec((tm, M_WIDTH), lambda i: (i, 0)),
            pl.BlockSpec((tm, GATE_PAD), lambda i: (i, 0)),
        ],
        out_shape=[
            jax.ShapeDtypeStruct((M, P_COLS), BF16),
            jax.ShapeDtypeStruct((M, M_WIDTH), BF16),
            jax.ShapeDtypeStruct((M, GATE_PAD), F32),
        ],
        scratch_shapes=[
            pltpu.VMEM((tm, D), BF16),
            pltpu.VMEM((tm, D), BF16),
            pltpu.VMEM((tm, GATE_PAD), F32),
            pltpu.VMEM((tm + IN_HIST, tn), F32),
            pltpu.VMEM((tm + IN_HIST, tn), F32),
            pltpu.VMEM((2, IN_HIST, tn), F32),
        ],
        compiler_params=pltpu.CompilerParams(
            dimension_semantics=("arbitrary",), vmem_limit_bytes=VMEM_LIMIT),
        name="inproj",
    )(x2d, x2d, mod3, mod3, mod3, mod3, norm_w, wa, wb, w_gate, conv_w)


def _mlstm_kernel(q_ref, k_ref, v_ref, og_ref, g_ref, nw_ref, gb_ref, o_ref, ct_scr, m_scr):
    L = M_CHUNK
    NB = q_ref.shape[0]
    c = pl.program_id(1)

    @pl.when(c == 0)
    def _():
        ct_scr[...] = jnp.zeros_like(ct_scr)
        m_scr[...] = jnp.zeros_like(m_scr)

    row = lax.broadcasted_iota(jnp.int32, (L, GATE_PAD), 0)

    def gate_math(bb):
        g = g_ref[bb] + gb_ref[...]
        lf = jnp.minimum(g, 0.0) - jnp.log1p(jnp.exp(-jnp.abs(g)))
        bc = lf
        sh = 1
        while sh < L:
            bc = bc + jnp.where(row >= sh, pltpu.roll(bc, sh, 0), 0.0)
            sh *= 2
        bsh = pltpu.roll(bc, GATE_PAD - M_HEADS, 1)
        a = g - bsh
        cm = a
        sh = 1
        while sh < L:
            cm = jnp.maximum(cm, jnp.where(row >= sh, pltpu.roll(cm, sh, 0), -jnp.inf))
            sh *= 2
        m_prev = m_scr[bb]
        mm = jnp.maximum(cm, m_prev)
        iw = jnp.exp(m_prev - mm)
        eneg = jnp.exp(-(bsh + mm))
        m_scr[bb] = bsh[L - 1:L, :] + mm[L - 1:L, :]
        return dict(a=a, a_t=a.T, mm=mm, iw=iw, eneg=eneg)

    gm = [gate_math(bb) for bb in range(NB)]

    tri = (lax.broadcasted_iota(jnp.int32, (L, L), 1) <= lax.broadcasted_iota(jnp.int32, (L, L), 0))
    ones_col = (lax.broadcasted_iota(jnp.int32, (L, M_AUG - M_HD), 1) == 0).astype(BF16)

    chains = [(bb, h) for bb in range(NB) for h in range(M_HEADS)]
    cols = [slice(h * M_HD, (h + 1) * M_HD) for h in range(M_HEADS)]
    qs = {ch: q_ref[ch[0], :, cols[ch[1]]] for ch in chains}
    ks = {ch: k_ref[ch[0], :, cols[ch[1]]] for ch in chains}
    v_aug = {ch: jnp.concatenate([v_ref[ch[0], :, cols[ch[1]]], ones_col], axis=1) for ch in chains}
    cts = {ch: ct_scr[ch[0], ch[1]] for ch in chains}

    s_raw = {ch: _nt_dot(qs[ch], ks[ch]) for ch in chains}
    r_state = {ch: jnp.dot(qs[ch], cts[ch].astype(BF16), preferred_element_type=F32) for ch in chains}
    for ch in chains:
        bb, h = ch
        m_last = gm[bb]["mm"][L - 1:L, h:h + 1]
        w_row = jnp.exp(gm[bb]["a_t"][h:h + 1, :] - m_last)
        decay = gm[bb]["iw"][L - 1:L, h:h + 1]
        ktw = (ks[ch].astype(F32).T * w_row).astype(BF16)
        ct_scr[bb, h] = decay * cts[ch] + jnp.dot(ktw, v_aug[ch], preferred_element_type=F32)

    s16 = {}
    for ch in chains:
        bb, h = ch
        dw = jnp.where(tri, jnp.exp(gm[bb]["a_t"][h:h + 1, :] - gm[bb]["mm"][:, h:h + 1]), 0.0)
        s16[ch] = (s_raw[ch] * dw).astype(BF16)
    rs = {ch: gm[ch[0]]["iw"][:, ch[1]:ch[1] + 1] * r_state[ch]
          + jnp.dot(s16[ch], v_aug[ch], preferred_element_type=F32) for ch in chains}

    for ch in chains:
        bb, h = ch
        num = rs[ch][:, 0:M_HD]
        d = jnp.maximum(jnp.abs(rs[ch][:, M_HD:M_HD + 1]), gm[bb]["eneg"][:, h:h + 1])
        hc = num - jnp.mean(num, axis=-1, keepdims=True)
        var = jnp.mean(hc * hc, axis=-1, keepdims=True)
        hn = hc * lax.rsqrt(var + EPS * d * d) * nw_ref[:, cols[h]]
        o_ref[bb, :, cols[h]] = (hn * og_ref[bb, :, cols[h]].astype(F32)).astype(BF16)


def _mlstm(p3, og3, gates3, m_norm_w, gbias):
    B, S, _ = p3.shape
    L = M_CHUNK
    W = M_WIDTH
    NB = M_NB

    def col(cb):
        return pl.BlockSpec((NB, L, W), lambda b, c, cb=cb: (b, c, cb))

    return pl.pallas_call(
        _mlstm_kernel,
        grid=(B // NB, S // L),
        in_specs=[
            col(0), col(1), col(2),
            pl.BlockSpec((NB, L, W), lambda b, c: (b, c, 0)),
            pl.BlockSpec((NB, L, GATE_PAD), lambda b, c: (b, c, 0)),
            pl.BlockSpec((1, W), lambda b, c: (0, 0)),
            pl.BlockSpec((1, GATE_PAD), lambda b, c: (0, 0)),
        ],
        out_specs=pl.BlockSpec((NB, L, W), lambda b, c: (b, c, 0)),
        out_shape=jax.ShapeDtypeStruct((B, S, W), BF16),
        scratch_shapes=[
            pltpu.VMEM((NB, M_HEADS, M_HD, M_AUG), F32),
            pltpu.VMEM((NB, 1, GATE_PAD), F32),
        ],
        compiler_params=pltpu.CompilerParams(
            dimension_semantics=("arbitrary", "arbitrary"), vmem_limit_bytes=VMEM_LIMIT),
        name="mlstm",
    )(p3, p3, p3, og3, gates3, m_norm_w, gbias)


def _bucket_tiles():
    T = ATT_BLK
    kk = np.arange(T)[:, None]
    qq = np.arange(T)[None, :]
    out = []
    for base in (0, T):
        n = base + qq - kk
        max_exact = N_BUCKETS // 2
        nf = np.maximum(n, 1).astype(np.float64)
        large = max_exact + (np.log(nf / max_exact) / math.log(MAX_DIST / max_exact)
                             * (N_BUCKETS - max_exact)).astype(np.int64)
        large = np.minimum(large, N_BUCKETS - 1)
        bk = np.where(n < max_exact, n, large)
        bk = np.where(n < 0, -1, bk)
        out.append(bk)
    return np.stack(out).astype(np.int32)


def _attn_kernel(relb_ref, bucket_ref, lam_ref, anw_ref, q_ref, k_ref, v_ref, z_ref, o_ref,
                 bias_scr, vt_scr, qq_scr, s0_scr, s1_scr, acc_scr):
    T = ATT_BLK
    HB = ATT_HB
    S = k_ref.shape[1]
    NQ = S // T
    hg = pl.program_id(0)
    b = pl.program_id(1)
    heads = [slice(hb * A_VHD, (hb + 1) * A_VHD) for hb in range(HB)]

    @pl.when(b == 0)
    def _():
        for hb in range(HB):
            h = hg * HB + hb
            far = relb_ref[N_BUCKETS - 1, h]
            for t in range(2):
                bk = bucket_ref[t]
                tile = jnp.zeros((T, T), F32)
                for bb in range(N_BUCKETS - 1):
                    tile = jnp.where(bk == bb, (relb_ref[bb, h] - far) * LOG2E, tile)
                tile = jnp.where(bk < 0, NEG, tile)
                bias_scr[hb, t, :, 0:2 * T] = jnp.concatenate([tile, tile], axis=1)

    lane = lax.broadcasted_iota(jnp.int32, (T, A_VHD), 1)
    ones_rows = (lax.broadcasted_iota(jnp.int32, (VT_ROWS - A_VHD, T), 0) == 0).astype(BF16)
    for hb in range(HB):
        for ci in range(NQ):
            rows = slice(ci * T, (ci + 1) * T)
            vt_scr[hb, 0:A_VHD, rows] = v_ref[0, rows, heads[hb]].astype(F32).T.astype(BF16)
            vt_scr[hb, A_VHD:VT_ROWS, rows] = ones_rows
            qs = q_ref[0, rows, heads[hb]]
            zero = jnp.zeros_like(qs)
            qq_scr[hb, ci, 0:T, :] = jnp.where(lane < A_QKD, qs, zero)
            qq_scr[hb, ci, T:2 * T, :] = jnp.where(lane >= A_QKD, qs, zero)
    acc_scr[...] = jnp.zeros_like(acc_scr)

    s_bufs = (s0_scr, s1_scr)

    def issue(buf, qb, kj, hb):
        s_bufs[buf][hb, :, 0:2 * T] = _nt_dot(k_ref[0, kj * T:(kj + 1) * T, heads[hb]], qq_scr[hb, qb])

    def run_stage(cur, nxt, kj, bias_idx, ms):
        start = kj * T
        out = []
        for hb in range(HB):
            s = s_bufs[cur][hb, :, 0:2 * T]
            if bias_idx is not None:
                s = s + bias_scr[hb, bias_idx, :, 0:2 * T]
            m_old = ms[hb]
            m_new = jnp.maximum(m_old, jnp.max(s, axis=0, keepdims=True))
            alpha = jnp.exp2(m_old - m_new)
            p = jnp.exp2(s - m_new).astype(BF16)
            if nxt is not None:
                issue(1 - cur, nxt[0], nxt[1], hb)
            pv = jnp.dot(vt_scr[hb, :, start:start + T], p, preferred_element_type=F32)
            acc_scr[hb, :, 0:2 * T] = alpha * acc_scr[hb, :, 0:2 * T] + pv
            out.append(m_new)
        return tuple(out)

    lam_p = lam_ref[...]
    lam = (jnp.exp(jnp.sum(lam_p[0:1] * lam_p[1:2], axis=-1, keepdims=True))
           - jnp.exp(jnp.sum(lam_p[2:3] * lam_p[3:4], axis=-1, keepdims=True)) + LAM_INIT)

    def finalize(qi):
        rows = slice(qi * T, (qi + 1) * T)
        for hb in range(HB):
            acc = acc_scr[hb, :, 0:2 * T]
            o = acc[0:A_VHD] / acc[A_VHD:A_VHD + 1]
            d = (o[:, 0:T] - lam * o[:, T:2 * T]).T
            y = d * lax.rsqrt(jnp.mean(d * d, axis=-1, keepdims=True) + EPS) * anw_ref[...] * (1.0 - LAM_INIT)
            o_ref[0, rows, heads[hb]] = (y * z_ref[0, rows, heads[hb]].astype(F32)).astype(BF16)

    m_init = tuple(jnp.full((1, 2 * T), NEG, F32) for _ in range(HB))

    pairs = [(qi, kj) for qi in range(NQ) for kj in range(qi + 1)]
    for hb in range(HB):
        issue(0, 0, 0, hb)
    ms = m_init
    for t, (qi, kj) in enumerate(pairs):
        nxt = pairs[t + 1] if t + 1 < len(pairs) else None
        bias_idx = 0 if kj == qi else (1 if kj == qi - 1 else None)
        ms = run_stage(t % 2, nxt, kj, bias_idx, ms)
        if kj == qi:
            finalize(qi)
            ms = m_init


def _attn(proj3, rel_bias, lam_p, a_norm_w):
    B, S, _ = proj3.shape
    T = ATT_BLK
    HB = ATT_HB
    W = HB * A_VHD
    q0 = 3 * IN_TN // W
    k0 = q0 + A_HEADS // HB
    v0 = k0 + A_HEADS // HB
    z0 = v0 + A_HEADS // HB
    buckets = jnp.asarray(_bucket_tiles())
    return pl.pallas_call(
        _attn_kernel,
        grid=(A_HEADS // HB, B),
        in_specs=[
            pl.BlockSpec(memory_space=pltpu.SMEM),
            pl.BlockSpec((2, T, T), lambda h, b: (0, 0, 0)),
            pl.BlockSpec((4, A_QKD), lambda h, b: (0, 0)),
            pl.BlockSpec((1, A_VHD), lambda h, b: (0, 0)),
            pl.BlockSpec((1, S, W), lambda h, b: (b, 0, q0 + h)),
            pl.BlockSpec((1, S, W), lambda h, b: (b, 0, k0 + h)),
            pl.BlockSpec((1, S, W), lambda h, b: (b, 0, v0 + h)),
            pl.BlockSpec((1, S, W), lambda h, b: (b, 0, z0 + h)),
        ],
        out_specs=pl.BlockSpec((1, S, W), lambda h, b: (b, 0, h)),
        out_shape=jax.ShapeDtypeStruct((B, S, A_WIDTH), BF16),
        scratch_shapes=[
            pltpu.VMEM((HB, 2, T, 2 * T + V7X_LANES), F32),
            pltpu.VMEM((HB, VT_ROWS, S + V7X_LANES), BF16),
            pltpu.VMEM((HB, S // T, 2 * T, A_VHD), BF16),
            pltpu.VMEM((HB, T, 2 * T + V7X_LANES), F32),
            pltpu.VMEM((HB, T, 2 * T + V7X_LANES), F32),
            pltpu.VMEM((HB, VT_ROWS, 2 * T + V7X_LANES), F32),
        ],
        compiler_params=pltpu.CompilerParams(
            dimension_semantics=("arbitrary", "arbitrary"), vmem_limit_bytes=VMEM_LIMIT),
        name="attn",
    )(rel_bias, buckets, lam_p, a_norm_w, proj3, proj3, proj3, proj3)


def _outproj_kernel(hm_ref, ha_ref, x_ref, gate_ref, w_ref, fw_ref, o_ref, raw0, raw1):
    sub = raw0.shape[0]
    nsub = x_ref.shape[0] // sub
    raws = (raw0, raw1)

    def matmul(r, buf):
        rows = slice(r * sub, (r + 1) * sub)
        y = jnp.dot(hm_ref[rows, :], w_ref[0:M_WIDTH, :], preferred_element_type=F32)
        raws[buf][...] = y + jnp.dot(ha_ref[rows, :], w_ref[M_WIDTH:, :], preferred_element_type=F32)

    def tail(r, buf):
        rows = slice(r * sub, (r + 1) * sub)
        res = x_ref[rows, :] + gate_ref[0] * raws[buf][...]
        o_ref[rows, :] = res * lax.rsqrt(jnp.mean(res * res, axis=-1, keepdims=True) + EPS) * fw_ref[...]

    matmul(0, 0)
    for r in range(1, nsub):
        tail(r - 1, (r - 1) % 2)
        matmul(r, r % 2)
    tail(nsub - 1, (nsub - 1) % 2)


def _outproj(hm2d, ha2d, x2d, mod3, w_out, final_w, B, S, tm=1024, sub=512):
    M, D = x2d.shape
    assert S % tm == 0 and tm % sub == 0
    tiles_per_seq = S // tm
    return pl.pallas_call(
        _outproj_kernel,
        grid=(M // tm,),
        scratch_shapes=[pltpu.VMEM((sub, D), F32), pltpu.VMEM((sub, D), F32)],
        in_specs=[
            pl.BlockSpec((tm, M_WIDTH), lambda i: (i, 0)),
            pl.BlockSpec((tm, A_WIDTH), lambda i: (i, 0)),
            pl.BlockSpec((tm, D), lambda i: (i, 0)),
            pl.BlockSpec((1, 1, D), lambda i: (2 * B + i // tiles_per_seq, 0, 0)),
            pl.BlockSpec((M_WIDTH + A_WIDTH, D), lambda i: (0, 0)),
            pl.BlockSpec((1, D), lambda i: (0, 0)),
        ],
        out_specs=pl.BlockSpec((tm, D), lambda i: (i, 0)),
        out_shape=jax.ShapeDtypeStruct((M, D), F32),
        compiler_params=pltpu.CompilerParams(
            dimension_semantics=("arbitrary",), vmem_limit_bytes=VMEM_LIMIT),
        name="outproj",
    )(hm2d, ha2d, x2d, mod3, w_out, final_w)


def kernel(x, c, norm_w, w_ada, b_ada, w_in, b_i, b_f, conv_q_w, conv_k_w, m_norm_w,
           lambda_q1, lambda_k1, lambda_q2, lambda_k2, a_norm_w, rel_bias, w_out, final_norm_w):
    B, S, D = x.shape
    assert norm_w.shape[0] == 1, "single layer only"
    assert D == D_MODEL and w_in.shape[2] == GATE_COL0 + 2 * M_HEADS + 4 * A_WIDTH
    assert B % M_NB == 0 and S % M_CHUNK == 0 and S % ATT_BLK == 0
    x2d = x.reshape(B * S, D)

    mod = _adaln(c, w_ada[0], b_ada[0])
    mod3 = mod.reshape(3 * B, 1, D)

    w_t = jnp.swapaxes(w_in, 1, 2)[0]
    a0 = GATE_COL0 + 2 * M_HEADS
    wa = w_t[:GATE_COL0].astype(BF16)
    wb = w_t[a0:].astype(BF16)
    w_gate = jnp.pad(w_t[GATE_COL0:a0], ((0, GATE_PAD - 2 * M_HEADS), (0, 0))).astype(BF16)
    conv_w = jnp.stack([conv_q_w[0], conv_k_w[0]])
    p, og, gates = _inproj(x2d, mod3, norm_w, wa, wb, w_gate, conv_w, B, S)
    p3 = p.reshape(B, S, P_COLS)
    gates3 = gates.reshape(B, S, GATE_PAD)

    gbias = jnp.pad(jnp.concatenate([b_i[0], b_f[0]]), (0, GATE_PAD - 2 * M_HEADS)).reshape(1, GATE_PAD)
    hm = _mlstm(p3, og.reshape(B, S, M_WIDTH), gates3, m_norm_w, gbias)

    lam_p = jnp.stack([lambda_q1[0], lambda_k1[0], lambda_q2[0], lambda_k2[0]])
    ha = _attn(p3, rel_bias, lam_p, a_norm_w)

    out = _outproj(hm.reshape(B * S, M_WIDTH), ha.reshape(B * S, A_WIDTH), x2d, mod3,
                   w_out[0].astype(BF16), final_norm_w.reshape(1, D), B, S)
    return out.reshape(B, S, D)
```

```python
import functools
import math

import numpy as np
import jax
import jax.numpy as jnp
from jax import lax
from jax.experimental import pallas as pl
from jax.experimental.pallas import tpu as pltpu

F32 = jnp.float32
BF16 = jnp.bfloat16

D_MODEL = 1024
M_WIDTH = 1024
M_HEADS = 4
M_HD = 256
CONV_K = 4
A_WIDTH = 1024
A_HEADS = 8
A_VHD = 128
A_QKD = 64
N_BUCKETS = 32
MAX_DIST = 128
EPS = 1e-6
LAM_INIT = 0.8 - 0.6 * math.exp(-0.3 * 0)

V7X_LANES = 128
V7X_SUBLANES = 8
V7X_BF16_SUBLANES = 16
V7X_MXU_DIM = 256
V7X_VMEM_BYTES = 64 * 1024 * 1024
VMEM_LIMIT = V7X_VMEM_BYTES - 8 * 1024 * 1024

GATE_COL0 = 5 * M_WIDTH
GATE_PAD = V7X_LANES

M_CHUNK = V7X_MXU_DIM
M_AUG = M_HD + V7X_LANES
M_NB = 4
ATT_BLK = V7X_MXU_DIM
ATT_HB = 2
VT_ROWS = A_VHD + V7X_BF16_SUBLANES
LOG2E = math.log2(math.e)
NEG = -1e30


def _nt_dot(a, b):
    return lax.dot_general(a, b, (((1,), (1,)), ((), ())), preferred_element_type=F32)


def _sigmoid(x):
    return 1.0 / (1.0 + jnp.exp(-x))


def _adaln_kernel(c_ref, w_ref, b_ref, o_ref):
    c = c_ref[...]
    sc = (c * _sigmoid(c)).astype(BF16)
    o_ref[0] = jnp.dot(sc, w_ref[...].astype(BF16), preferred_element_type=F32) + b_ref[0]


def _adaln(c, w_ada, b_ada):
    B, D = c.shape
    return pl.pallas_call(
        _adaln_kernel,
        grid=(3,),
        in_specs=[
            pl.BlockSpec((B, D), lambda j: (0, 0)),
            pl.BlockSpec((D, D), lambda j: (0, j)),
            pl.BlockSpec((1, 1, D), lambda j: (j, 0, 0)),
        ],
        out_specs=pl.BlockSpec((1, B, D), lambda j: (j, 0, 0)),
        out_shape=jax.ShapeDtypeStruct((3, B, D), F32),
        compiler_params=pltpu.CompilerParams(
            dimension_semantics=("arbitrary",), vmem_limit_bytes=VMEM_LIMIT),
        name="adaln",
    )(c, w_ada, b_ada.reshape(3, 1, D))


IN_TN = 1024
IN_TILES = 9
P_TILES = 7
P_COLS = P_TILES * IN_TN
IN_ORDER = (0, 1, 3, 4, 8, 2, 5, 6, 7)
IN_P_SLOT = {0: 0, 1: 1, 2: 2, 5: 3, 6: 4, 7: 5, 8: 6}
IN_HIST = V7X_SUBLANES


def _inproj_kernel(x_ref, xn_ref, shift_ref, scale_ref, shiftn_ref, scalen_ref, nw_ref,
                   wa_ref, wb_ref, wg_ref, cw_ref,
                   p_ref, g_ref, gates_ref, h_scr, hn_scr, gn_scr, raw0, raw1, tail_scr, *, tiles_per_seq):
    i = pl.program_id(0)
    tm = x_ref.shape[0]
    tn = IN_TN
    H = IN_HIST
    raws = (raw0, raw1)
    nbuf = len(raws)
    col_scale = {0: M_HD ** -0.5, 5: A_QKD ** -0.5 * LOG2E}

    def normed(xr, shr, scr):
        x = xr[...]
        y = x * lax.rsqrt(jnp.mean(x * x, axis=-1, keepdims=True) + EPS)
        return (y * (nw_ref[...] * (1.0 + scr[0])) + shr[0]).astype(BF16)

    @pl.when(i == 0)
    def _():
        h = normed(x_ref, shift_ref, scale_ref)
        h_scr[...] = h
        gates_ref[...] = _nt_dot(h, wg_ref[...])
        tail_scr[...] = jnp.zeros_like(tail_scr)

    @pl.when(i > 0)
    def _():
        h_scr[...] = hn_scr[...]
        gates_ref[...] = gn_scr[...]

    first = (i % tiles_per_seq) == 0

    def matmul(j, buf):
        raw = raws[buf]

        def dot_cols(w_ref, c0, width):
            return _nt_dot(h_scr[...], w_ref[c0:c0 + width, :])

        if j in (3, 4):
            half = tn // 2
            raw[H:tm + H, 0:half] = dot_cols(wa_ref, 3 * tn + (j - 3) * half, half)
            raw[H:tm + H, half:tn] = dot_cols(wa_ref, 4 * tn + (j - 3) * half, half)
        elif j < 3:
            raw[H:tm + H, :] = dot_cols(wa_ref, j * tn, tn)
        else:
            raw[H:tm + H, :] = dot_cols(wb_ref, (j - 5) * tn, tn)

    def tail(j, buf):
        raw = raws[buf]
        if j in (3, 4):
            o, z = raw[H:tm + H, 0:tn // 2], raw[H:tm + H, tn // 2:tn]
            g_ref[:, (j - 3) * (tn // 2):(j - 2) * (tn // 2)] = (_sigmoid(o) * (z * _sigmoid(z))).astype(BF16)
            return
        if j in (0, 1):
            raw[0:H, :] = jnp.where(first, 0.0, tail_scr[j])
            y = raw[H:tm + H, :] * cw_ref[j, CONV_K - 1:CONV_K, :]
            for jj in range(CONV_K - 1):
                off = H - (CONV_K - 1) + jj
                y = y + raw[off:off + tm, :] * cw_ref[j, jj:jj + 1, :]
            tail_scr[j] = raw[tm:tm + H, :]
            y = y * _sigmoid(y)
        elif j == 8:
            a = raw[H:tm + H, :]
            y = a * _sigmoid(a)
        else:
            y = raw[H:tm + H, :]
        if j in col_scale:
            y = y * col_scale[j]
        slot = IN_P_SLOT[j]
        p_ref[:, slot * tn:(slot + 1) * tn] = y.astype(BF16)

    matmul(IN_ORDER[0], 0)
    for n in range(1, IN_TILES):
        tail(IN_ORDER[n - 1], (n - 1) % nbuf)
        matmul(IN_ORDER[n], n % nbuf)
    tail(IN_ORDER[-1], (IN_TILES - 1) % nbuf)
    hn = normed(xn_ref, shiftn_ref, scalen_ref)
    hn_scr[...] = hn
    gn_scr[...] = _nt_dot(hn, wg_ref[...])


def _inproj(x2d, mod3, norm_w, wa, wb, w_gate, conv_w, B, S, tm=512):
    M, D = x2d.shape
    assert S % tm == 0
    tn = IN_TN
    tiles_per_seq = S // tm
    last = M // tm - 1

    def nxt(i):
        return jnp.minimum(i + 1, last)

    return pl.pallas_call(
        functools.partial(_inproj_kernel, tiles_per_seq=tiles_per_seq),
        grid=(M // tm,),
        in_specs=[
            pl.BlockSpec((tm, D), lambda i: (i, 0)),
            pl.BlockSpec((tm, D), lambda i: (nxt(i), 0)),
            pl.BlockSpec((1, 1, D), lambda i: (i // tiles_per_seq, 0, 0)),
            pl.BlockSpec((1, 1, D), lambda i: (B + i // tiles_per_seq, 0, 0)),
            pl.BlockSpec((1, 1, D), lambda i: (nxt(i) // tiles_per_seq, 0, 0)),
            pl.BlockSpec((1, 1, D), lambda i: (B + nxt(i) // tiles_per_seq, 0, 0)),
            pl.BlockSpec((1, D), lambda i: (0, 0)),
            pl.BlockSpec((5 * tn, D), lambda i: (0, 0), pipeline_mode=pl.Buffered(1)),
            pl.BlockSpec((4 * tn, D), lambda i: (0, 0), pipeline_mode=pl.Buffered(1)),
            pl.BlockSpec((GATE_PAD, D), lambda i: (0, 0)),
            pl.BlockSpec((2, CONV_K, tn), lambda i: (0, 0, 0)),
        ],
        out_specs=[
            pl.BlockSpec((tm, P_COLS), lambda i: (i, 0)),
            pl.BlockSpec((tm, M_WIDTH), lambda i: (i, 0)),
            pl.BlockSpec((tm, GATE_PAD), lambda i: (i, 0)),
        ],
        out_shape=[
            jax.ShapeDtypeStruct((M, P_COLS), BF16),
            jax.ShapeDtypeStruct((M, M_WIDTH), BF16),
            jax.ShapeDtypeStruct((M, GATE_PAD), F32),
        ],
        scratch_shapes=[
            pltpu.VMEM((tm, D), BF16),
            pltpu.VMEM((tm, D), BF16),
            pltpu.VMEM((tm, GATE_PAD), F32),
            pltpu.VMEM((tm + IN_HIST, tn), F32),
            pltpu.VMEM((tm + IN_HIST, tn), F32),
            pltpu.VMEM((2, IN_HIST, tn), F32),
        ],
        compiler_params=pltpu.CompilerParams(
            dimension_semantics=("arbitrary",), vmem_limit_bytes=VMEM_LIMIT),
        name="inproj",
    )(x2d, x2d, mod3, mod3, mod3, mod3, norm_w, wa, wb, w_gate, conv_w)


def _mlstm_kernel(q_ref, k_ref, v_ref, og_ref, g_ref, nw_ref, gb_ref, o_ref, ct_scr, m_scr):
    L = M_CHUNK
    NB = q_ref.shape[0]
    c = pl.program_id(1)

    @pl.when(c == 0)
    def _():
        ct_scr[...] = jnp.zeros_like(ct_scr)
        m_scr[...] = jnp.zeros_like(m_scr)

    row = lax.broadcasted_iota(jnp.int32, (L, GATE_PAD), 0)

    def gate_math(bb):
        g = g_ref[bb] + gb_ref[...]
        lf = jnp.minimum(g, 0.0) - jnp.log1p(jnp.exp(-jnp.abs(g)))
        bc = lf
        sh = 1
        while sh < L:
            bc = bc + jnp.where(row >= sh, pltpu.roll(bc, sh, 0), 0.0)
            sh *= 2
        bsh = pltpu.roll(bc, GATE_PAD - M_HEADS, 1)
        a = g - bsh
        cm = a
        sh = 1
        while sh < L:
            cm = jnp.maximum(cm, jnp.where(row >= sh, pltpu.roll(cm, sh, 0), -jnp.inf))
            sh *= 2
        m_prev = m_scr[bb]
        mm = jnp.maximum(cm, m_prev)
        iw = jnp.exp(m_prev - mm)
        eneg = jnp.exp(-(bsh + mm))
        m_scr[bb] = bsh[L - 1:L, :] + mm[L - 1:L, :]
        return dict(a=a, a_t=a.T, mm=mm, iw=iw, eneg=eneg)

    gm = [gate_math(bb) for bb in range(NB)]

    tri = (lax.broadcasted_iota(jnp.int32, (L, L), 1) <= lax.broadcasted_iota(jnp.int32, (L, L), 0))
    ones_col = (lax.broadcasted_iota(jnp.int32, (L, M_AUG - M_HD), 1) == 0).astype(BF16)

    chains = [(bb, h) for bb in range(NB) for h in range(M_HEADS)]
    cols = [slice(h * M_HD, (h + 1) * M_HD) for h in range(M_HEADS)]
    qs = {ch: q_ref[ch[0], :, cols[ch[1]]] for ch in chains}
    ks = {ch: k_ref[ch[0], :, cols[ch[1]]] for ch in chains}
    v_aug = {ch: jnp.concatenate([v_ref[ch[0], :, cols[ch[1]]], ones_col], axis=1) for ch in chains}
    cts = {ch: ct_scr[ch[0], ch[1]] for ch in chains}

    s_raw = {ch: _nt_dot(qs[ch], ks[ch]) for ch in chains}
    r_state = {ch: jnp.dot(qs[ch], cts[ch].astype(BF16), preferred_element_type=F32) for ch in chains}
    for ch in chains:
        bb, h = ch
        m_last = gm[bb]["mm"][L - 1:L, h:h + 1]
        w_row = jnp.exp(gm[bb]["a_t"][h:h + 1, :] - m_last)
        decay = gm[bb]["iw"][L - 1:L, h:h + 1]
        ktw = (ks[ch].astype(F32).T * w_row).astype(BF16)
        ct_scr[bb, h] = decay * cts[ch] + jnp.dot(ktw, v_aug[ch], preferred_element_type=F32)

    s16 = {}
    for ch in chains:
        bb, h = ch
        dw = jnp.where(tri, jnp.exp(gm[bb]["a_t"][h:h + 1, :] - gm[bb]["mm"][:, h:h + 1]), 0.0)
        s16[ch] = (s_raw[ch] * dw).astype(BF16)
    rs = {ch: gm[ch[0]]["iw"][:, ch[1]:ch[1] + 1] * r_state[ch]
          + jnp.dot(s16[ch], v_aug[ch], preferred_element_type=F32) for ch in chains}

    for ch in chains:
        bb, h = ch
        num = rs[ch][:, 0:M_HD]
        d = jnp.maximum(jnp.abs(rs[ch][:, M_HD:M_HD + 1]), gm[bb]["eneg"][:, h:h + 1])
        hc = num - jnp.mean(num, axis=-1, keepdims=True)
        var = jnp.mean(hc * hc, axis=-1, keepdims=True)
        hn = hc * lax.rsqrt(var + EPS * d * d) * nw_ref[:, cols[h]]
        o_ref[bb, :, cols[h]] = (hn * og_ref[bb, :, cols[h]].astype(F32)).astype(BF16)


def _mlstm(p3, og3, gates3, m_norm_w, gbias):
    B, S, _ = p3.shape
    L = M_CHUNK
    W = M_WIDTH
    NB = M_NB

    def col(cb):
        return pl.BlockSpec((NB, L, W), lambda b, c, cb=cb: (b, c, cb))

    return pl.pallas_call(
        _mlstm_kernel,
        grid=(B // NB, S // L),
        in_specs=[
            col(0), col(1), col(2),
            pl.BlockSpec((NB, L, W), lambda b, c: (b, c, 0)),
            pl.BlockSpec((NB, L, GATE_PAD), lambda b, c: (b, c, 0)),
            pl.BlockSpec((1, W), lambda b, c: (0, 0)),
            pl.BlockSpec((1, GATE_PAD), lambda b, c: (0, 0)),
        ],
        out_specs=pl.BlockSpec((NB, L, W), lambda b, c: (b, c, 0)),
        out_shape=jax.ShapeDtypeStruct((B, S, W), BF16),
        scratch_shapes=[
            pltpu.VMEM((NB, M_HEADS, M_HD, M_AUG), F32),
            pltpu.VMEM((NB, 1, GATE_PAD), F32),
        ],
        compiler_params=pltpu.CompilerParams(
            dimension_semantics=("arbitrary", "arbitrary"), vmem_limit_bytes=VMEM_LIMIT),
        name="mlstm",
    )(p3, p3, p3, og3, gates3, m_norm_w, gbias)


def _bucket_tiles():
    T = ATT_BLK
    kk = np.arange(T)[:, None]
    qq = np.arange(T)[None, :]
    out = []
    for base in (0, T):
        n = base + qq - kk
        max_exact = N_BUCKETS // 2
        nf = np.maximum(n, 1).astype(np.float64)
        large = max_exact + (np.log(nf / max_exact) / math.log(MAX_DIST / max_exact)
                             * (N_BUCKETS - max_exact)).astype(np.int64)
        large = np.minimum(large, N_BUCKETS - 1)
        bk = np.where(n < max_exact, n, large)
        bk = np.where(n < 0, -1, bk)
        out.append(bk)
    return np.stack(out).astype(np.int32)


def _attn_kernel(relb_ref, bucket_ref, lam_ref, anw_ref, q_ref, k_ref, v_ref, z_ref, o_ref,
                 bias_scr, vt_scr, qq_scr, s0_scr, s1_scr, acc_scr):
    T = ATT_BLK
    HB = ATT_HB
    S = k_ref.shape[1]
    NQ = S // T
    hg = pl.program_id(0)
    b = pl.program_id(1)
    heads = [slice(hb * A_VHD, (hb + 1) * A_VHD) for hb in range(HB)]

    @pl.when(b == 0)
    def _():
        for hb in range(HB):
            h = hg * HB + hb
            far = relb_ref[N_BUCKETS - 1, h]
            for t in range(2):
                bk = bucket_ref[t]
                tile = jnp.zeros((T, T), F32)
                for bb in range(N_BUCKETS - 1):
                    tile = jnp.where(bk == bb, (relb_ref[bb, h] - far) * LOG2E, tile)
                tile = jnp.where(bk < 0, NEG, tile)
                bias_scr[hb, t, :, 0:2 * T] = jnp.concatenate([tile, tile], axis=1)

    lane = lax.broadcasted_iota(jnp.int32, (T, A_VHD), 1)
    ones_rows = (lax.broadcasted_iota(jnp.int32, (VT_ROWS - A_VHD, T), 0) == 0).astype(BF16)
    for hb in range(HB):
        for ci in range(NQ):
            rows = slice(ci * T, (ci + 1) * T)
            vt_scr[hb, 0:A_VHD, rows] = v_ref[0, rows, heads[hb]].astype(F32).T.astype(BF16)
            vt_scr[hb, A_VHD:VT_ROWS, rows] = ones_rows
            qs = q_ref[0, rows, heads[hb]]
            zero = jnp.zeros_like(qs)
            qq_scr[hb, ci, 0:T, :] = jnp.where(lane < A_QKD, qs, zero)
            qq_scr[hb, ci, T:2 * T, :] = jnp.where(lane >= A_QKD, qs, zero)
    acc_scr[...] = jnp.zeros_like(acc_scr)

    s_bufs = (s0_scr, s1_scr)

    def issue(buf, qb, kj, hb):
        s_bufs[buf][hb, :, 0:2 * T] = _nt_dot(k_ref[0, kj * T:(kj + 1) * T, heads[hb]], qq_scr[hb, qb])

    def run_stage(cur, nxt, kj, bias_idx, ms):
        start = kj * T
        out = []
        for hb in range(HB):
            s = s_bufs[cur][hb, :, 0:2 * T]
            if bias_idx is not None:
                s = s + bias_scr[hb, bias_idx, :, 0:2 * T]
            m_old = ms[hb]
            m_new = jnp.maximum(m_old, jnp.max(s, axis=0, keepdims=True))
            alpha = jnp.exp2(m_old - m_new)
            p = jnp.exp2(s - m_new).astype(BF16)
            if nxt is not None:
                issue(1 - cur, nxt[0], nxt[1], hb)
            pv = jnp.dot(vt_scr[hb, :, start:start + T], p, preferred_element_type=F32)
            acc_scr[hb, :, 0:2 * T] = alpha * acc_scr[hb, :, 0:2 * T] + pv
            out.append(m_new)
        return tuple(out)

    lam_p = lam_ref[...]
    lam = (jnp.exp(jnp.sum(lam_p[0:1] * lam_p[1:2], axis=-1, keepdims=True))
           - jnp.exp(jnp.sum(lam_p[2:3] * lam_p[3:4], axis=-1, keepdims=True)) + LAM_INIT)

    def finalize(qi):
        rows = slice(qi * T, (qi + 1) * T)
        for hb in range(HB):
            acc = acc_scr[hb, :, 0:2 * T]
            o = acc[0:A_VHD] / acc[A_VHD:A_VHD + 1]
            d = (o[:, 0:T] - lam * o[:, T:2 * T]).T
            y = d * lax.rsqrt(jnp.mean(d * d, axis=-1, keepdims=True) + EPS) * anw_ref[...] * (1.0 - LAM_INIT)
            o_ref[0, rows, heads[hb]] = (y * z_ref[0, rows, heads[hb]].astype(F32)).astype(BF16)

    m_init = tuple(jnp.full((1, 2 * T), NEG, F32) for _ in range(HB))

    pairs = [(qi, kj) for qi in range(NQ) for kj in range(qi + 1)]
    for hb in range(HB):
        issue(0, 0, 0, hb)
    ms = m_init
    for t, (qi, kj) in enumerate(pairs):
        nxt = pairs[t + 1] if t + 1 < len(pairs) else None
        bias_idx = 0 if kj == qi else (1 if kj == qi - 1 else None)
        ms = run_stage(t % 2, nxt, kj, bias_idx, ms)
        if kj == qi:
            finalize(qi)
            ms = m_init


def _attn(proj3, rel_bias, lam_p, a_norm_w):
    B, S, _ = proj3.shape
    T = ATT_BLK
    HB = ATT_HB
    W = HB * A_VHD
    q0 = 3 * IN_TN // W
    k0 = q0 + A_HEADS // HB
    v0 = k0 + A_HEADS // HB
    z0 = v0 + A_HEADS // HB
    buckets = jnp.asarray(_bucket_tiles())
    return pl.pallas_call(
        _attn_kernel,
        grid=(A_HEADS // HB, B),
        in_specs=[
            pl.BlockSpec(memory_space=pltpu.SMEM),
            pl.BlockSpec((2, T, T), lambda h, b: (0, 0, 0)),
            pl.BlockSpec((4, A_QKD), lambda h, b: (0, 0)),
            pl.BlockSpec((1, A_VHD), lambda h, b: (0, 0)),
            pl.BlockSpec((1, S, W), lambda h, b: (b, 0, q0 + h)),
            pl.BlockSpec((1, S, W), lambda h, b: (b, 0, k0 + h)),
            pl.BlockSpec((1, S, W), lambda h, b: (b, 0, v0 + h)),
            pl.BlockSpec((1, S, W), lambda h, b: (b, 0, z0 + h)),
        ],
        out_specs=pl.BlockSpec((1, S, W), lambda h, b: (b, 0, h)),
        out_shape=jax.ShapeDtypeStruct((B, S, A_WIDTH), BF16),
        scratch_shapes=[
            pltpu.VMEM((HB, 2, T, 2 * T), F32),
            pltpu.VMEM((HB, VT_ROWS, S + V7X_LANES), BF16),
            pltpu.VMEM((HB, S // T, 2 * T, A_VHD), BF16),
            pltpu.VMEM((HB, T, 2 * T), F32),
            pltpu.VMEM((HB, T, 2 * T), F32),
            pltpu.VMEM((HB, VT_ROWS, 2 * T), F32),
        ],
        compiler_params=pltpu.CompilerParams(
            dimension_semantics=("arbitrary", "arbitrary"), vmem_limit_bytes=VMEM_LIMIT),
        name="attn",
    )(rel_bias, buckets, lam_p, a_norm_w, proj3, proj3, proj3, proj3)


def _outproj_kernel(hm_ref, ha_ref, x_ref, gate_ref, w_ref, fw_ref, o_ref, raw0, raw1):
    sub = raw0.shape[0]
    nsub = x_ref.shape[0] // sub
    raws = (raw0, raw1)

    def matmul(r, buf):
        rows = slice(r * sub, (r + 1) * sub)
        y = jnp.dot(hm_ref[rows, :], w_ref[0:M_WIDTH, :], preferred_element_type=F32)
        raws[buf][...] = y + jnp.dot(ha_ref[rows, :], w_ref[M_WIDTH:, :], preferred_element_type=F32)

    def tail(r, buf):
        rows = slice(r * sub, (r + 1) * sub)
        res = x_ref[rows, :] + gate_ref[0] * raws[buf][...]
        o_ref[rows, :] = res * lax.rsqrt(jnp.mean(res * res, axis=-1, keepdims=True) + EPS) * fw_ref[...]

    matmul(0, 0)
    for r in range(1, nsub):
        tail(r - 1, (r - 1) % 2)
        matmul(r, r % 2)
    tail(nsub - 1, (nsub - 1) % 2)


def _outproj(hm2d, ha2d, x2d, mod3, w_out, final_w, B, S, tm=1024, sub=512):
    M, D = x2d.shape
    assert S % tm == 0 and tm % sub == 0
    tiles_per_seq = S // tm
    return pl.pallas_call(
        _outproj_kernel,
        grid=(M // tm,),
        scratch_shapes=[pltpu.VMEM((sub, D), F32), pltpu.VMEM((sub, D), F32)],
        in_specs=[
            pl.BlockSpec((tm, M_WIDTH), lambda i: (i, 0)),
            pl.BlockSpec((tm, A_WIDTH), lambda i: (i, 0)),
            pl.BlockSpec((tm, D), lambda i: (i, 0)),
            pl.BlockSpec((1, 1, D), lambda i: (2 * B + i // tiles_per_seq, 0, 0)),
            pl.BlockSpec((M_WIDTH + A_WIDTH, D), lambda i: (0, 0)),
            pl.BlockSpec((1, D), lambda i: (0, 0)),
        ],
        out_specs=pl.BlockSpec((tm, D), lambda i: (i, 0)),
        out_shape=jax.ShapeDtypeStruct((M, D), F32),
        compiler_params=pltpu.CompilerParams(
            dimension_semantics=("arbitrary",), vmem_limit_bytes=VMEM_LIMIT),
        name="outproj",
    )(hm2d, ha2d, x2d, mod3, w_out, final_w)


def kernel(x, c, norm_w, w_ada, b_ada, w_in, b_i, b_f, conv_q_w, conv_k_w, m_norm_w,
           lambda_q1, lambda_k1, lambda_q2, lambda_k2, a_norm_w, rel_bias, w_out, final_norm_w):
    B, S, D = x.shape
    assert norm_w.shape[0] == 1, "single layer only"
    assert D == D_MODEL and w_in.shape[2] == GATE_COL0 + 2 * M_HEADS + 4 * A_WIDTH
    assert B % M_NB == 0 and S % M_CHUNK == 0 and S % ATT_BLK == 0
    x2d = x.reshape(B * S, D)

    mod = _adaln(c, w_ada[0], b_ada[0])
    mod3 = mod.reshape(3 * B, 1, D)

    w_t = jnp.swapaxes(w_in, 1, 2)[0]
    a0 = GATE_COL0 + 2 * M_HEADS
    wa = w_t[:GATE_COL0].astype(BF16)
    wb = w_t[a0:].astype(BF16)
    w_gate = jnp.pad(w_t[GATE_COL0:a0], ((0, GATE_PAD - 2 * M_HEADS), (0, 0))).astype(BF16)
    conv_w = jnp.stack([conv_q_w[0], conv_k_w[0]])
    p, og, gates = _inproj(x2d, mod3, norm_w, wa, wb, w_gate, conv_w, B, S)
    p3 = p.reshape(B, S, P_COLS)
    gates3 = gates.reshape(B, S, GATE_PAD)

    gbias = jnp.pad(jnp.concatenate([b_i[0], b_f[0]]), (0, GATE_PAD - 2 * M_HEADS)).reshape(1, GATE_PAD)
    hm = _mlstm(p3, og.reshape(B, S, M_WIDTH), gates3, m_norm_w, gbias)

    lam_p = jnp.stack([lambda_q1[0], lambda_k1[0], lambda_q2[0], lambda_k2[0]])
    ha = _attn(p3, rel_bias, lam_p, a_norm_w)

    out = _outproj(hm.reshape(B * S, M_WIDTH), ha.reshape(B * S, A_WIDTH), x2d, mod3,
                   w_out[0].astype(BF16), final_norm_w.reshape(1, D), B, S)
    return out.reshape(B, S, D)
```
